```python
import jax
import jax.numpy as jnp
from jax import lax
import numpy as np

D_MODEL = 1024
BATCH = 4
SEQ = 4096
DEPTH = 2
DEC_BATCH = 32
DEC_SEQ = 1
PAST_LEN = 16384
PAGE_SIZE = 128

N_EVEN = (DEPTH + 1) // 2
N_ODD = DEPTH // 2
CONV_DIM = D_MODEL // 2
CONV_WIDTH = 31
ATT_HEADS = 8
ATT_HD = 64
ATT_DIM = ATT_HEADS * ATT_HD
IN_DIM = 2 * CONV_DIM + 3 * ATT_DIM
MOBA_BLOCK = 256
MOBA_TOPK = 3
Q_CHUNK = 32
ROPE_THETA = 10000.0
MEM_TOKENS = 256
MEM_HEADS = 4
MEM_HD = 128
MEM_DIM = MEM_HEADS * MEM_HD
RWKV_HD = 64
RWKV_HEADS = D_MODEL // RWKV_HD
DECAY_LORA = 64
AAA_LORA = 64
GATE_LORA = 160
D_FF = 4 * D_MODEL
NORM_EPS = 1e-6
LN_EPS = 1e-5
GN_EPS = 64e-5
NEG_INF = -1e30

kernel_name = 'hybrid_conv_moba_rwkv7_decoder_step'


def rms_norm(x, g):
    xf = x.astype(jnp.float32)
    y = xf * lax.rsqrt(jnp.mean(xf * xf, axis=-1, keepdims=True) + NORM_EPS)
    return (y * g.astype(jnp.float32)).astype(x.dtype)


def layer_norm(x, g, b, eps):
    xf = x.astype(jnp.float32)
    mu = jnp.mean(xf, axis=-1, keepdims=True)
    var = jnp.mean(jnp.square(xf - mu), axis=-1, keepdims=True)
    return ((xf - mu) * lax.rsqrt(var + eps) * g.astype(jnp.float32) + b.astype(jnp.float32)).astype(x.dtype)


def rotary(x, pos):
    half = x.shape[-1] // 2
    inv_freq = ROPE_THETA ** (-jnp.arange(half, dtype=jnp.float32) / half)
    ang = pos.astype(jnp.float32)[:, None] * inv_freq[None, :]
    cos = jnp.cos(ang)[None, :, None, :]
    sin = jnp.sin(ang)[None, :, None, :]
    xf = x.astype(jnp.float32)
    x1, x2 = xf[..., :half], xf[..., half:]
    return jnp.concatenate([x1 * cos - x2 * sin, x2 * cos + x1 * sin], axis=-1).astype(x.dtype)


def causal_depthwise_conv(u, buf, w, b):
    full = jnp.concatenate([buf.astype(u.dtype), u], axis=1)
    y = lax.conv_general_dilated(full, w[:, None, :].astype(u.dtype), (1,), 'VALID',
                                 dimension_numbers=('NWC', 'WIO', 'NWC'),
                                 feature_group_count=u.shape[-1])
    return y + b, full[:, -(CONV_WIDTH - 1):]


def moba_attention(q, k, v, q_offset):
    B, Sq, H, hd = q.shape
    L = k.shape[1]
    nb = -(-L // MOBA_BLOCK)
    pad_k = nb * MOBA_BLOCK - L
    kb = jnp.pad(k, ((0, 0), (0, pad_k), (0, 0), (0, 0))).reshape(B, nb, MOBA_BLOCK, H, hd)
    vb = jnp.pad(v, ((0, 0), (0, pad_k), (0, 0), (0, 0))).reshape(B, nb, MOBA_BLOCK, H, hd)
    k_mean = jnp.mean(kb.astype(jnp.float32), axis=2)
    n_sel = min(MOBA_TOPK, nb)
    C = min(Q_CHUNK, Sq)
    n_chunks = -(-Sq // C)
    q_p = jnp.pad(q, ((0, 0), (0, n_chunks * C - Sq), (0, 0), (0, 0)))
    q_chunks = jnp.swapaxes(q_p.reshape(B, n_chunks, C, H, hd), 0, 1)
    t_chunks = (q_offset + jnp.arange(n_chunks * C, dtype=jnp.int32)).reshape(n_chunks, C)
    b_idx = jnp.arange(B)[:, None, None, None]
    h_idx = jnp.arange(H)[None, :, None, None]
    blk_ids = jnp.arange(nb, dtype=jnp.int32)
    offs = jnp.arange(MOBA_BLOCK, dtype=jnp.int32)
    scale = hd ** -0.5

    def attend_chunk(args):
        qi, ti = args
        own = ti // MOBA_BLOCK
        past = blk_ids[None, :] < own[:, None]
        gate = jnp.einsum('bchd,bnhd->bhcn', qi.astype(jnp.float32), k_mean)
        gate = jnp.where(past[None, None], gate, NEG_INF)
        _, sel = lax.top_k(gate, n_sel)
        own_idx = jnp.broadcast_to(jnp.minimum(own, nb - 1)[None, None, :, None], (B, H, C, 1))
        idx = jnp.concatenate([sel.astype(jnp.int32), own_idx], axis=-1)
        ks = kb[b_idx, idx, :, h_idx]
        vs = vb[b_idx, idx, :, h_idx]
        sel_ok = jnp.concatenate([jnp.arange(n_sel)[None, :] < own[:, None],
                                  jnp.ones((C, 1), dtype=jnp.bool_)], axis=-1)
        key_pos = idx[..., None] * MOBA_BLOCK + offs
        ok = (key_pos <= ti[None, None, :, None, None]) & sel_ok[None, None, :, :, None]
        s = jnp.einsum('bchd,bhcjnd->bhcjn', qi, ks).astype(jnp.float32) * scale
        s = jnp.where(ok, s, NEG_INF).reshape(B, H, C, -1)
        pr = jax.nn.softmax(s, axis=-1).astype(vs.dtype)
        return jnp.einsum('bhcm,bhcmd->bchd', pr, vs.reshape(B, H, C, -1, hd)).astype(q.dtype)

    out = lax.map(attend_chunk, (q_chunks, t_chunks))
    return jnp.swapaxes(out, 0, 1).reshape(B, n_chunks * C, H, hd)[:, :Sq]


def rwkv7_time_mix(x, x_last, s0, o, p):
    B, T, D = x.shape
    H, N = RWKV_HEADS, RWKV_HD
    x_prev = jnp.concatenate([x_last[:, None, :].astype(x.dtype), x[:, :-1]], axis=1)
    xx = x_prev - x
    mu = p['rwkv_mu'][o]
    xr, xw, xk, xv, xa, xg = (x + xx * mu[i] for i in range(6))
    r = xr @ p['rwkv_w_r'][o]
    k = xk @ p['rwkv_w_k'][o]
    v = xv @ p['rwkv_w_v'][o]
    w_log = -jax.nn.softplus(-(p['rwkv_w0'][o] + jnp.tanh(xw @ p['rwkv_w1'][o]) @ p['rwkv_w2'][o])) - 0.5
    a = jax.nn.sigmoid(p['rwkv_a0'][o] + (xa @ p['rwkv_a1'][o]) @ p['rwkv_a2'][o])
    g = jax.nn.sigmoid(xg @ p['rwkv_g1'][o]) @ p['rwkv_g2'][o]

    def heads(t):
        return t.reshape(B, T, H, N).astype(jnp.float32)

    r, k, v, a, w_log = heads(r), heads(k), heads(v), heads(a), heads(w_log)
    kk = k * p['rwkv_k_k'][o].reshape(H, N).astype(jnp.float32)
    kk = kk / jnp.maximum(jnp.sqrt(jnp.sum(kk * kk, axis=-1, keepdims=True)), 1e-12)
    k = k * (1.0 + (a - 1.0) * p['rwkv_k_a'][o].reshape(H, N).astype(jnp.float32))
    decay = jnp.exp(-jnp.exp(w_log))

    def step(state, inp):
        r_t, d_t, k_t, v_t, kk_t, b_t = inp
        sa = jnp.einsum('bhvk,bhk->bhv', state, -kk_t)
        state = (state * d_t[:, :, None, :] + sa[..., None] * b_t[:, :, None, :]
                 + v_t[..., None] * k_t[:, :, None, :])
        return state, jnp.einsum('bhvk,bhk->bhv', state, r_t)

    xs = tuple(jnp.swapaxes(t, 0, 1) for t in (r, decay, k, v, kk, kk * a))
    s_T, ys = lax.scan(step, s0.astype(jnp.float32), xs)
    y = jnp.swapaxes(ys, 0, 1)
    mean = jnp.mean(y, axis=-1, keepdims=True)
    var = jnp.mean(jnp.square(y - mean), axis=-1, keepdims=True)
    y = ((y - mean) * lax.rsqrt(var + GN_EPS)).reshape(B, T, D)
    y = y * p['rwkv_lnx_g'][o].astype(jnp.float32) + p['rwkv_lnx_b'][o].astype(jnp.float32)
    bonus = (jnp.sum(r * k * p['rwkv_r_k'][o].astype(jnp.float32), axis=-1, keepdims=True) * v).reshape(B, T, D)
    out = ((y + bonus) * g.astype(jnp.float32)).astype(x.dtype) @ p['rwkv_w_o'][o]
    return out, x[:, -1], s_T.astype(s0.dtype)


def cross_attend(hn, mk, mv, w_q, w_o):
    B, S, _ = hn.shape
    q = (hn @ w_q).reshape(B, S, MEM_HEADS, MEM_HD)
    s = jnp.einsum('bshd,bmhd->bhsm', q, mk).astype(jnp.float32) * (MEM_HD ** -0.5)
    pr = jax.nn.softmax(s, axis=-1).astype(mv.dtype)
    o = jnp.einsum('bhsm,bmhd->bshd', pr, mv).reshape(B, S, MEM_DIM)
    return o @ w_o


def trunk(x, pos0, mem_kv, past_kv, conv_bufs, shift_states, wkv_states, p):
    B, S, _ = x.shape
    pos = pos0 + jnp.arange(S, dtype=jnp.int32)
    h = x
    new_kv, new_conv, new_shift, new_wkv = [], [], [], []
    for l in range(DEPTH):
        g = p['norm_gain'][l]
        hn = rms_norm(h, g[0])
        if l % 2 == 0:
            e = l // 2
            z = hn @ p['w_in'][e]
            c0 = CONV_DIM
            u_val, u_gate = z[..., :c0], z[..., c0:2 * c0]
            q = z[..., 2 * c0:2 * c0 + ATT_DIM]
            k = z[..., 2 * c0 + ATT_DIM:2 * c0 + 2 * ATT_DIM]
            v = z[..., 2 * c0 + 2 * ATT_DIM:]
            u = u_val * jax.nn.sigmoid(u_gate)
            c, buf = causal_depthwise_conv(u, conv_bufs[e], p['conv_w'][e], p['conv_b'][e])
            c = jax.nn.silu(layer_norm(c, p['conv_ln_g'][e], p['conv_ln_b'][e], LN_EPS))
            q = rotary(q.reshape(B, S, ATT_HEADS, ATT_HD), pos)
            k = rotary(k.reshape(B, S, ATT_HEADS, ATT_HD), pos)
            v = v.reshape(B, S, ATT_HEADS, ATT_HD)
            if past_kv is None:
                k_all, v_all = k, v
            else:
                k_all = jnp.concatenate([past_kv[e][0].astype(k.dtype), k], axis=1)
                v_all = jnp.concatenate([past_kv[e][1].astype(v.dtype), v], axis=1)
            att = moba_attention(q, k_all, v_all, pos0).reshape(B, S, ATT_DIM)
            mix = jnp.concatenate([c, att], axis=-1) @ p['w_out'][e]
            new_kv.append((k, v))
            new_conv.append(buf)
        else:
            o = l // 2
            mix, last, st = rwkv7_time_mix(hn, shift_states[o], wkv_states[o], o, p)
            new_shift.append(last)
            new_wkv.append(st)
        h = h + rms_norm(mix, g[1])
        hn = rms_norm(h, g[2])
        h = h + rms_norm(cross_attend(hn, mem_kv[l][0], mem_kv[l][1], p['w_cq'][l], p['w_co'][l]), g[3])
        hn = rms_norm(h, g[4])
        up = jnp.square(jax.nn.relu(hn @ p['w_up'][l]))
        h = h + rms_norm(up @ p['w_down'][l], g[5])
    return h, new_kv, new_conv, new_shift, new_wkv


def setup_inputs(seed: int = 0) -> dict:
    key = jax.random.key(seed)
    keys = iter(jax.random.split(key, 64))

    def nrm(shape, scale):
        return jax.random.normal(next(keys), shape, jnp.float32) * scale

    n_pages = PAST_LEN // PAGE_SIZE
    n_used = DEC_BATCH * n_pages
    n_phys = n_used + max(1, n_used // 4)
    D = D_MODEL
    H, N = RWKV_HEADS, RWKV_HD
    inp = {}
    inp['x_prompt'] = nrm((BATCH, SEQ, D), 1.0)
    inp['x_sample'] = nrm((DEC_BATCH, DEC_SEQ, D), 1.0)
    inp['mem_prompt'] = nrm((BATCH, MEM_TOKENS, D), 1.0)
    inp['cache_k'] = nrm((N_EVEN, n_phys, PAGE_SIZE, ATT_HEADS, ATT_HD), 1.0)
    inp['cache_v'] = nrm((N_EVEN, n_phys, PAGE_SIZE, ATT_HEADS, ATT_HD), 1.0)
    inp['page_table'] = jax.random.permutation(next(keys), n_phys)[:n_used].reshape(DEC_BATCH, n_pages).astype(jnp.int32)
    inp['cache_mem_k'] = nrm((DEPTH, DEC_BATCH, MEM_TOKENS, MEM_HEADS, MEM_HD), 1.0)
    inp['cache_mem_v'] = nrm((DEPTH, DEC_BATCH, MEM_TOKENS, MEM_HEADS, MEM_HD), 1.0)
    inp['state_conv'] = nrm((N_EVEN, DEC_BATCH, CONV_WIDTH - 1, CONV_DIM), 1.0)
    inp['state_shift'] = nrm((N_ODD, DEC_BATCH, D), 1.0)
    inp['state_wkv'] = nrm((N_ODD, DEC_BATCH, H, N, N), 0.1)
    inp['norm_gain'] = 1.0 + nrm((DEPTH, 6, D), 0.05)
    inp['w_in'] = nrm((N_EVEN, D, IN_DIM), D ** -0.5)
    inp['conv_w'] = nrm((N_EVEN, CONV_WIDTH, CONV_DIM), CONV_WIDTH ** -0.5)
    inp['conv_b'] = nrm((N_EVEN, CONV_DIM), 0.01)
    inp['conv_ln_g'] = 1.0 + nrm((N_EVEN, CONV_DIM), 0.05)
    inp['conv_ln_b'] = nrm((N_EVEN, CONV_DIM), 0.01)
    inp['w_out'] = nrm((N_EVEN, CONV_DIM + ATT_DIM, D), (CONV_DIM + ATT_DIM) ** -0.5)
    inp['rwkv_mu'] = jax.random.uniform(next(keys), (N_ODD, 6, D), jnp.float32)
    inp['rwkv_w_r'] = nrm((N_ODD, D, D), D ** -0.5)
    inp['rwkv_w_k'] = nrm((N_ODD, D, D), D ** -0.5)
    inp['rwkv_w_v'] = nrm((N_ODD, D, D), D ** -0.5)
    inp['rwkv_w_o'] = nrm((N_ODD, D, D), D ** -0.5)
    inp['rwkv_w0'] = nrm((N_ODD, D), 0.5) - 0.5
    inp['rwkv_w1'] = nrm((N_ODD, D, DECAY_LORA), D ** -0.5)
    inp['rwkv_w2'] = nrm((N_ODD, DECAY_LORA, D), 0.1 * DECAY_LORA ** -0.5)
    inp['rwkv_a0'] = nrm((N_ODD, D), 0.1)
    inp['rwkv_a1'] = nrm((N_ODD, D, AAA_LORA), D ** -0.5)
    inp['rwkv_a2'] = nrm((N_ODD, AAA_LORA, D), 0.1 * AAA_LORA ** -0.5)
    inp['rwkv_g1'] = nrm((N_ODD, D, GATE_LORA), D ** -0.5)
    inp['rwkv_g2'] = nrm((N_ODD, GATE_LORA, D), GATE_LORA ** -0.5)
    inp['rwkv_k_k'] = 0.85 + nrm((N_ODD, D), 0.05)
    inp['rwkv_k_a'] = 1.0 + nrm((N_ODD, D), 0.05)
    inp['rwkv_r_k'] = nrm((N_ODD, H, N), 0.1)
    inp['rwkv_lnx_g'] = 1.0 + nrm((N_ODD, D), 0.05)
    inp['rwkv_lnx_b'] = nrm((N_ODD, D), 0.01)
    inp['w_cq'] = nrm((DEPTH, D, MEM_DIM), D ** -0.5)
    inp['w_ck'] = nrm((DEPTH, D, MEM_DIM), D ** -0.5)
    inp['w_cv'] = nrm((DEPTH, D, MEM_DIM), D ** -0.5)
    inp['w_co'] = nrm((DEPTH, MEM_DIM, D), MEM_DIM ** -0.5)
    inp['w_up'] = nrm((DEPTH, D, D_FF), D ** -0.5)
    inp['w_down'] = nrm((DEPTH, D_FF, D), D_FF ** -0.5)
    return inp


def reference(x_prompt, x_sample, mem_prompt, cache_k, cache_v, page_table, cache_mem_k, cache_mem_v,
              state_conv, state_shift, state_wkv, norm_gain, w_in, conv_w, conv_b, conv_ln_g, conv_ln_b,
              w_out, rwkv_mu, rwkv_w_r, rwkv_w_k, rwkv_w_v, rwkv_w_o, rwkv_w0, rwkv_w1, rwkv_w2,
              rwkv_a0, rwkv_a1, rwkv_a2, rwkv_g1, rwkv_g2, rwkv_k_k, rwkv_k_a, rwkv_r_k, rwkv_lnx_g,
              rwkv_lnx_b, w_cq, w_ck, w_cv, w_co, w_up, w_down):
    p = dict(norm_gain=norm_gain, w_in=w_in, conv_w=conv_w, conv_b=conv_b, conv_ln_g=conv_ln_g,
             conv_ln_b=conv_ln_b, w_out=w_out, rwkv_mu=rwkv_mu, rwkv_w_r=rwkv_w_r, rwkv_w_k=rwkv_w_k,
             rwkv_w_v=rwkv_w_v, rwkv_w_o=rwkv_w_o, rwkv_w0=rwkv_w0, rwkv_w1=rwkv_w1, rwkv_w2=rwkv_w2,
             rwkv_a0=rwkv_a0, rwkv_a1=rwkv_a1, rwkv_a2=rwkv_a2, rwkv_g1=rwkv_g1, rwkv_g2=rwkv_g2,
             rwkv_k_k=rwkv_k_k, rwkv_k_a=rwkv_k_a, rwkv_r_k=rwkv_r_k, rwkv_lnx_g=rwkv_lnx_g,
             rwkv_lnx_b=rwkv_lnx_b, w_cq=w_cq, w_co=w_co, w_up=w_up, w_down=w_down)
    Bp = x_prompt.shape[0]
    Bd = x_sample.shape[0]
    past_len = page_table.shape[1] * cache_k.shape[2]

    mem_kv_p = [((mem_prompt @ w_ck[l]).reshape(Bp, -1, MEM_HEADS, MEM_HD),
                 (mem_prompt @ w_cv[l]).reshape(Bp, -1, MEM_HEADS, MEM_HD)) for l in range(DEPTH)]
    conv0 = [jnp.zeros((Bp, CONV_WIDTH - 1, CONV_DIM), x_prompt.dtype) for _ in range(N_EVEN)]
    shift0 = [jnp.zeros((Bp, D_MODEL), x_prompt.dtype) for _ in range(N_ODD)]
    wkv0 = [jnp.zeros((Bp, RWKV_HEADS, RWKV_HD, RWKV_HD), state_wkv.dtype) for _ in range(N_ODD)]
    y_prompt, kv_p, conv_p, shift_p, wkv_p = trunk(x_prompt, 0, mem_kv_p, None, conv0, shift0, wkv0, p)

    past_kv_s = [(cache_k[e][page_table].reshape(Bd, past_len, ATT_HEADS, ATT_HD),
                  cache_v[e][page_table].reshape(Bd, past_len, ATT_HEADS, ATT_HD)) for e in range(N_EVEN)]
    mem_kv_s = [(cache_mem_k[l], cache_mem_v[l]) for l in range(DEPTH)]
    y_sample, kv_s, conv_s, shift_s, wkv_s = trunk(
        x_sample, past_len, mem_kv_s, past_kv_s,
        [state_conv[e] for e in range(N_EVEN)],
        [state_shift[o] for o in range(N_ODD)],
        [state_wkv[o] for o in range(N_ODD)], p)

    k_prompt = jnp.stack([kv[0] for kv in kv_p])
    v_prompt = jnp.stack([kv[1] for kv in kv_p])
    k_sample = jnp.stack([kv[0] for kv in kv_s])
    v_sample = jnp.stack([kv[1] for kv in kv_s])
    conv_prompt = jnp.stack(conv_p)
    conv_sample = jnp.stack(conv_s)
    shift_prompt = jnp.stack(shift_p)
    shift_sample = jnp.stack(shift_s)
    wkv_prompt = jnp.stack(wkv_p)
    wkv_sample = jnp.stack(wkv_s)
    mem_k_prompt = jnp.stack([kv[0] for kv in mem_kv_p])
    mem_v_prompt = jnp.stack([kv[1] for kv in mem_kv_p])
    return (y_prompt, y_sample, k_prompt, v_prompt, k_sample, v_sample, conv_prompt, conv_sample,
            shift_prompt, shift_sample, wkv_prompt, wkv_sample, mem_k_prompt, mem_v_prompt)
```

```python
import functools

import jax
import jax.numpy as jnp
from jax import lax
from jax.experimental import pallas as pl
from jax.experimental.pallas import tpu as pltpu

F32 = jnp.float32
BF16 = jnp.bfloat16

NORM_EPS = 1e-6
LN_EPS = 1e-5
GN_EPS = 64e-5
NEG_INF = -1e30
ATT_HD = 64
MEM_HEADS = 4
MEM_HD = 128
RWKV_HD = 64
MOBA_BLOCK = 256
MOBA_TOPK = 3
CONV_WIDTH = 31
ROPE_THETA = 10000.0

LANES = 128
SUBLANES = 8
MXU_DIM = 256
VMEM_LIMIT_BYTES = 56 * 1024 * 1024
SCAN_CHUNK = 64
SCAN_GROUP = MXU_DIM // RWKV_HD
HALO = 32


def _cparams(sem):
    return pltpu.CompilerParams(dimension_semantics=sem, vmem_limit_bytes=VMEM_LIMIT_BYTES)


def _const_spec(shape):
    nd = len(shape)
    return pl.BlockSpec(shape, lambda *_: (0,) * nd, pipeline_mode=pl.Buffered(1))


def _rms(x, g):
    return x * lax.rsqrt(jnp.mean(x * x, axis=-1, keepdims=True) + NORM_EPS) * g


def _sigmoid(x):
    return 1.0 / (1.0 + jnp.exp(-x))


def _mm(a, b):
    return jnp.dot(a.astype(BF16), b.astype(BF16), preferred_element_type=F32)


def _mm_nt(a, b):
    return lax.dot_general(a.astype(BF16), b.astype(BF16), (((1,), (1,)), ((), ())),
                           preferred_element_type=F32)


def _mm_tn(a, b):
    return lax.dot_general(a.astype(BF16), b.astype(BF16), (((0,), (0,)), ((), ())),
                           preferred_element_type=F32)


def _split(x):
    hi = x.astype(BF16)
    lo = (x - hi.astype(F32)).astype(BF16)
    return hi, lo


def _mm2(a, b):
    hi, lo = _split(a)
    return (jnp.dot(hi, b, preferred_element_type=F32)
            + jnp.dot(lo, b, preferred_element_type=F32))


def _proj_kernel(x_ref, w_ref, o_ref):
    o_ref[...] = _mm(x_ref[...], w_ref[...])


def _proj(x, w, tm):
    m, kdim = x.shape
    n = w.shape[1]
    return pl.pallas_call(
        _proj_kernel,
        out_shape=jax.ShapeDtypeStruct((m, n), F32),
        grid=(m // tm,),
        in_specs=[pl.BlockSpec((tm, kdim), lambda i: (i, 0)), _const_spec((kdim, n))],
        out_specs=pl.BlockSpec((tm, n), lambda i: (i, 0)),
        compiler_params=_cparams(("parallel",)),
        name="proj",
    )(x, w)


def _rope(x, cos, sin):
    width = x.shape[1]
    lane = lax.broadcasted_iota(jnp.int32, x.shape, 1)
    first_half = (lane % ATT_HD) < (ATT_HD // 2)
    partner = jnp.where(first_half,
                        pltpu.roll(x, width - ATT_HD // 2, axis=1),
                        pltpu.roll(x, ATT_HD // 2, axis=1))
    return x * cos + partner * sin


def _front0_kernel(x_ref, g_ref, w_ref, cos_ref, sin_ref, u_ref, q_ref, k_ref, v_ref, *, conv_dim, att_dim):
    hn = _rms(x_ref[...], g_ref[...]).astype(BF16)
    c0 = conv_dim
    val = jnp.dot(hn, w_ref[:, 0:c0], preferred_element_type=F32)
    gate = jnp.dot(hn, w_ref[:, c0:2 * c0], preferred_element_type=F32)
    u_ref[...] = val * _sigmoid(gate)
    reps = att_dim // LANES
    cos = jnp.concatenate([cos_ref[...]] * reps, axis=1)
    sin = jnp.concatenate([sin_ref[...]] * reps, axis=1)
    o = 2 * c0
    q = jnp.dot(hn, w_ref[:, o:o + att_dim], preferred_element_type=F32)
    q_ref[...] = _rope(q, cos, sin)
    k = jnp.dot(hn, w_ref[:, o + att_dim:o + 2 * att_dim], preferred_element_type=F32)
    k_ref[...] = _rope(k, cos, sin)
    v_ref[...] = jnp.dot(hn, w_ref[:, o + 2 * att_dim:o + 3 * att_dim], preferred_element_type=F32)


def _front0(h, g, w_in, cos_tab, sin_tab, conv_dim, att_dim, tm):
    m, d = h.shape
    n_tab = cos_tab.shape[0] // tm
    row = lambda i: (i, 0)
    tab = lambda i: (i % n_tab, 0)
    return pl.pallas_call(
        functools.partial(_front0_kernel, conv_dim=conv_dim, att_dim=att_dim),
        out_shape=(jax.ShapeDtypeStruct((m, conv_dim), F32),) + (jax.ShapeDtypeStruct((m, att_dim), F32),) * 3,
        grid=(m // tm,),
        in_specs=[pl.BlockSpec((tm, d), row), _const_spec((1, d)), _const_spec(w_in.shape),
                  pl.BlockSpec((tm, LANES), tab), pl.BlockSpec((tm, LANES), tab)],
        out_specs=(pl.BlockSpec((tm, conv_dim), row),) + (pl.BlockSpec((tm, att_dim), row),) * 3,
        compiler_params=_cparams(("parallel",)),
        name="front0",
    )(h, g, w_in, cos_tab, sin_tab)


def _rope_tables(pos):
    half = ATT_HD // 2
    inv_freq = ROPE_THETA ** (-jnp.arange(half, dtype=F32) / half)
    ang = pos.astype(F32)[:, None] * inv_freq[None, :]
    cos, sin = jnp.cos(ang), jnp.sin(ang)
    reps = LANES // ATT_HD
    return (jnp.tile(jnp.concatenate([cos, cos], axis=1), (1, reps)),
            jnp.tile(jnp.concatenate([-sin, sin], axis=1), (1, reps)))


def _ln_silu(y, g, b):
    mu = jnp.mean(y, axis=-1, keepdims=True)
    yc = y - mu
    var = jnp.mean(yc * yc, axis=-1, keepdims=True)
    ln = yc * lax.rsqrt(var + LN_EPS) * g + b
    return ln * _sigmoid(ln)


def _conv_kernel(u_ref, uprev_ref, w_ref, b_ref, lg_ref, lb_ref, c_ref, full_scr, *, tc, sub):
    i = pl.program_id(1)
    full_scr[0:HALO, :] = jnp.where(i > 0, uprev_ref[0], 0.0)
    full_scr[HALO:HALO + tc, :] = u_ref[0]
    first = HALO - (CONV_WIDTH - 1)
    for r0 in range(0, tc, sub):
        acc = jnp.broadcast_to(b_ref[...], (sub, b_ref.shape[1]))
        for j in range(CONV_WIDTH):
            acc = acc + w_ref[j:j + 1, :] * full_scr[first + r0 + j:first + r0 + j + sub, :]
        c_ref[0, r0:r0 + sub, :] = _ln_silu(acc, lg_ref[...], lb_ref[...])


def _conv_prompt(u, w, b, lg, lb, tc):
    bsz, s, c = u.shape
    per = tc // HALO
    wp = jnp.pad(w, ((0, HALO - CONV_WIDTH), (0, 0)))
    return pl.pallas_call(
        functools.partial(_conv_kernel, tc=tc, sub=HALO),
        out_shape=jax.ShapeDtypeStruct((bsz, s, c), F32),
        grid=(bsz, s // tc),
        in_specs=[pl.BlockSpec((1, tc, c), lambda bi, i: (bi, i, 0)),
                  pl.BlockSpec((1, HALO, c), lambda bi, i: (bi, jnp.maximum(i * per - 1, 0), 0)),
                  _const_spec((HALO, c)), _const_spec((1, c)), _const_spec((1, c)), _const_spec((1, c))],
        out_specs=pl.BlockSpec((1, tc, c), lambda bi, i: (bi, i, 0)),
        scratch_shapes=[pltpu.VMEM((HALO + tc, c), F32)],
        compiler_params=_cparams(("parallel", "parallel")),
        name="conv_prompt",
    )(u, u, wp, b, lg, lb)


def _conv_step_kernel(st_ref, u_ref, w_ref, b_ref, lg_ref, lb_ref, c_ref):
    acc = b_ref[...] + w_ref[CONV_WIDTH - 1:CONV_WIDTH, :] * u_ref[...]
    for j in range(CONV_WIDTH - 1):
        acc = acc + w_ref[j:j + 1, :] * st_ref[j]
    c_ref[...] = _ln_silu(acc, lg_ref[...], lb_ref[...])


def _conv_step(state_t, u, w, b, lg, lb):
    bsz, c = u.shape
    wp = jnp.pad(w, ((0, HALO - CONV_WIDTH), (0, 0)))
    return pl.pallas_call(
        _conv_step_kernel,
        out_shape=jax.ShapeDtypeStruct((bsz, c), F32),
        grid=(1,),
        in_specs=[_const_spec(state_t.shape), _const_spec((bsz, c)), _const_spec((HALO, c)),
                  _const_spec((1, c)), _const_spec((1, c)), _const_spec((1, c))],
        out_specs=pl.BlockSpec((bsz, c), lambda i: (0, 0)),
        compiler_params=_cparams(("arbitrary",)),
        name="conv_step",
    )(state_t, u, wp, b, lg, lb)


def _top_blocks(gate, n_past):
    col = lax.broadcasted_iota(jnp.int32, gate.shape, 1).astype(F32)
    g = jnp.where(col < n_past.astype(F32), gate, NEG_INF)
    picks = []
    for r in range(MOBA_TOPK):
        m = jnp.max(g, axis=1, keepdims=True)
        idx = jnp.min(jnp.where(g == m, col, 1e9), axis=1, keepdims=True)
        picks.append(jnp.where(r < n_past, idx, -1.0))
        g = jnp.where(col == idx, -jnp.inf, g)
    return picks


def _moba_kernel(q_ref, k_ref, v_ref, o_ref, kmean_scr, *, nb, scale):
    i = pl.program_id(2)
    blk = MOBA_BLOCK

    @pl.when(i == 0)
    def _():
        kmean_scr[...] = jnp.zeros_like(kmean_scr)
        for j in range(nb):
            kmean_scr[j:j + 1, :] = jnp.mean(k_ref[0, j * blk:(j + 1) * blk, :], axis=0, keepdims=True)

    q = q_ref[0]
    lane = lax.broadcasted_iota(jnp.int32, q.shape, 1)
    row = lax.broadcasted_iota(jnp.int32, (blk, blk), 0)
    colk = lax.broadcasted_iota(jnp.int32, (blk, blk), 1)
    causal = colk <= row
    start = pl.multiple_of(i * blk, blk)
    k_own = k_ref[0, pl.ds(start, blk), :].astype(BF16)
    v_own = v_ref[0, pl.ds(start, blk), :].astype(BF16)
    km_hi, km_lo = _split(kmean_scr[...])
    out = jnp.zeros(q.shape, F32)
    for hh in range(LANES // ATT_HD):
        in_head = (lane >= hh * ATT_HD) & (lane < (hh + 1) * ATT_HD)
        qh = jnp.where(in_head, q, 0.0)
        q_hi, q_lo = _split(qh)
        nt = (((1,), (1,)), ((), ()))
        gate = (lax.dot_general(q_hi, km_hi, nt, preferred_element_type=F32)
                + lax.dot_general(q_hi, km_lo, nt, preferred_element_type=F32)
                + lax.dot_general(q_lo, km_hi, nt, preferred_element_type=F32))
        picks = _top_blocks(gate, i)

        s = lax.dot_general(q_hi, k_own, nt, preferred_element_type=F32) * scale
        s = jnp.where(causal, s, NEG_INF)
        m0 = jnp.max(s, axis=1, keepdims=True)
        p = jnp.exp(s - m0)
        l0 = jnp.sum(p, axis=1, keepdims=True)
        acc0 = jnp.dot(p.astype(BF16), v_own, preferred_element_type=F32)

        def body(j, carry, q_hi=q_hi, picks=picks):
            m, l, acc = carry
            off = pl.multiple_of(j * blk, blk)
            kj = k_ref[0, pl.ds(off, blk), :].astype(BF16)
            vj = v_ref[0, pl.ds(off, blk), :].astype(BF16)
            jf = j.astype(F32)
            chosen = (picks[0] == jf) | (picks[1] == jf) | (picks[2] == jf)
            sj = lax.dot_general(q_hi, kj, nt, preferred_element_type=F32) * scale
            sj = jnp.where(chosen, sj, NEG_INF)
            m_new = jnp.maximum(m, jnp.max(sj, axis=1, keepdims=True))
            alpha = jnp.exp(m - m_new)
            pj = jnp.exp(sj - m_new)
            l = alpha * l + jnp.sum(pj, axis=1, keepdims=True)
            acc = alpha * acc + jnp.dot(pj.astype(BF16), vj, preferred_element_type=F32)
            return m_new, l, acc

        _, l, acc = lax.fori_loop(0, i, body, (m0, l0, acc0))
        out = jnp.where(in_head, acc / l, out)
    o_ref[0] = out


def _moba_prompt(q, k, v):
    bsz, s, a = q.shape
    nb = s // MOBA_BLOCK
    blk = pl.BlockSpec((1, MOBA_BLOCK, LANES), lambda b, hp, i: (b, i, hp))
    seq = pl.BlockSpec((1, s, LANES), lambda b, hp, i: (b, 0, hp))
    return pl.pallas_call(
        functools.partial(_moba_kernel, nb=nb, scale=ATT_HD ** -0.5),
        out_shape=jax.ShapeDtypeStruct((bsz, s, a), F32),
        grid=(bsz, a // LANES, nb),
        in_specs=[blk, seq, seq],
        out_specs=blk,
        scratch_shapes=[pltpu.VMEM((LANES, LANES), F32)],
        compiler_params=_cparams(("parallel", "parallel", "arbitrary")),
        name="moba_prompt",
    )(q, k, v)


def _moba_gate_kernel(pt_ref, q_ref, k0_ref, k1_ref, sel_ref, prod_scr, *, nb):
    del pt_ref
    j = pl.program_id(1)
    ksum = (jnp.sum(k0_ref[0], axis=0, keepdims=True) + jnp.sum(k1_ref[0], axis=0, keepdims=True))
    prod_scr[pl.ds(j, 1), :] = q_ref[0] * (ksum * (1.0 / MOBA_BLOCK))

    @pl.when(j == nb - 1)
    def _():
        prod = prod_scr[...]
        a = prod.shape[1]
        feat = lax.broadcasted_iota(jnp.int32, (a, LANES), 0) // ATT_HD
        head = lax.broadcasted_iota(jnp.int32, (a, LANES), 1)
        seg = jnp.where(feat == head, 1.0, 0.0).astype(BF16)
        hi, lo = _split(prod)
        mid, lo = _split(prod - hi.astype(F32))
        gate = (jnp.dot(hi, seg, preferred_element_type=F32) + jnp.dot(mid, seg, preferred_element_type=F32)
                + jnp.dot(lo, seg, preferred_element_type=F32))
        blk = lax.broadcasted_iota(jnp.int32, gate.shape, 0).astype(F32)
        rows = []
        for _ in range(MOBA_TOPK):
            m = jnp.max(gate, axis=0, keepdims=True)
            idx = jnp.min(jnp.where(gate == m, blk, 1e9), axis=0, keepdims=True)
            rows.append(idx)
            gate = jnp.where(blk == idx, -jnp.inf, gate)
        rows += [jnp.zeros_like(rows[0])] * (SUBLANES - MOBA_TOPK)
        sel_ref[0] = jnp.concatenate(rows, axis=0).astype(jnp.int32)


def _moba_gate(page_table, q, cache_k, nb, pages_per_block):
    bsz, a = q.shape
    n_phys, page, _ = cache_k.shape
    assert pages_per_block == 2
    q3 = q.reshape(bsz, 1, a)
    grid_spec = pltpu.PrefetchScalarGridSpec(
        num_scalar_prefetch=1,
        grid=(bsz, nb),
        in_specs=[pl.BlockSpec((1, 1, a), lambda b, j, pt: (b, 0, 0)),
                  pl.BlockSpec((1, page, a), lambda b, j, pt: (pt[b, 2 * j], 0, 0)),
                  pl.BlockSpec((1, page, a), lambda b, j, pt: (pt[b, 2 * j + 1], 0, 0))],
        out_specs=pl.BlockSpec((1, SUBLANES, LANES), lambda b, j, pt: (b, 0, 0)),
        scratch_shapes=[pltpu.VMEM((nb, a), F32)],
    )
    return pl.pallas_call(
        functools.partial(_moba_gate_kernel, nb=nb),
        out_shape=jax.ShapeDtypeStruct((bsz, SUBLANES, LANES), jnp.int32),
        grid_spec=grid_spec,
        compiler_params=_cparams(("parallel", "arbitrary")),
        name="moba_gate",
    )(page_table, q3, cache_k, cache_k)


def _moba_decode_kernel(pt_ref, sel_ref, q_ref, kn_ref, vn_ref, k0_ref, k1_ref, v0_ref, v1_ref, o_ref,
                        m_scr, l_scr, acc_scr, *, scale):
    del pt_ref, sel_ref
    h = pl.program_id(1)
    r = pl.program_id(2)
    lane = lax.broadcasted_iota(jnp.int32, (1, LANES), 1)
    half = h % (LANES // ATT_HD)
    in_head = (lane >= half * ATT_HD) & (lane < (half + 1) * ATT_HD)
    qh = jnp.where(in_head, q_ref[0], 0.0)

    @pl.when(r == 0)
    def _():
        m_scr[...] = jnp.sum(qh * kn_ref[0], axis=1, keepdims=True) * scale
        l_scr[...] = jnp.ones_like(l_scr)
        acc_scr[...] = vn_ref[0]

    kb = jnp.concatenate([k0_ref[0], k1_ref[0]], axis=0)
    vb = jnp.concatenate([v0_ref[0], v1_ref[0]], axis=0)
    s = jnp.sum(kb * qh, axis=1, keepdims=True) * scale
    m_old = m_scr[...]
    m_new = jnp.maximum(m_old, jnp.max(s, axis=0, keepdims=True))
    alpha = jnp.exp(m_old - m_new)
    p = jnp.exp(s - m_new)
    l_scr[...] = alpha * l_scr[...] + jnp.sum(p, axis=0, keepdims=True)
    acc_scr[...] = alpha * acc_scr[...] + jnp.sum(p * vb, axis=0, keepdims=True)
    m_scr[...] = m_new

    @pl.when(r == MOBA_TOPK - 1)
    def _():
        res = acc_scr[...] / l_scr[...]
        keep = jnp.where(half == 0, jnp.zeros_like(res), o_ref[0])
        o_ref[0] = jnp.where(in_head, res, keep)


def _moba_decode(page_table, sel, q, k_new, v_new, cache_k, cache_v):
    bsz, a = q.shape
    heads = a // ATT_HD
    n_phys, page, _ = cache_k.shape
    per = LANES // ATT_HD
    sel = sel[:, :MOBA_TOPK, :heads]
    vec = pl.BlockSpec((1, 1, LANES), lambda b, h, r, pt, sl: (b, 0, h // per))

    def page_spec(which):
        return pl.BlockSpec((1, page, LANES),
                            lambda b, h, r, pt, sl: (pt[b, 2 * sl[b, r, h] + which], 0, h // per))

    grid_spec = pltpu.PrefetchScalarGridSpec(
        num_scalar_prefetch=2,
        grid=(bsz, heads, MOBA_TOPK),
        in_specs=[vec, vec, vec, page_spec(0), page_spec(1), page_spec(0), page_spec(1)],
        out_specs=vec,
        scratch_shapes=[pltpu.VMEM((1, 1), F32), pltpu.VMEM((1, 1), F32), pltpu.VMEM((1, LANES), F32)],
    )
    out = pl.pallas_call(
        functools.partial(_moba_decode_kernel, scale=ATT_HD ** -0.5),
        out_shape=jax.ShapeDtypeStruct((bsz, 1, a), F32),
        grid_spec=grid_spec,
        compiler_params=_cparams(("parallel", "arbitrary", "arbitrary")),
        name="moba_decode",
    )(page_table, sel, q.reshape(bsz, 1, a), k_new.reshape(bsz, 1, a), v_new.reshape(bsz, 1, a),
      cache_k, cache_k, cache_v, cache_v)
    return out.reshape(bsz, a)


def _tail_kernel(*refs, n_mix, nb, s, ff_chunk):
    h_ref = refs[0]
    mix_refs = refs[1:1 + n_mix]
    wmix_refs = refs[1 + n_mix:1 + 2 * n_mix]
    g_ref, wcq_ref, mk_ref, mv_ref, wco_ref, wup_ref, wdn_ref, o_ref, ca_scr = refs[1 + 2 * n_mix:]

    mix = _mm(mix_refs[0][...], wmix_refs[0][...])
    for mr, wr in zip(mix_refs[1:], wmix_refs[1:]):
        mix = mix + _mm(mr[...], wr[...])
    h = h_ref[...] + _rms(mix, g_ref[1:2, :])

    q = _mm(_rms(h, g_ref[2:3, :]), wcq_ref[...])
    rows = max(s, SUBLANES)
    for bi in range(nb):
        qb = q[bi * s:(bi + 1) * s, :]
        if s < rows:
            qb = jnp.broadcast_to(qb[0:1, :], (rows, qb.shape[1]))
        for hh in range(MEM_HEADS):
            sl = slice(hh * MEM_HD, (hh + 1) * MEM_HD)
            sc = _mm_nt(qb[:, sl], mk_ref[bi, :, sl]) * (MEM_HD ** -0.5)
            p = jnp.exp(sc - jnp.max(sc, axis=1, keepdims=True))
            oh = _mm(p, mv_ref[bi, :, sl]) / jnp.sum(p, axis=1, keepdims=True)
            ca_scr[bi * s:(bi + 1) * s, sl] = oh[0:s, :]
    h = h + _rms(_mm(ca_scr[...], wco_ref[...]), g_ref[3:4, :])

    hn = _rms(h, g_ref[4:5, :]).astype(BF16)
    d_ff = wup_ref.shape[1]
    acc = jnp.zeros(h.shape, F32)
    for c0 in range(0, d_ff, ff_chunk):
        up = jnp.dot(hn, wup_ref[:, c0:c0 + ff_chunk], preferred_element_type=F32)
        up = jnp.square(jnp.maximum(up, 0.0))
        acc = acc + jnp.dot(up.astype(BF16), wdn_ref[c0:c0 + ff_chunk, :], preferred_element_type=F32)
    o_ref[...] = h + _rms(acc, g_ref[5:6, :])


def _tail(h, mixes, w_mixes, gains, w_cq, mk, mv, w_co, w_up, w_down, tm, seq_len):
    m, d = h.shape
    s = min(seq_len, tm)
    nb = tm // s
    n_mix = len(mixes)
    mem = mk.shape[1]
    mem_dim = mk.shape[2]
    row = lambda i: (i, 0)
    tiles_per_seq = max(seq_len // tm, 1)
    mem_map = lambda i: (i // tiles_per_seq, 0, 0)
    in_specs = [pl.BlockSpec((tm, d), row)]
    in_specs += [pl.BlockSpec((tm, x.shape[1]), row) for x in mixes]
    in_specs += [_const_spec(w.shape) for w in w_mixes]
    in_specs += [_const_spec(gains.shape), _const_spec(w_cq.shape),
                 pl.BlockSpec((nb, mem, mem_dim), mem_map), pl.BlockSpec((nb, mem, mem_dim), mem_map),
                 _const_spec(w_co.shape), _const_spec(w_up.shape), _const_spec(w_down.shape)]
    return pl.pallas_call(
        functools.partial(_tail_kernel, n_mix=n_mix, nb=nb, s=s, ff_chunk=min(1024, w_up.shape[1])),
        out_shape=jax.ShapeDtypeStruct((m, d), F32),
        grid=(m // tm,),
        in_specs=in_specs,
        out_specs=pl.BlockSpec((tm, d), row),
        scratch_shapes=[pltpu.VMEM((tm, mem_dim), F32)],
        compiler_params=_cparams(("parallel",)),
        name="tail",
    )(h, *mixes, *w_mixes, gains, w_cq, mk, mv, w_co, w_up, w_down)


def _front1_kernel(h_ref, hprev_ref, xlast_ref, g_ref, mu_ref, wr_ref, wk_ref, wv_ref, w0_ref, w1_ref, w2_ref,
                   a0_ref, a1_ref, a2_ref, g1_ref, g2_ref, kk_ref, ka_ref, seg_ref,
                   r_out, ld_out, k_out, v_out, kk_out, a_out, gt_out, sh_out, *, seq, tiles_per_seq):
    i = pl.program_id(0)
    g = g_ref[...]
    hn = _rms(h_ref[...], g)
    if seq:
        prev_row = _rms(hprev_ref[SUBLANES - 1:SUBLANES, :], g)
        prev_row = jnp.where(i % tiles_per_seq == 0, xlast_ref[0], prev_row)
        rowid = lax.broadcasted_iota(jnp.int32, hn.shape, 0)
        xprev = jnp.where(rowid == 0, prev_row, pltpu.roll(hn, 1, axis=0))
        sh_out[0] = hn[hn.shape[0] - 1:, :]
    else:
        xprev = xlast_ref[...]
        sh_out[...] = hn
    xx = xprev - hn
    mix = lambda n: hn + xx * mu_ref[n:n + 1, :]
    r = _mm(mix(0), wr_ref[...])
    k = _mm(mix(2), wk_ref[...])
    v = _mm(mix(3), wv_ref[...])
    wl = w0_ref[...] + _mm(jnp.tanh(_mm(mix(1), w1_ref[...])), w2_ref[...])
    z = -wl
    softplus = jnp.maximum(z, 0.0) + jnp.log(1.0 + jnp.exp(-jnp.abs(z)))
    ld_out[...] = -jnp.exp(-softplus - 0.5)
    a = _sigmoid(a0_ref[...] + _mm(_mm(mix(4), a1_ref[...]), a2_ref[...]))
    gt_out[...] = _mm(_sigmoid(_mm(mix(5), g1_ref[...])), g2_ref[...])
    kk = k * kk_ref[...]
    ss = _mm2(kk * kk, seg_ref[...])
    kk_out[...] = kk / jnp.maximum(jnp.sqrt(ss), 1e-12)
    k_out[...] = k * (1.0 + (a - 1.0) * ka_ref[...])
    r_out[...] = r
    v_out[...] = v
    a_out[...] = a


def _front1(h, x_last, g, p, tm, s):
    m, d = h.shape
    bsz = m // s
    seq = s > 1
    row = lambda i: (i, 0)
    big = jax.ShapeDtypeStruct((m, d), F32)
    if seq:
        tps = s // tm
        per = tm // SUBLANES
        hprev_spec = pl.BlockSpec((SUBLANES, d), lambda i: (jnp.maximum(i * per - 1, 0), 0))
        xlast = x_last.reshape(bsz, 1, d)
        xlast_spec = pl.BlockSpec((1, 1, d), lambda i: (i // tps, 0, 0))
        sh_shape = jax.ShapeDtypeStruct((bsz, 1, d), F32)
        sh_spec = pl.BlockSpec((1, 1, d), lambda i: (i // tps, 0, 0))
        sem = ("arbitrary",)
    else:
        tps = 1
        hprev_spec = pl.BlockSpec((SUBLANES, d), lambda i: (0, 0))
        xlast = x_last
        xlast_spec = pl.BlockSpec((tm, d), row)
        sh_shape = big
        sh_spec = pl.BlockSpec((tm, d), row)
        sem = ("parallel",)
    consts = [g, p["mu"], p["w_r"], p["w_k"], p["w_v"], p["w0"], p["w1"], p["w2"], p["a0"], p["a1"], p["a2"],
              p["g1"], p["g2"], p["k_k"], p["k_a"], p["seg"]]
    outs = pl.pallas_call(
        functools.partial(_front1_kernel, seq=seq, tiles_per_seq=tps),
        out_shape=(big,) * 7 + (sh_shape,),
        grid=(m // tm,),
        in_specs=[pl.BlockSpec((tm, d), row), hprev_spec, xlast_spec] + [_const_spec(c.shape) for c in consts],
        out_specs=(pl.BlockSpec((tm, d), row),) * 7 + (sh_spec,),
        compiler_params=_cparams(sem),
        name="front1",
    )(h, h, xlast, *consts)
    return outs[:7], outs[7].reshape(bsz, d)


def _scan_kernel(r_ref, ld_ref, k_ref, v_ref, kk_ref, a_ref, gt_ref, s0_ref, rk_ref, lg_ref, lb_ref,
                 o_ref, st_ref, s_scr, *, n_chunks):
    c = pl.program_id(2)
    ch = SCAN_CHUNK
    grp = SCAN_GROUP
    w = grp * RWKV_HD
    rows = grp * ch
    rhead = lax.broadcasted_iota(jnp.int32, (rows, w), 0) // ch
    chead = lax.broadcasted_iota(jnp.int32, (rows, w), 1) // RWKV_HD
    bd = rhead == chead
    tile_r = lambda x: jnp.concatenate([x] * grp, axis=0)
    diag = lambda x: jnp.where(bd, tile_r(x), 0.0)

    @pl.when(c == 0)
    def _():
        s0 = s0_ref[0, 0]
        s_scr[...] = jnp.where(bd, jnp.concatenate([s0] * grp, axis=1), 0.0)

    r = r_ref[0]
    k = k_ref[0]
    v = v_ref[0]
    kk = kk_ref[0]
    b = kk * a_ref[0]
    ld = ld_ref[0]

    trow = lax.broadcasted_iota(jnp.int32, ld.shape, 0)
    cum = ld
    sh = 1
    while sh < ch:
        cum = cum + jnp.where(trow >= sh, pltpu.roll(cum, sh, axis=0), 0.0)
        sh *= 2
    clast = cum[ch - 1:ch, :]
    p_in = jnp.exp(cum)
    p_inv = jnp.exp(-cum)
    p_to = jnp.exp(clast - cum)
    rt = r * p_in
    at = -kk * jnp.exp(cum - ld)
    bt = b * p_inv
    kt = k * p_inv

    state = s_scr[...]
    lhs = jnp.concatenate([diag(at), diag(rt)], axis=0)
    rhs = jnp.concatenate([tile_r(bt), tile_r(kt)], axis=0)
    mall = _mm_nt(lhs, rhs)
    tr = lax.broadcasted_iota(jnp.int32, (rows, rows), 0)
    tc = lax.broadcasted_iota(jnp.int32, (rows, rows), 1)
    same = (tr // ch) == (tc // ch)
    strict = same & (tc < tr)
    incl = same & (tc <= tr)
    lmat = jnp.where(strict, mall[0:rows, 0:rows], 0.0)
    mak = jnp.where(strict, mall[0:rows, rows:], 0.0)
    mrb = jnp.where(incl, mall[rows:, 0:rows], 0.0)
    mrk = jnp.where(incl, mall[rows:, rows:], 0.0)

    tinv = jnp.where(tr == tc, 1.0, 0.0) + lmat
    lp = lmat
    n = 2
    while n < ch:
        lp = _mm(lp, lp)
        tinv = tinv + _mm(tinv, lp)
        n *= 2

    vd = diag(v)
    s0t = _mm_nt(lhs, state)
    u = _mm(tinv, s0t[0:rows] + _mm(mak, vd))
    ybd = s0t[rows:] + _mm(mrb, u) + _mm(mrk, vd)
    y = ybd[0:ch]
    for hh in range(1, grp):
        y = y + ybd[hh * ch:(hh + 1) * ch]

    upd = _mm_tn(jnp.concatenate([u, vd], axis=0), jnp.concatenate([diag(b * p_to), diag(k * p_to)], axis=0))
    new_state = state * jnp.exp(clast) + upd
    s_scr[...] = new_state

    @pl.when(c == n_chunks - 1)
    def _():
        acc = new_state[:, 0:RWKV_HD]
        for hh in range(1, grp):
            acc = acc + new_state[:, hh * RWKV_HD:(hh + 1) * RWKV_HD]
        st_ref[0, 0] = acc

    fa = lax.broadcasted_iota(jnp.int32, (w, w), 0) // RWKV_HD
    fb = lax.broadcasted_iota(jnp.int32, (w, w), 1) // RWKV_HD
    seg = jnp.where(fa == fb, 1.0, 0.0).astype(BF16)
    mean = _mm2(y, seg) * (1.0 / RWKV_HD)
    yc = y - mean
    var = _mm2(yc * yc, seg) * (1.0 / RWKV_HD)
    yn = yc * lax.rsqrt(var + GN_EPS) * lg_ref[...] + lb_ref[...]
    bonus = _mm2(r * k * rk_ref[...], seg) * v
    o_ref[0] = (yn + bonus) * gt_ref[0]


def _scan(streams, gate, s0, r_k, lnx_g, lnx_b):
    bsz, t, d = gate.shape
    w = SCAN_GROUP * RWKV_HD
    nq = d // w
    nc = t // SCAN_CHUNK
    tok = pl.BlockSpec((1, SCAN_CHUNK, w), lambda b, q, c: (b, c, q))
    st = pl.BlockSpec((1, 1, w, RWKV_HD), lambda b, q, c: (b, q, 0, 0))
    par = pl.BlockSpec((1, w), lambda b, q, c: (0, q))
    s0g = s0.reshape(bsz, nq, w, RWKV_HD)
    out, s_t = pl.pallas_call(
        functools.partial(_scan_kernel, n_chunks=nc),
        out_shape=(jax.ShapeDtypeStruct((bsz, t, d), F32), jax.ShapeDtypeStruct(s0g.shape, F32)),
        grid=(bsz, nq, nc),
        in_specs=[tok] * 7 + [st, par, par, par],
        out_specs=(tok, st),
        scratch_shapes=[pltpu.VMEM((w, w), F32)],
        compiler_params=_cparams(("parallel", "parallel", "arbitrary")),
        name="rwkv_scan",
    )(*streams, gate, s0g, r_k, lnx_g, lnx_b)
    return out, s_t.reshape(s0.shape)


def _row_tile(m, cap):
    t = min(m, cap)
    assert m % t == 0
    return t


def _pad_cols(w, to):
    return jnp.pad(w, ((0, 0), (0, to - w.shape[1])))


def _pad_rows(w, to):
    return jnp.pad(w, ((0, to - w.shape[0]), (0, 0)))


def kernel(x_prompt, x_sample, mem_prompt, cache_k, cache_v, page_table, cache_mem_k, cache_mem_v, state_conv, state_shift, state_wkv, norm_gain, w_in, conv_w, conv_b, conv_ln_g, conv_ln_b, w_out, rwkv_mu, rwkv_w_r, rwkv_w_k, rwkv_w_v, rwkv_w_o, rwkv_w0, rwkv_w1, rwkv_w2, rwkv_a0, rwkv_a1, rwkv_a2, rwkv_g1, rwkv_g2, rwkv_k_k, rwkv_k_a, rwkv_r_k, rwkv_lnx_g, rwkv_lnx_b, w_cq, w_ck, w_cv, w_co, w_up, w_down):
    bp, sp, d = x_prompt.shape
    bs, ss, _ = x_sample.shape
    assert ss == 1
    depth = norm_gain.shape[0]
    conv_dim = conv_w.shape[2]
    att_dim = (w_in.shape[2] - 2 * conv_dim) // 3
    mem = mem_prompt.shape[1]
    mem_dim = w_ck.shape[2]
    n_pages, page = page_table.shape[1], cache_k.shape[2]
    past_len = n_pages * page
    pages_per_block = MOBA_BLOCK // page
    nb_past = past_len // MOBA_BLOCK
    assert past_len % MOBA_BLOCK == 0 and nb_past >= MOBA_TOPK and sp % MOBA_BLOCK == 0
    heads = d // RWKV_HD

    tm_p = _row_tile(sp, 512)
    tm_s = bs
    bf = lambda x: x.astype(BF16)

    w_mem = bf(jnp.concatenate([w for l in range(depth) for w in (w_ck[l], w_cv[l])], axis=1))
    memkv = _proj(mem_prompt.reshape(bp * mem, d), w_mem, _row_tile(bp * mem, 512))
    memkv = memkv.reshape(bp, mem, depth, 2, mem_dim)
    mem_k_p = [memkv[:, :, l, 0] for l in range(depth)]
    mem_v_p = [memkv[:, :, l, 1] for l in range(depth)]

    cos_p, sin_p = _rope_tables(jnp.arange(sp, dtype=jnp.int32))
    cos_s, sin_s = _rope_tables(jnp.full((tm_s,), past_len, dtype=jnp.int32))

    hp = x_prompt.reshape(bp * sp, d)
    hs = x_sample.reshape(bs, d)
    k_p, v_p, k_s, v_s, conv_p, conv_s, shift_p, shift_s, wkv_p, wkv_s = ([] for _ in range(10))

    for l in range(depth):
        gains = norm_gain[l]
        g0 = gains[0:1]
        tail_w = (gains, bf(w_cq[l]))
        tail_w2 = (bf(w_co[l]), bf(w_up[l]), bf(w_down[l]))
        mk_s = cache_mem_k[l].reshape(bs, mem, mem_dim)
        mv_s = cache_mem_v[l].reshape(bs, mem, mem_dim)
        if l % 2 == 0:
            e = l // 2
            w_in_b = bf(w_in[e])
            w_o = bf(w_out[e])
            cb, lg, lb = conv_b[e][None], conv_ln_g[e][None], conv_ln_b[e][None]

            u, q, k, v = _front0(hp, g0, w_in_b, cos_p, sin_p, conv_dim, att_dim, tm_p)
            u3 = u.reshape(bp, sp, conv_dim)
            c = _conv_prompt(u3, conv_w[e], cb, lg, lb, _row_tile(sp, 256))
            att = _moba_prompt(q.reshape(bp, sp, att_dim), k.reshape(bp, sp, att_dim), v.reshape(bp, sp, att_dim))
            hp = _tail(hp, [c.reshape(bp * sp, conv_dim), att.reshape(bp * sp, att_dim)],
                       [w_o[:conv_dim], w_o[conv_dim:]], *tail_w, mem_k_p[l], mem_v_p[l], *tail_w2, tm_p, sp)
            k_p.append(k.reshape(bp, sp, att_dim // ATT_HD, ATT_HD))
            v_p.append(v.reshape(bp, sp, att_dim // ATT_HD, ATT_HD))
            conv_p.append(u3[:, sp - (CONV_WIDTH - 1):])

            u, q, k, v = _front0(hs, g0, w_in_b, cos_s, sin_s, conv_dim, att_dim, tm_s)
            c = _conv_step(jnp.swapaxes(state_conv[e], 0, 1), u, conv_w[e], cb, lg, lb)
            ck = cache_k[e].reshape(-1, page, att_dim)
            cv = cache_v[e].reshape(-1, page, att_dim)
            sel = _moba_gate(page_table, q, ck, nb_past, pages_per_block)
            att = _moba_decode(page_table, sel, q, k, v, ck, cv)
            hs = _tail(hs, [c, att], [w_o[:conv_dim], w_o[conv_dim:]], *tail_w, mk_s, mv_s, *tail_w2, tm_s, 1)
            k_s.append(k.reshape(bs, 1, att_dim // ATT_HD, ATT_HD))
            v_s.append(v.reshape(bs, 1, att_dim // ATT_HD, ATT_HD))
            conv_s.append(jnp.concatenate([state_conv[e][:, 1:], u[:, None]], axis=1))
        else:
            o = l // 2
            lora = LANES
            gl = MXU_DIM
            feat = jnp.arange(d) // RWKV_HD
            prm = dict(
                mu=rwkv_mu[o], w_r=bf(rwkv_w_r[o]), w_k=bf(rwkv_w_k[o]), w_v=bf(rwkv_w_v[o]),
                w0=rwkv_w0[o][None], w1=bf(_pad_cols(rwkv_w1[o], lora)), w2=bf(_pad_rows(rwkv_w2[o], lora)),
                a0=rwkv_a0[o][None], a1=bf(_pad_cols(rwkv_a1[o], lora)), a2=bf(_pad_rows(rwkv_a2[o], lora)),
                g1=bf(_pad_cols(rwkv_g1[o], gl)), g2=bf(_pad_rows(rwkv_g2[o], gl)),
                k_k=rwkv_k_k[o][None], k_a=rwkv_k_a[o][None],
                seg=(feat[:, None] == feat[None, :]).astype(BF16))
            r_k = rwkv_r_k[o].reshape(1, d)
            lnx_g, lnx_b = rwkv_lnx_g[o][None], rwkv_lnx_b[o][None]
            w_o = bf(rwkv_w_o[o])

            streams, shift = _front1(hp, jnp.zeros((bp, d), F32), g0, prm, tm_p, sp)
            to3 = lambda x: x.reshape(bp, sp, d)
            gated, s_t = _scan([to3(x) for x in streams[:6]], to3(streams[6]),
                               jnp.zeros((bp, heads, RWKV_HD, RWKV_HD), F32), r_k, lnx_g, lnx_b)
            hp = _tail(hp, [gated.reshape(bp * sp, d)], [w_o], *tail_w, mem_k_p[l], mem_v_p[l], *tail_w2, tm_p, sp)
            shift_p.append(shift)
            wkv_p.append(s_t)

            streams, shift = _front1(hs, state_shift[o], g0, prm, tm_s, 1)
            pad = lambda x: jnp.pad(x[:, None, :], ((0, 0), (0, SCAN_CHUNK - 1), (0, 0)))
            gated, s_t = _scan([pad(x) for x in streams[:6]], pad(streams[6]), state_wkv[o], r_k, lnx_g, lnx_b)
            hs = _tail(hs, [gated[:, 0]], [w_o], *tail_w, mk_s, mv_s, *tail_w2, tm_s, 1)
            shift_s.append(shift)
            wkv_s.append(s_t)

    mem_shape = (bp, mem, MEM_HEADS, mem_dim // MEM_HEADS)
    return (hp.reshape(bp, sp, d), hs.reshape(bs, 1, d),
            jnp.stack(k_p), jnp.stack(v_p), jnp.stack(k_s), jnp.stack(v_s),
            jnp.stack(conv_p), jnp.stack(conv_s), jnp.stack(shift_p), jnp.stack(shift_s),
            jnp.stack(wkv_p), jnp.stack(wkv_s),
            jnp.stack([x.reshape(mem_shape) for x in mem_k_p]), jnp.stack([x.reshape(mem_shape) for x in mem_v_p]))
```

```python
import functools

import jax
import jax.numpy as jnp
from jax import lax
from jax.experimental import pallas as pl
from jax.experimental.pallas import tpu as pltpu

F32 = jnp.float32
BF16 = jnp.bfloat16

NORM_EPS = 1e-6
LN_EPS = 1e-5
GN_EPS = 64e-5
NEG_INF = -1e30
ATT_HD = 64
MEM_HEADS = 4
MEM_HD = 128
RWKV_HD = 64
MOBA_BLOCK = 256
MOBA_TOPK = 3
CONV_WIDTH = 31
ROPE_THETA = 10000.0

LANES = 128
SUBLANES = 8
MXU_DIM = 256
VMEM_LIMIT_BYTES = 56 * 1024 * 1024
SCAN_CHUNK = 64
SCAN_GROUP = MXU_DIM // RWKV_HD
HALO = 32


def _cparams(sem):
    return pltpu.CompilerParams(dimension_semantics=sem, vmem_limit_bytes=VMEM_LIMIT_BYTES)


def _const_spec(shape):
    nd = len(shape)
    return pl.BlockSpec(shape, lambda *_: (0,) * nd, pipeline_mode=pl.Buffered(1))


def _rms(x, g):
    return x * lax.rsqrt(jnp.mean(x * x, axis=-1, keepdims=True) + NORM_EPS) * g


def _sigmoid(x):
    return 1.0 / (1.0 + jnp.exp(-x))


def _mm(a, b):
    return jnp.dot(a.astype(BF16), b.astype(BF16), preferred_element_type=F32)


def _mm_nt(a, b):
    return lax.dot_general(a.astype(BF16), b.astype(BF16), (((1,), (1,)), ((), ())),
                           preferred_element_type=F32)


def _mm_tn(a, b):
    return lax.dot_general(a.astype(BF16), b.astype(BF16), (((0,), (0,)), ((), ())),
                           preferred_element_type=F32)


def _split(x):
    hi = x.astype(BF16)
    lo = (x - hi.astype(F32)).astype(BF16)
    return hi, lo


def _mm2(a, b):
    hi, lo = _split(a)
    return (jnp.dot(hi, b, preferred_element_type=F32)
            + jnp.dot(lo, b, preferred_element_type=F32))


def _proj_kernel(x_ref, w_ref, o_ref):
    o_ref[...] = _mm(x_ref[...], w_ref[...])


def _proj(x, w, tm):
    m, kdim = x.shape
    n = w.shape[1]
    return pl.pallas_call(
        _proj_kernel,
        out_shape=jax.ShapeDtypeStruct((m, n), F32),
        grid=(m // tm,),
        in_specs=[pl.BlockSpec((tm, kdim), lambda i: (i, 0)), _const_spec((kdim, n))],
        out_specs=pl.BlockSpec((tm, n), lambda i: (i, 0)),
        compiler_params=_cparams(("parallel",)),
        name="proj",
    )(x, w)


def _rope(x, cos, sin):
    width = x.shape[1]
    lane = lax.broadcasted_iota(jnp.int32, x.shape, 1)
    first_half = (lane % ATT_HD) < (ATT_HD // 2)
    partner = jnp.where(first_half,
                        pltpu.roll(x, width - ATT_HD // 2, axis=1),
                        pltpu.roll(x, ATT_HD // 2, axis=1))
    return x * cos + partner * sin


def _front0_kernel(x_ref, g_ref, w_ref, cos_ref, sin_ref, u_ref, q_ref, k_ref, v_ref, *, conv_dim, att_dim):
    hn = _rms(x_ref[...], g_ref[...]).astype(BF16)
    c0 = conv_dim
    val = jnp.dot(hn, w_ref[:, 0:c0], preferred_element_type=F32)
    gate = jnp.dot(hn, w_ref[:, c0:2 * c0], preferred_element_type=F32)
    u_ref[...] = val * _sigmoid(gate)
    reps = att_dim // LANES
    cos = jnp.concatenate([cos_ref[...]] * reps, axis=1)
    sin = jnp.concatenate([sin_ref[...]] * reps, axis=1)
    o = 2 * c0
    q = jnp.dot(hn, w_ref[:, o:o + att_dim], preferred_element_type=F32)
    q_ref[...] = _rope(q, cos, sin)
    k = jnp.dot(hn, w_ref[:, o + att_dim:o + 2 * att_dim], preferred_element_type=F32)
    k_ref[...] = _rope(k, cos, sin)
    v_ref[...] = jnp.dot(hn, w_ref[:, o + 2 * att_dim:o + 3 * att_dim], preferred_element_type=F32)


def _front0(h, g, w_in, cos_tab, sin_tab, conv_dim, att_dim, tm):
    m, d = h.shape
    n_tab = cos_tab.shape[0] // tm
    row = lambda i: (i, 0)
    tab = lambda i: (i % n_tab, 0)
    return pl.pallas_call(
        functools.partial(_front0_kernel, conv_dim=conv_dim, att_dim=att_dim),
        out_shape=(jax.ShapeDtypeStruct((m, conv_dim), F32),) + (jax.ShapeDtypeStruct((m, att_dim), F32),) * 3,
        grid=(m // tm,),
        in_specs=[pl.BlockSpec((tm, d), row), _const_spec((1, d)), _const_spec(w_in.shape),
                  pl.BlockSpec((tm, LANES), tab), pl.BlockSpec((tm, LANES), tab)],
        out_specs=(pl.BlockSpec((tm, conv_dim), row),) + (pl.BlockSpec((tm, att_dim), row),) * 3,
        compiler_params=_cparams(("parallel",)),
        name="front0",
    )(h, g, w_in, cos_tab, sin_tab)


def _rope_tables(pos):
    half = ATT_HD // 2
    inv_freq = ROPE_THETA ** (-jnp.arange(half, dtype=F32) / half)
    ang = pos.astype(F32)[:, None] * inv_freq[None, :]
    cos, sin = jnp.cos(ang), jnp.sin(ang)
    reps = LANES // ATT_HD
    return (jnp.tile(jnp.concatenate([cos, cos], axis=1), (1, reps)),
            jnp.tile(jnp.concatenate([-sin, sin], axis=1), (1, reps)))


def _ln_silu(y, g, b):
    mu = jnp.mean(y, axis=-1, keepdims=True)
    yc = y - mu
    var = jnp.mean(yc * yc, axis=-1, keepdims=True)
    ln = yc * lax.rsqrt(var + LN_EPS) * g + b
    return ln * _sigmoid(ln)


def _conv_kernel(u_ref, uprev_ref, w_ref, b_ref, lg_ref, lb_ref, c_ref, full_scr, *, tc, sub):
    i = pl.program_id(1)
    full_scr[0:HALO, :] = jnp.where(i > 0, uprev_ref[0], 0.0)
    full_scr[HALO:HALO + tc, :] = u_ref[0]
    first = HALO - (CONV_WIDTH - 1)
    for r0 in range(0, tc, sub):
        acc = jnp.broadcast_to(b_ref[...], (sub, b_ref.shape[1]))
        for j in range(CONV_WIDTH):
            acc = acc + w_ref[j:j + 1, :] * full_scr[first + r0 + j:first + r0 + j + sub, :]
        c_ref[0, r0:r0 + sub, :] = _ln_silu(acc, lg_ref[...], lb_ref[...])


def _conv_prompt(u, w, b, lg, lb, tc):
    bsz, s, c = u.shape
    per = tc // HALO
    wp = jnp.pad(w, ((0, HALO - CONV_WIDTH), (0, 0)))
    return pl.pallas_call(
        functools.partial(_conv_kernel, tc=tc, sub=HALO),
        out_shape=jax.ShapeDtypeStruct((bsz, s, c), F32),
        grid=(bsz, s // tc),
        in_specs=[pl.BlockSpec((1, tc, c), lambda bi, i: (bi, i, 0)),
                  pl.BlockSpec((1, HALO, c), lambda bi, i: (bi, jnp.maximum(i * per - 1, 0), 0)),
                  _const_spec((HALO, c)), _const_spec((1, c)), _const_spec((1, c)), _const_spec((1, c))],
        out_specs=pl.BlockSpec((1, tc, c), lambda bi, i: (bi, i, 0)),
        scratch_shapes=[pltpu.VMEM((HALO + tc, c), F32)],
        compiler_params=_cparams(("parallel", "parallel")),
        name="conv_prompt",
    )(u, u, wp, b, lg, lb)


def _conv_step_kernel(st_ref, u_ref, w_ref, b_ref, lg_ref, lb_ref, c_ref):
    acc = b_ref[...] + w_ref[CONV_WIDTH - 1:CONV_WIDTH, :] * u_ref[...]
    for j in range(CONV_WIDTH - 1):
        acc = acc + w_ref[j:j + 1, :] * st_ref[j]
    c_ref[...] = _ln_silu(acc, lg_ref[...], lb_ref[...])


def _conv_step(state_t, u, w, b, lg, lb):
    bsz, c = u.shape
    wp = jnp.pad(w, ((0, HALO - CONV_WIDTH), (0, 0)))
    return pl.pallas_call(
        _conv_step_kernel,
        out_shape=jax.ShapeDtypeStruct((bsz, c), F32),
        grid=(1,),
        in_specs=[_const_spec(state_t.shape), _const_spec((bsz, c)), _const_spec((HALO, c)),
                  _const_spec((1, c)), _const_spec((1, c)), _const_spec((1, c))],
        out_specs=pl.BlockSpec((bsz, c), lambda i: (0, 0)),
        compiler_params=_cparams(("arbitrary",)),
        name="conv_step",
    )(state_t, u, wp, b, lg, lb)


def _top_blocks(gate, n_past):
    col = lax.broadcasted_iota(jnp.int32, gate.shape, 1).astype(F32)
    g = jnp.where(col < n_past.astype(F32), gate, NEG_INF)
    picks = []
    for r in range(MOBA_TOPK):
        m = jnp.max(g, axis=1, keepdims=True)
        idx = jnp.min(jnp.where(g == m, col, 1e9), axis=1, keepdims=True)
        picks.append(jnp.where(r < n_past, idx, -1.0))
        g = jnp.where(col == idx, -jnp.inf, g)
    return picks


def _moba_kernel(q_ref, k_ref, v_ref, o_ref, kmean_scr, *, nb, scale):
    i = pl.program_id(2)
    blk = MOBA_BLOCK

    @pl.when(i == 0)
    def _():
        kmean_scr[...] = jnp.zeros_like(kmean_scr)
        for j in range(nb):
            kmean_scr[j:j + 1, :] = jnp.mean(k_ref[0, j * blk:(j + 1) * blk, :], axis=0, keepdims=True)

    q = q_ref[0]
    lane = lax.broadcasted_iota(jnp.int32, q.shape, 1)
    row = lax.broadcasted_iota(jnp.int32, (blk, blk), 0)
    colk = lax.broadcasted_iota(jnp.int32, (blk, blk), 1)
    causal = colk <= row
    start = pl.multiple_of(i * blk, blk)
    k_own = k_ref[0, pl.ds(start, blk), :].astype(BF16)
    v_own = v_ref[0, pl.ds(start, blk), :].astype(BF16)
    km_hi, km_lo = _split(kmean_scr[...])
    out = jnp.zeros(q.shape, F32)
    for hh in range(LANES // ATT_HD):
        in_head = (lane >= hh * ATT_HD) & (lane < (hh + 1) * ATT_HD)
        qh = jnp.where(in_head, q, 0.0)
        q_hi, q_lo = _split(qh)
        nt = (((1,), (1,)), ((), ()))
        gate = (lax.dot_general(q_hi, km_hi, nt, preferred_element_type=F32)
                + lax.dot_general(q_hi, km_lo, nt, preferred_element_type=F32)
                + lax.dot_general(q_lo, km_hi, nt, preferred_element_type=F32))
        picks = _top_blocks(gate, i)

        s = lax.dot_general(q_hi, k_own, nt, preferred_element_type=F32) * scale
        s = jnp.where(causal, s, NEG_INF)
        m0 = jnp.max(s, axis=1, keepdims=True)
        p = jnp.exp(s - m0)
        l0 = jnp.sum(p, axis=1, keepdims=True)
        acc0 = jnp.dot(p.astype(BF16), v_own, preferred_element_type=F32)

        def body(j, carry, q_hi=q_hi, picks=picks):
            m, l, acc = carry
            off = pl.multiple_of(j * blk, blk)
            kj = k_ref[0, pl.ds(off, blk), :].astype(BF16)
            vj = v_ref[0, pl.ds(off, blk), :].astype(BF16)
            jf = j.astype(F32)
            chosen = (picks[0] == jf) | (picks[1] == jf) | (picks[2] == jf)
            sj = lax.dot_general(q_hi, kj, nt, preferred_element_type=F32) * scale
            sj = jnp.where(chosen, sj, NEG_INF)
            m_new = jnp.maximum(m, jnp.max(sj, axis=1, keepdims=True))
            alpha = jnp.exp(m - m_new)
            pj = jnp.exp(sj - m_new)
            l = alpha * l + jnp.sum(pj, axis=1, keepdims=True)
            acc = alpha * acc + jnp.dot(pj.astype(BF16), vj, preferred_element_type=F32)
            return m_new, l, acc

        _, l, acc = lax.fori_loop(0, i, body, (m0, l0, acc0))
        out = jnp.where(in_head, acc / l, out)
    o_ref[0] = out


def _moba_prompt(q, k, v):
    bsz, s, a = q.shape
    nb = s // MOBA_BLOCK
    blk = pl.BlockSpec((1, MOBA_BLOCK, LANES), lambda b, hp, i: (b, i, hp))
    seq = pl.BlockSpec((1, s, LANES), lambda b, hp, i: (b, 0, hp))
    return pl.pallas_call(
        functools.partial(_moba_kernel, nb=nb, scale=ATT_HD ** -0.5),
        out_shape=jax.ShapeDtypeStruct((bsz, s, a), F32),
        grid=(bsz, a // LANES, nb),
        in_specs=[blk, seq, seq],
        out_specs=blk,
        scratch_shapes=[pltpu.VMEM((LANES, LANES), F32)],
        compiler_params=_cparams(("parallel", "parallel", "arbitrary")),
        name="moba_prompt",
    )(q, k, v)


def _moba_gate_kernel(pt_ref, q_ref, *refs, nb, pages_per_step, pages_per_block):
    del pt_ref
    page_refs = refs[:pages_per_step]
    sel_ref, qb_scr, g_scr = refs[pages_per_step:]
    j = pl.program_id(1)
    blocks_per_step = pages_per_step // pages_per_block

    @pl.when(j == 0)
    def _():
        qb_scr[...] = jnp.broadcast_to(q_ref[0], qb_scr.shape)

    qb = qb_scr[...]
    for bl in range(blocks_per_step):
        ksum = page_refs[bl * pages_per_block][0]
        for pg in range(1, pages_per_block):
            ksum = ksum + page_refs[bl * pages_per_block + pg][0]
        g_scr[j * blocks_per_step + bl] = jnp.sum(ksum * qb, axis=1)

    @pl.when(j == nb // blocks_per_step - 1)
    def _():
        gate = jnp.sum(g_scr[...], axis=2, keepdims=True) * (1.0 / MOBA_BLOCK)
        blk = lax.broadcasted_iota(jnp.int32, gate.shape, 0).astype(F32)
        lane = lax.broadcasted_iota(jnp.int32, sel_ref.shape[1:], 1)
        sel = jnp.zeros(sel_ref.shape[1:], F32)
        for r in range(MOBA_TOPK):
            m = jnp.max(gate, axis=0, keepdims=True)
            idx = jnp.min(jnp.where(gate == m, blk, 1e9), axis=0, keepdims=True)
            sel = jnp.where(lane == r, idx[0], sel)
            gate = jnp.where(blk == idx, -jnp.inf, gate)
        sel_ref[0] = sel.astype(jnp.int32)


def _moba_gate(page_table, q_col, cache_kt, nb, pages_per_block, pages_per_step):
    bsz, heads, hd, _ = q_col.shape
    page = cache_kt.shape[3]
    steps = nb * pages_per_block // pages_per_step

    def page_spec(w):
        return pl.BlockSpec((1, heads, hd, page), lambda b, j, pt: (pt[b, j * pages_per_step + w], 0, 0, 0))

    grid_spec = pltpu.PrefetchScalarGridSpec(
        num_scalar_prefetch=1,
        grid=(bsz, steps),
        in_specs=[pl.BlockSpec((1, heads, hd, 1), lambda b, j, pt: (b, 0, 0, 0))]
        + [page_spec(w) for w in range(pages_per_step)],
        out_specs=pl.BlockSpec((1, heads, LANES), lambda b, j, pt: (b, 0, 0)),
        scratch_shapes=[pltpu.VMEM((heads, hd, page), F32), pltpu.VMEM((nb, heads, page), F32)],
    )
    return pl.pallas_call(
        functools.partial(_moba_gate_kernel, nb=nb, pages_per_step=pages_per_step, pages_per_block=pages_per_block),
        out_shape=jax.ShapeDtypeStruct((bsz, heads, LANES), jnp.int32),
        grid_spec=grid_spec,
        compiler_params=_cparams(("parallel", "arbitrary")),
        name="moba_gate",
    )(page_table, q_col, *([cache_kt] * pages_per_step))


def _moba_decode_kernel(pt_ref, sel_ref, q_ref, kn_ref, vn_ref, *refs, n_pages, scale):
    del pt_ref, sel_ref
    k_refs = refs[:n_pages]
    v_refs = refs[n_pages:2 * n_pages]
    o_ref = refs[2 * n_pages]
    q = q_ref[0, 0]
    s_own = jnp.sum(q * kn_ref[0, 0], axis=0, keepdims=True) * scale
    s = [jnp.sum(kr[0, 0] * q, axis=0, keepdims=True) * scale for kr in k_refs]
    m = s_own
    for sp in s:
        m = jnp.maximum(m, jnp.max(sp, axis=1, keepdims=True))
    p_own = jnp.exp(s_own - m)
    l = p_own
    acc = p_own * vn_ref[0, 0]
    for sp, vr in zip(s, v_refs):
        p = jnp.exp(sp - m)
        l = l + jnp.sum(p, axis=1, keepdims=True)
        acc = acc + jnp.sum(vr[0, 0] * p, axis=1, keepdims=True)
    o_ref[0, 0] = acc / l


def _moba_decode(page_table, sel, q_col, kn_col, vn_col, cache_kt, cache_vt, pages_per_block):
    bsz, heads, hd, _ = q_col.shape
    page = cache_kt.shape[3]
    n_pages = MOBA_TOPK * pages_per_block
    vec = pl.BlockSpec((1, 1, hd, 1), lambda b, h, pt, sl: (b, h, 0, 0))

    def page_spec(r, w):
        return pl.BlockSpec((1, 1, hd, page),
                            lambda b, h, pt, sl: (pt[b, pages_per_block * sl[b, h, r] + w], h, 0, 0))

    pages = [page_spec(r, w) for r in range(MOBA_TOPK) for w in range(pages_per_block)]
    grid_spec = pltpu.PrefetchScalarGridSpec(
        num_scalar_prefetch=2,
        grid=(bsz, heads),
        in_specs=[vec, vec, vec] + pages + pages,
        out_specs=vec,
    )
    return pl.pallas_call(
        functools.partial(_moba_decode_kernel, n_pages=n_pages, scale=ATT_HD ** -0.5),
        out_shape=jax.ShapeDtypeStruct((bsz, heads, hd, 1), F32),
        grid_spec=grid_spec,
        compiler_params=_cparams(("parallel", "parallel")),
        name="moba_decode",
    )(page_table, sel, q_col, kn_col, vn_col, *([cache_kt] * n_pages), *([cache_vt] * n_pages))


def _tail_kernel(*refs, n_mix, nb, s, ff_chunk):
    h_ref = refs[0]
    mix_refs = refs[1:1 + n_mix]
    wmix_refs = refs[1 + n_mix:1 + 2 * n_mix]
    g_ref, wcq_ref, mk_ref, mv_ref, wco_ref, wup_ref, wdn_ref, o_ref, ca_scr = refs[1 + 2 * n_mix:]

    mix = _mm(mix_refs[0][...], wmix_refs[0][...])
    for mr, wr in zip(mix_refs[1:], wmix_refs[1:]):
        mix = mix + _mm(mr[...], wr[...])
    h = h_ref[...] + _rms(mix, g_ref[1:2, :])

    q = _mm(_rms(h, g_ref[2:3, :]), wcq_ref[...])
    rows = max(s, SUBLANES)
    for bi in range(nb):
        qb = q[bi * s:(bi + 1) * s, :]
        if s < rows:
            qb = jnp.broadcast_to(qb[0:1, :], (rows, qb.shape[1]))
        for hh in range(MEM_HEADS):
            sl = slice(hh * MEM_HD, (hh + 1) * MEM_HD)
            sc = _mm_nt(qb[:, sl], mk_ref[bi, :, sl]) * (MEM_HD ** -0.5)
            p = jnp.exp(sc - jnp.max(sc, axis=1, keepdims=True))
            oh = _mm(p, mv_ref[bi, :, sl]) / jnp.sum(p, axis=1, keepdims=True)
            ca_scr[bi * s:(bi + 1) * s, sl] = oh[0:s, :]
    h = h + _rms(_mm(ca_scr[...], wco_ref[...]), g_ref[3:4, :])

    hn = _rms(h, g_ref[4:5, :]).astype(BF16)
    d_ff = wup_ref.shape[1]
    acc = jnp.zeros(h.shape, F32)
    for c0 in range(0, d_ff, ff_chunk):
        up = jnp.dot(hn, wup_ref[:, c0:c0 + ff_chunk], preferred_element_type=F32)
        up = jnp.square(jnp.maximum(up, 0.0))
        acc = acc + jnp.dot(up.astype(BF16), wdn_ref[c0:c0 + ff_chunk, :], preferred_element_type=F32)
    o_ref[...] = h + _rms(acc, g_ref[5:6, :])


def _tail(h, mixes, w_mixes, gains, w_cq, mk, mv, w_co, w_up, w_down, tm, seq_len):
    m, d = h.shape
    s = min(seq_len, tm)
    nb = tm // s
    n_mix = len(mixes)
    mem = mk.shape[1]
    mem_dim = mk.shape[2]
    row = lambda i: (i, 0)
    tiles_per_seq = max(seq_len // tm, 1)
    mem_map = lambda i: (i // tiles_per_seq, 0, 0)
    in_specs = [pl.BlockSpec((tm, d), row)]
    in_specs += [pl.BlockSpec((tm, x.shape[1]), row) for x in mixes]
    in_specs += [_const_spec(w.shape) for w in w_mixes]
    in_specs += [_const_spec(gains.shape), _const_spec(w_cq.shape),
                 pl.BlockSpec((nb, mem, mem_dim), mem_map), pl.BlockSpec((nb, mem, mem_dim), mem_map),
                 _const_spec(w_co.shape), _const_spec(w_up.shape), _const_spec(w_down.shape)]
    return pl.pallas_call(
        functools.partial(_tail_kernel, n_mix=n_mix, nb=nb, s=s, ff_chunk=min(1024, w_up.shape[1])),
        out_shape=jax.ShapeDtypeStruct((m, d), F32),
        grid=(m // tm,),
        in_specs=in_specs,
        out_specs=pl.BlockSpec((tm, d), row),
        scratch_shapes=[pltpu.VMEM((tm, mem_dim), F32)],
        compiler_params=_cparams(("parallel",)),
        name="tail",
    )(h, *mixes, *w_mixes, gains, w_cq, mk, mv, w_co, w_up, w_down)


def _front1_kernel(h_ref, hprev_ref, xlast_ref, g_ref, mu_ref, wr_ref, wk_ref, wv_ref, w0_ref, w1_ref, w2_ref,
                   a0_ref, a1_ref, a2_ref, g1_ref, g2_ref, kk_ref, ka_ref, seg_ref,
                   r_out, ld_out, k_out, v_out, kk_out, a_out, gt_out, sh_out, *, seq, tiles_per_seq):
    i = pl.program_id(0)
    g = g_ref[...]
    hn = _rms(h_ref[...], g)
    if seq:
        prev_row = _rms(hprev_ref[SUBLANES - 1:SUBLANES, :], g)
        prev_row = jnp.where(i % tiles_per_seq == 0, xlast_ref[0], prev_row)
        rowid = lax.broadcasted_iota(jnp.int32, hn.shape, 0)
        xprev = jnp.where(rowid == 0, prev_row, pltpu.roll(hn, 1, axis=0))
        sh_out[0] = hn[hn.shape[0] - 1:, :]
    else:
        xprev = xlast_ref[...]
        sh_out[...] = hn
    xx = xprev - hn
    mix = lambda n: hn + xx * mu_ref[n:n + 1, :]
    r = _mm(mix(0), wr_ref[...])
    k = _mm(mix(2), wk_ref[...])
    v = _mm(mix(3), wv_ref[...])
    wl = w0_ref[...] + _mm(jnp.tanh(_mm(mix(1), w1_ref[...])), w2_ref[...])
    z = -wl
    softplus = jnp.maximum(z, 0.0) + jnp.log(1.0 + jnp.exp(-jnp.abs(z)))
    ld_out[...] = -jnp.exp(-softplus - 0.5)
    a = _sigmoid(a0_ref[...] + _mm(_mm(mix(4), a1_ref[...]), a2_ref[...]))
    gt_out[...] = _mm(_sigmoid(_mm(mix(5), g1_ref[...])), g2_ref[...])
    kk = k * kk_ref[...]
    ss = _mm2(kk * kk, seg_ref[...])
    kk_out[...] = kk / jnp.maximum(jnp.sqrt(ss), 1e-12)
    k_out[...] = k * (1.0 + (a - 1.0) * ka_ref[...])
    r_out[...] = r
    v_out[...] = v
    a_out[...] = a


def _front1(h, x_last, g, p, tm, s):
    m, d = h.shape
    bsz = m // s
    seq = s > 1
    row = lambda i: (i, 0)
    big = jax.ShapeDtypeStruct((m, d), F32)
    if seq:
        tps = s // tm
        per = tm // SUBLANES
        hprev_spec = pl.BlockSpec((SUBLANES, d), lambda i: (jnp.maximum(i * per - 1, 0), 0))
        xlast = x_last.reshape(bsz, 1, d)
        xlast_spec = pl.BlockSpec((1, 1, d), lambda i: (i // tps, 0, 0))
        sh_shape = jax.ShapeDtypeStruct((bsz, 1, d), F32)
        sh_spec = pl.BlockSpec((1, 1, d), lambda i: (i // tps, 0, 0))
        sem = ("arbitrary",)
    else:
        tps = 1
        hprev_spec = pl.BlockSpec((SUBLANES, d), lambda i: (0, 0))
        xlast = x_last
        xlast_spec = pl.BlockSpec((tm, d), row)
        sh_shape = big
        sh_spec = pl.BlockSpec((tm, d), row)
        sem = ("parallel",)
    consts = [g, p["mu"], p["w_r"], p["w_k"], p["w_v"], p["w0"], p["w1"], p["w2"], p["a0"], p["a1"], p["a2"],
              p["g1"], p["g2"], p["k_k"], p["k_a"], p["seg"]]
    outs = pl.pallas_call(
        functools.partial(_front1_kernel, seq=seq, tiles_per_seq=tps),
        out_shape=(big,) * 7 + (sh_shape,),
        grid=(m // tm,),
        in_specs=[pl.BlockSpec((tm, d), row), hprev_spec, xlast_spec] + [_const_spec(c.shape) for c in consts],
        out_specs=(pl.BlockSpec((tm, d), row),) * 7 + (sh_spec,),
        compiler_params=_cparams(sem),
        name="front1",
    )(h, h, xlast, *consts)
    return outs[:7], outs[7].reshape(bsz, d)


def _scan_kernel(r_ref, ld_ref, k_ref, v_ref, kk_ref, a_ref, gt_ref, s0_ref, rk_ref, lg_ref, lb_ref,
                 o_ref, st_ref, s_scr, *, n_chunks, n_groups):
    c = pl.program_id(1)
    grp = SCAN_GROUP
    w = grp * RWKV_HD
    sbd = (lax.broadcasted_iota(jnp.int32, (w, w), 0) // RWKV_HD
           == lax.broadcasted_iota(jnp.int32, (w, w), 1) // RWKV_HD)

    @pl.when(c == 0)
    def _():
        for q in range(n_groups):
            s0 = s0_ref[0, q]
            s_scr[q] = jnp.where(sbd, jnp.concatenate([s0] * grp, axis=1), 0.0)

    _scan_chunk(r_ref[0], ld_ref[0], k_ref[0], v_ref[0], kk_ref[0], a_ref[0], gt_ref[0],
                rk_ref[...], lg_ref[...], lb_ref[...], o_ref, s_scr, n_groups)

    @pl.when(c == n_chunks - 1)
    def _():
        for q in range(n_groups):
            full = s_scr[q]
            acc = full[:, 0:RWKV_HD]
            for hh in range(1, grp):
                acc = acc + full[:, hh * RWKV_HD:(hh + 1) * RWKV_HD]
            st_ref[0, q] = acc


def _scan_chunk(r, ld, k, v, kk, a_gate, gt, r_k, lnx_g, lnx_b, o_ref, s_scr, n_groups):
    ch = SCAN_CHUNK
    grp = SCAN_GROUP
    w = grp * RWKV_HD
    rows = grp * ch
    gs = range(n_groups)
    rhead = lax.broadcasted_iota(jnp.int32, (rows, w), 0) // ch
    chead = lax.broadcasted_iota(jnp.int32, (rows, w), 1) // RWKV_HD
    bd = rhead == chead
    tile_r = lambda x: jnp.concatenate([x] * grp, axis=0)
    diag = lambda x: jnp.where(bd, tile_r(x), 0.0)
    part = lambda x, q: x[:, q * w:(q + 1) * w]

    b = kk * a_gate
    trow = lax.broadcasted_iota(jnp.int32, ld.shape, 0)
    cum = ld
    sh = 1
    while sh < ch:
        cum = cum + jnp.where(trow >= sh, pltpu.roll(cum, sh, axis=0), 0.0)
        sh *= 2
    clast = cum[ch - 1:ch, :]
    p_inv = jnp.exp(-cum)
    p_to = jnp.exp(clast - cum)
    rt = r * jnp.exp(cum)
    at = -kk * jnp.exp(cum - ld)
    bt = b * p_inv
    kt = k * p_inv
    bh = b * p_to
    kh = k * p_to
    p_end = jnp.exp(clast)

    tr = lax.broadcasted_iota(jnp.int32, (rows, rows), 0)
    tc = lax.broadcasted_iota(jnp.int32, (rows, rows), 1)
    same = (tr // ch) == (tc // ch)
    strict = same & (tc < tr)
    incl = same & (tc <= tr)
    eye = jnp.where(tr == tc, 1.0, 0.0)

    lhs = [jnp.concatenate([diag(part(at, q)), diag(part(rt, q))], axis=0).astype(BF16) for q in gs]
    rhs = [jnp.concatenate([tile_r(part(bt, q)), tile_r(part(kt, q))], axis=0) for q in gs]
    mall = [_mm_nt(lhs[q], rhs[q]) for q in gs]
    lp = [jnp.where(strict, mall[q][0:rows, 0:rows], 0.0) for q in gs]
    mak = [jnp.where(strict, mall[q][0:rows, rows:], 0.0) for q in gs]
    mrb = [jnp.where(incl, mall[q][rows:, 0:rows], 0.0) for q in gs]
    mrk = [jnp.where(incl, mall[q][rows:, rows:], 0.0) for q in gs]

    tinv = [eye + lp[q] for q in gs]
    n = 2
    while n < ch:
        lp = [_mm(lp[q], lp[q]) for q in gs]
        tinv = [tinv[q] + _mm(tinv[q], lp[q]) for q in gs]
        n *= 2

    vd = [diag(part(v, q)).astype(BF16) for q in gs]
    state = [s_scr[q] for q in gs]
    s0t = [_mm_nt(lhs[q], state[q]) for q in gs]
    wmat = [s0t[q][0:rows] + _mm(mak[q], vd[q]) for q in gs]
    u = [_mm(tinv[q], wmat[q]) for q in gs]
    ybd = [s0t[q][rows:] + _mm(mrb[q], u[q]) + _mm(mrk[q], vd[q]) for q in gs]
    upd = [_mm_tn(jnp.concatenate([u[q].astype(BF16), vd[q]], axis=0),
                  jnp.concatenate([diag(part(bh, q)), diag(part(kh, q))], axis=0)) for q in gs]
    for q in gs:
        s_scr[q] = state[q] * part(p_end, q) + upd[q]

    fa = lax.broadcasted_iota(jnp.int32, (w, w), 0) // RWKV_HD
    fb = lax.broadcasted_iota(jnp.int32, (w, w), 1) // RWKV_HD
    seg = jnp.where(fa == fb, 1.0, 0.0).astype(BF16)
    rkr = r * k * r_k
    for q in gs:
        y = ybd[q][0:ch]
        for hh in range(1, grp):
            y = y + ybd[q][hh * ch:(hh + 1) * ch]
        mean = _mm2(y, seg) * (1.0 / RWKV_HD)
        yc = y - mean
        var = _mm2(yc * yc, seg) * (1.0 / RWKV_HD)
        yn = yc * lax.rsqrt(var + GN_EPS) * part(lnx_g, q) + part(lnx_b, q)
        bonus = _mm2(part(rkr, q), seg) * part(v, q)
        o_ref[0, :, q * w:(q + 1) * w] = (yn + bonus) * part(gt, q)


def _scan(streams, gate, s0, r_k, lnx_g, lnx_b):
    bsz, t, d = gate.shape
    w = SCAN_GROUP * RWKV_HD
    nq = d // w
    nc = t // SCAN_CHUNK
    tok = pl.BlockSpec((1, SCAN_CHUNK, d), lambda b, c: (b, c, 0))
    st = pl.BlockSpec((1, nq, w, RWKV_HD), lambda b, c: (b, 0, 0, 0))
    s0g = s0.reshape(bsz, nq, w, RWKV_HD)
    out, s_t = pl.pallas_call(
        functools.partial(_scan_kernel, n_chunks=nc, n_groups=nq),
        out_shape=(jax.ShapeDtypeStruct((bsz, t, d), F32), jax.ShapeDtypeStruct(s0g.shape, F32)),
        grid=(bsz, nc),
        in_specs=[tok] * 7 + [st, _const_spec((1, d)), _const_spec((1, d)), _const_spec((1, d))],
        out_specs=(tok, st),
        scratch_shapes=[pltpu.VMEM((nq, w, w), F32)],
        compiler_params=_cparams(("parallel", "arbitrary")),
        name="rwkv_scan",
    )(*streams, gate, s0g, r_k, lnx_g, lnx_b)
    return out, s_t.reshape(s0.shape)


def _row_tile(m, cap):
    t = min(m, cap)
    assert m % t == 0
    return t


def _largest_divisor(n, cap):
    return max(t for t in range(1, cap + 1) if n % t == 0)


def _pad_cols(w, to):
    return jnp.pad(w, ((0, 0), (0, to - w.shape[1])))


def _pad_rows(w, to):
    return jnp.pad(w, ((0, to - w.shape[0]), (0, 0)))


def kernel(x_prompt, x_sample, mem_prompt, cache_k, cache_v, page_table, cache_mem_k, cache_mem_v, state_conv, state_shift, state_wkv, norm_gain, w_in, conv_w, conv_b, conv_ln_g, conv_ln_b, w_out, rwkv_mu, rwkv_w_r, rwkv_w_k, rwkv_w_v, rwkv_w_o, rwkv_w0, rwkv_w1, rwkv_w2, rwkv_a0, rwkv_a1, rwkv_a2, rwkv_g1, rwkv_g2, rwkv_k_k, rwkv_k_a, rwkv_r_k, rwkv_lnx_g, rwkv_lnx_b, w_cq, w_ck, w_cv, w_co, w_up, w_down):
    bp, sp, d = x_prompt.shape
    bs, ss, _ = x_sample.shape
    assert ss == 1
    depth = norm_gain.shape[0]
    conv_dim = conv_w.shape[2]
    att_dim = (w_in.shape[2] - 2 * conv_dim) // 3
    mem = mem_prompt.shape[1]
    mem_dim = w_ck.shape[2]
    n_pages, page = page_table.shape[1], cache_k.shape[2]
    past_len = n_pages * page
    pages_per_block = MOBA_BLOCK // page
    att_heads = att_dim // ATT_HD
    nb_past = past_len // MOBA_BLOCK
    assert past_len % MOBA_BLOCK == 0 and nb_past >= MOBA_TOPK and sp % MOBA_BLOCK == 0
    heads = d // RWKV_HD

    tm_p = _row_tile(sp, 512)
    tm_s = bs
    bf = lambda x: x.astype(BF16)

    w_mem = bf(jnp.concatenate([w for l in range(depth) for w in (w_ck[l], w_cv[l])], axis=1))
    memkv = _proj(mem_prompt.reshape(bp * mem, d), w_mem, _row_tile(bp * mem, 512))
    memkv = memkv.reshape(bp, mem, depth, 2, mem_dim)
    mem_k_p = [memkv[:, :, l, 0] for l in range(depth)]
    mem_v_p = [memkv[:, :, l, 1] for l in range(depth)]

    cos_p, sin_p = _rope_tables(jnp.arange(sp, dtype=jnp.int32))
    cos_s, sin_s = _rope_tables(jnp.full((tm_s,), past_len, dtype=jnp.int32))

    hp = x_prompt.reshape(bp * sp, d)
    hs = x_sample.reshape(bs, d)
    k_p, v_p, k_s, v_s, conv_p, conv_s, shift_p, shift_s, wkv_p, wkv_s = ([] for _ in range(10))

    for l in range(depth):
        gains = norm_gain[l]
        g0 = gains[0:1]
        tail_w = (gains, bf(w_cq[l]))
        tail_w2 = (bf(w_co[l]), bf(w_up[l]), bf(w_down[l]))
        mk_s = cache_mem_k[l].reshape(bs, mem, mem_dim)
        mv_s = cache_mem_v[l].reshape(bs, mem, mem_dim)
        if l % 2 == 0:
            e = l // 2
            w_in_b = bf(w_in[e])
            w_o = bf(w_out[e])
            cb, lg, lb = conv_b[e][None], conv_ln_g[e][None], conv_ln_b[e][None]

            u, q, k, v = _front0(hp, g0, w_in_b, cos_p, sin_p, conv_dim, att_dim, tm_p)
            u3 = u.reshape(bp, sp, conv_dim)
            c = _conv_prompt(u3, conv_w[e], cb, lg, lb, _row_tile(sp, 256))
            att = _moba_prompt(q.reshape(bp, sp, att_dim), k.reshape(bp, sp, att_dim), v.reshape(bp, sp, att_dim))
            hp = _tail(hp, [c.reshape(bp * sp, conv_dim), att.reshape(bp * sp, att_dim)],
                       [w_o[:conv_dim], w_o[conv_dim:]], *tail_w, mem_k_p[l], mem_v_p[l], *tail_w2, tm_p, sp)
            k_p.append(k.reshape(bp, sp, att_dim // ATT_HD, ATT_HD))
            v_p.append(v.reshape(bp, sp, att_dim // ATT_HD, ATT_HD))
            conv_p.append(u3[:, sp - (CONV_WIDTH - 1):])

            u, q, k, v = _front0(hs, g0, w_in_b, cos_s, sin_s, conv_dim, att_dim, tm_s)
            c = _conv_step(jnp.swapaxes(state_conv[e], 0, 1), u, conv_w[e], cb, lg, lb)
            ckt = jnp.transpose(cache_k[e], (0, 2, 3, 1))
            cvt = jnp.transpose(cache_v[e], (0, 2, 3, 1))
            col = lambda x: x.reshape(bs, att_heads, ATT_HD, 1)
            pages_per_step = pages_per_block * _largest_divisor(nb_past, 4)
            sel = _moba_gate(page_table, col(q), ckt, nb_past, pages_per_block, pages_per_step)
            att = _moba_decode(page_table, sel[:, :, :MOBA_TOPK], col(q), col(k), col(v), ckt, cvt,
                               pages_per_block).reshape(bs, att_dim)
            hs = _tail(hs, [c, att], [w_o[:conv_dim], w_o[conv_dim:]], *tail_w, mk_s, mv_s, *tail_w2, tm_s, 1)
            k_s.append(k.reshape(bs, 1, att_dim // ATT_HD, ATT_HD))
            v_s.append(v.reshape(bs, 1, att_dim // ATT_HD, ATT_HD))
            conv_s.append(jnp.concatenate([state_conv[e][:, 1:], u[:, None]], axis=1))
        else:
            o = l // 2
            lora = LANES
            gl = MXU_DIM
            feat = jnp.arange(d) // RWKV_HD
            prm = dict(
                mu=rwkv_mu[o], w_r=bf(rwkv_w_r[o]), w_k=bf(rwkv_w_k[o]), w_v=bf(rwkv_w_v[o]),
                w0=rwkv_w0[o][None], w1=bf(_pad_cols(rwkv_w1[o], lora)), w2=bf(_pad_rows(rwkv_w2[o], lora)),
                a0=rwkv_a0[o][None], a1=bf(_pad_cols(rwkv_a1[o], lora)), a2=bf(_pad_rows(rwkv_a2[o], lora)),
                g1=bf(_pad_cols(rwkv_g1[o], gl)), g2=bf(_pad_rows(rwkv_g2[o], gl)),
                k_k=rwkv_k_k[o][None], k_a=rwkv_k_a[o][None],
                seg=(feat[:, None] == feat[None, :]).astype(BF16))
            r_k = rwkv_r_k[o].reshape(1, d)
            lnx_g, lnx_b = rwkv_lnx_g[o][None], rwkv_lnx_b[o][None]
            w_o = bf(rwkv_w_o[o])

            streams, shift = _front1(hp, jnp.zeros((bp, d), F32), g0, prm, tm_p, sp)
            to3 = lambda x: x.reshape(bp, sp, d)
            gated, s_t = _scan([to3(x) for x in streams[:6]], to3(streams[6]),
                               jnp.zeros((bp, heads, RWKV_HD, RWKV_HD), F32), r_k, lnx_g, lnx_b)
            hp = _tail(hp, [gated.reshape(bp * sp, d)], [w_o], *tail_w, mem_k_p[l], mem_v_p[l], *tail_w2, tm_p, sp)
            shift_p.append(shift)
            wkv_p.append(s_t)

            streams, shift = _front1(hs, state_shift[o], g0, prm, tm_s, 1)
            pad = lambda x: jnp.pad(x[:, None, :], ((0, 0), (0, SCAN_CHUNK - 1), (0, 0)))
            gated, s_t = _scan([pad(x) for x in streams[:6]], pad(streams[6]), state_wkv[o], r_k, lnx_g, lnx_b)
            hs = _tail(hs, [gated[:, 0]], [w_o], *tail_w, mk_s, mv_s, *tail_w2, tm_s, 1)
            shift_s.append(shift)
            wkv_s.append(s_t)

    mem_shape = (bp, mem, MEM_HEADS, mem_dim // MEM_HEADS)
    return (hp.reshape(bp, sp, d), hs.reshape(bs, 1, d),
            jnp.stack(k_p), jnp.stack(v_p), jnp.stack(k_s), jnp.stack(v_s),
            jnp.stack(conv_p), jnp.stack(conv_s), jnp.stack(shift_p), jnp.stack(shift_s),
            jnp.stack(wkv_p), jnp.stack(wkv_s),
            jnp.stack([x.reshape(mem_shape) for x in mem_k_p]), jnp.stack([x.reshape(mem_shape) for x in mem_v_p]))
```

```python
import functools

import jax
import jax.numpy as jnp
from jax import lax
from jax.experimental import pallas as pl
from jax.experimental.pallas import tpu as pltpu

F32 = jnp.float32
BF16 = jnp.bfloat16

NORM_EPS = 1e-6
LN_EPS = 1e-5
GN_EPS = 64e-5
NEG_INF = -1e30
ATT_HD = 64
MEM_HEADS = 4
MEM_HD = 128
RWKV_HD = 64
MOBA_BLOCK = 256
MOBA_TOPK = 3
CONV_WIDTH = 31
ROPE_THETA = 10000.0

LANES = 128
SUBLANES = 8
MXU_DIM = 256
VMEM_LIMIT_BYTES = 56 * 1024 * 1024
SCAN_CHUNK = 64
SCAN_GROUP = MXU_DIM // RWKV_HD
HALO = 32
LOG2_E = 1.4426950408889634


def _cparams(sem):
    return pltpu.CompilerParams(dimension_semantics=sem, vmem_limit_bytes=VMEM_LIMIT_BYTES)


def _const_spec(shape):
    nd = len(shape)
    return pl.BlockSpec(shape, lambda *_: (0,) * nd, pipeline_mode=pl.Buffered(1))


def _rms(x, g):
    return x * lax.rsqrt(jnp.mean(x * x, axis=-1, keepdims=True) + NORM_EPS) * g


def _sigmoid(x):
    return 1.0 / (1.0 + jnp.exp(-x))


def _mm(a, b):
    return jnp.dot(a.astype(BF16), b.astype(BF16), preferred_element_type=F32)


def _mm_nt(a, b):
    return lax.dot_general(a.astype(BF16), b.astype(BF16), (((1,), (1,)), ((), ())),
                           preferred_element_type=F32)


def _mm_tn(a, b):
    return lax.dot_general(a.astype(BF16), b.astype(BF16), (((0,), (0,)), ((), ())),
                           preferred_element_type=F32)


def _split(x):
    hi = x.astype(BF16)
    lo = (x - hi.astype(F32)).astype(BF16)
    return hi, lo


def _mm2(a, b):
    hi, lo = _split(a)
    return (jnp.dot(hi, b, preferred_element_type=F32)
            + jnp.dot(lo, b, preferred_element_type=F32))


def _proj_kernel(x_ref, w_ref, o_ref):
    o_ref[...] = _mm(x_ref[...], w_ref[...])


def _proj(x, w, tm):
    m, kdim = x.shape
    n = w.shape[1]
    return pl.pallas_call(
        _proj_kernel,
        out_shape=jax.ShapeDtypeStruct((m, n), F32),
        grid=(m // tm,),
        in_specs=[pl.BlockSpec((tm, kdim), lambda i: (i, 0)), _const_spec((kdim, n))],
        out_specs=pl.BlockSpec((tm, n), lambda i: (i, 0)),
        compiler_params=_cparams(("parallel",)),
        name="proj",
    )(x, w)


def _rope(x, cos, sin):
    width = x.shape[1]
    lane = lax.broadcasted_iota(jnp.int32, x.shape, 1)
    first_half = (lane % ATT_HD) < (ATT_HD // 2)
    partner = jnp.where(first_half,
                        pltpu.roll(x, width - ATT_HD // 2, axis=1),
                        pltpu.roll(x, ATT_HD // 2, axis=1))
    return x * cos + partner * sin


def _front0_kernel(x_ref, g_ref, w_ref, cos_ref, sin_ref, u_ref, q_ref, k_ref, v_ref, *, conv_dim, att_dim):
    hn = _rms(x_ref[...], g_ref[...]).astype(BF16)
    c0 = conv_dim
    val = jnp.dot(hn, w_ref[:, 0:c0], preferred_element_type=F32)
    gate = jnp.dot(hn, w_ref[:, c0:2 * c0], preferred_element_type=F32)
    u_ref[...] = val * _sigmoid(gate)
    reps = att_dim // LANES
    cos = jnp.concatenate([cos_ref[...]] * reps, axis=1)
    sin = jnp.concatenate([sin_ref[...]] * reps, axis=1)
    o = 2 * c0
    q = jnp.dot(hn, w_ref[:, o:o + att_dim], preferred_element_type=F32)
    q_ref[...] = _rope(q, cos, sin)
    k = jnp.dot(hn, w_ref[:, o + att_dim:o + 2 * att_dim], preferred_element_type=F32)
    k_ref[...] = _rope(k, cos, sin)
    v_ref[...] = jnp.dot(hn, w_ref[:, o + 2 * att_dim:o + 3 * att_dim], preferred_element_type=F32)


def _front0(h, g, w_in, cos_tab, sin_tab, conv_dim, att_dim, tm):
    m, d = h.shape
    n_tab = cos_tab.shape[0] // tm
    row = lambda i: (i, 0)
    tab = lambda i: (i % n_tab, 0)
    return pl.pallas_call(
        functools.partial(_front0_kernel, conv_dim=conv_dim, att_dim=att_dim),
        out_shape=(jax.ShapeDtypeStruct((m, conv_dim), F32),) + (jax.ShapeDtypeStruct((m, att_dim), F32),) * 3,
        grid=(m // tm,),
        in_specs=[pl.BlockSpec((tm, d), row), _const_spec((1, d)), _const_spec(w_in.shape),
                  pl.BlockSpec((tm, LANES), tab), pl.BlockSpec((tm, LANES), tab)],
        out_specs=(pl.BlockSpec((tm, conv_dim), row),) + (pl.BlockSpec((tm, att_dim), row),) * 3,
        compiler_params=_cparams(("parallel",)),
        name="front0",
    )(h, g, w_in, cos_tab, sin_tab)


def _rope_tables(pos):
    half = ATT_HD // 2
    inv_freq = ROPE_THETA ** (-jnp.arange(half, dtype=F32) / half)
    ang = pos.astype(F32)[:, None] * inv_freq[None, :]
    cos, sin = jnp.cos(ang), jnp.sin(ang)
    reps = LANES // ATT_HD
    return (jnp.tile(jnp.concatenate([cos, cos], axis=1), (1, reps)),
            jnp.tile(jnp.concatenate([-sin, sin], axis=1), (1, reps)))


def _ln_silu(y, g, b):
    mu = jnp.mean(y, axis=-1, keepdims=True)
    yc = y - mu
    var = jnp.mean(yc * yc, axis=-1, keepdims=True)
    ln = yc * lax.rsqrt(var + LN_EPS) * g + b
    return ln * _sigmoid(ln)


def _conv_kernel(u_ref, uprev_ref, w_ref, b_ref, lg_ref, lb_ref, c_ref, full_scr, *, tc, sub):
    i = pl.program_id(1)
    full_scr[0:HALO, :] = jnp.where(i > 0, uprev_ref[0], 0.0)
    full_scr[HALO:HALO + tc, :] = u_ref[0]
    first = HALO - (CONV_WIDTH - 1)
    for r0 in range(0, tc, sub):
        acc = jnp.broadcast_to(b_ref[...], (sub, b_ref.shape[1]))
        for j in range(CONV_WIDTH):
            acc = acc + w_ref[j:j + 1, :] * full_scr[first + r0 + j:first + r0 + j + sub, :]
        c_ref[0, r0:r0 + sub, :] = _ln_silu(acc, lg_ref[...], lb_ref[...])


def _conv_prompt(u, w, b, lg, lb, tc):
    bsz, s, c = u.shape
    per = tc // HALO
    wp = jnp.pad(w, ((0, HALO - CONV_WIDTH), (0, 0)))
    return pl.pallas_call(
        functools.partial(_conv_kernel, tc=tc, sub=HALO),
        out_shape=jax.ShapeDtypeStruct((bsz, s, c), F32),
        grid=(bsz, s // tc),
        in_specs=[pl.BlockSpec((1, tc, c), lambda bi, i: (bi, i, 0)),
                  pl.BlockSpec((1, HALO, c), lambda bi, i: (bi, jnp.maximum(i * per - 1, 0), 0)),
                  _const_spec((HALO, c)), _const_spec((1, c)), _const_spec((1, c)), _const_spec((1, c))],
        out_specs=pl.BlockSpec((1, tc, c), lambda bi, i: (bi, i, 0)),
        scratch_shapes=[pltpu.VMEM((HALO + tc, c), F32)],
        compiler_params=_cparams(("parallel", "parallel")),
        name="conv_prompt",
    )(u, u, wp, b, lg, lb)


def _conv_step_kernel(st_ref, u_ref, w_ref, b_ref, lg_ref, lb_ref, c_ref):
    acc = b_ref[...] + w_ref[CONV_WIDTH - 1:CONV_WIDTH, :] * u_ref[...]
    for j in range(CONV_WIDTH - 1):
        acc = acc + w_ref[j:j + 1, :] * st_ref[j]
    c_ref[...] = _ln_silu(acc, lg_ref[...], lb_ref[...])


def _conv_step(state_t, u, w, b, lg, lb):
    bsz, c = u.shape
    wp = jnp.pad(w, ((0, HALO - CONV_WIDTH), (0, 0)))
    return pl.pallas_call(
        _conv_step_kernel,
        out_shape=jax.ShapeDtypeStruct((bsz, c), F32),
        grid=(1,),
        in_specs=[_const_spec(state_t.shape), _const_spec((bsz, c)), _const_spec((HALO, c)),
                  _const_spec((1, c)), _const_spec((1, c)), _const_spec((1, c))],
        out_specs=pl.BlockSpec((bsz, c), lambda i: (0, 0)),
        compiler_params=_cparams(("arbitrary",)),
        name="conv_step",
    )(state_t, u, wp, b, lg, lb)


def _top_blocks(gate, n_past):
    blk = lax.broadcasted_iota(jnp.int32, gate.shape, 0).astype(F32)
    g = jnp.where(blk < n_past.astype(F32), gate, NEG_INF)
    picks = []
    for r in range(MOBA_TOPK):
        m = jnp.max(g, axis=0, keepdims=True)
        idx = jnp.min(jnp.where(g == m, blk, 1e9), axis=0, keepdims=True)
        picks.append(jnp.where(r < n_past, idx, -1.0))
        g = jnp.where(blk == idx, -jnp.inf, g)
    return picks


def _moba_kernel(q_ref, k_ref, v_ref, o_ref, kmean_scr, kb_scr, vt_scr, s_scr, *, nb, scale):
    i = pl.program_id(2)
    blk = MOBA_BLOCK
    hpp = LANES // ATT_HD

    @pl.when(i == 0)
    def _():
        kmean_scr[...] = jnp.zeros_like(kmean_scr)
        for j in range(nb):
            rows = slice(j * blk, (j + 1) * blk)
            kj = k_ref[0, rows, :]
            kmean_scr[j:j + 1, :] = jnp.mean(kj, axis=0, keepdims=True)
            kb_scr[rows, :] = kj.astype(BF16)
            vt_scr[:, rows] = v_ref[0, rows, :].T.astype(BF16)

    q = q_ref[0] * (scale * LOG2_E)
    lane = lax.broadcasted_iota(jnp.int32, q.shape, 1)
    q2 = jnp.concatenate([jnp.where((lane >= hh * ATT_HD) & (lane < (hh + 1) * ATT_HD), q, 0.0)
                          for hh in range(hpp)], axis=0)
    q_hi, q_lo = _split(q2)
    km_hi, km_lo = _split(kmean_scr[...])
    nt = (((1,), (1,)), ((), ()))
    gate = (lax.dot_general(km_hi, q_hi, nt, preferred_element_type=F32)
            + lax.dot_general(km_lo, q_hi, nt, preferred_element_type=F32)
            + lax.dot_general(km_hi, q_lo, nt, preferred_element_type=F32))
    picks = _top_blocks(gate, i)

    ones = jnp.ones((SUBLANES, blk), BF16)

    def pv(off, p):
        vt_j = vt_scr[:, pl.ds(off, blk)]
        upd = [jnp.dot(vt_j[hh * ATT_HD:(hh + 1) * ATT_HD, :], p[:, hh * blk:(hh + 1) * blk],
                       preferred_element_type=F32) for hh in range(hpp)]
        return upd, jnp.dot(ones, p, preferred_element_type=F32)[0:1, :]

    def scores(off):
        return lax.dot_general(kb_scr[pl.ds(off, blk), :], q_hi, nt, preferred_element_type=F32)

    def offset(j):
        return pl.multiple_of(jnp.minimum(j, nb - 1) * blk, blk)

    def past_scores(j):
        jf = j.astype(F32)
        chosen = (picks[0] == jf) | (picks[1] == jf) | (picks[2] == jf)
        return jnp.where(chosen, scores(offset(j)), NEG_INF)

    start = offset(i)
    s = scores(start)
    key = lax.broadcasted_iota(jnp.int32, s.shape, 0)
    qry = lax.broadcasted_iota(jnp.int32, s.shape, 1) % blk
    s = jnp.where(key <= qry, s, NEG_INF)
    s_scr[0] = past_scores(jnp.int32(0))
    m0 = jnp.max(s, axis=0, keepdims=True)
    acc0, l0 = pv(start, jnp.exp2(s - m0).astype(BF16))

    def fold(j, slot, carry):
        m, l, acc = carry
        sj = s_scr[slot]
        m_new = jnp.maximum(m, jnp.max(sj, axis=0, keepdims=True))
        alpha = jnp.exp2(m - m_new)
        upd, psum = pv(offset(j), jnp.exp2(sj - m_new).astype(BF16))
        l = alpha * l + psum
        acc = [alpha[:, hh * blk:(hh + 1) * blk] * acc[hh] + upd[hh] for hh in range(hpp)]
        return m_new, l, acc

    def body(jj, carry):
        j = 2 * jj
        s_scr[1] = past_scores(j + 1)
        carry = fold(j, 0, carry)
        s_scr[0] = past_scores(j + 2)
        return fold(j + 1, 1, carry)

    _, l, acc = lax.fori_loop(0, (i + 1) // 2, body, (m0, l0, acc0))
    out_t = jnp.concatenate([acc[hh] / l[:, hh * blk:(hh + 1) * blk] for hh in range(hpp)], axis=0)
    o_ref[0] = out_t.T


def _moba_prompt(q, k, v):
    bsz, s, a = q.shape
    nb = s // MOBA_BLOCK
    blk = pl.BlockSpec((1, MOBA_BLOCK, LANES), lambda b, hp, i: (b, i, hp))
    seq = pl.BlockSpec((1, s, LANES), lambda b, hp, i: (b, 0, hp))
    nb_pad = -(-nb // SUBLANES) * SUBLANES
    return pl.pallas_call(
        functools.partial(_moba_kernel, nb=nb, scale=ATT_HD ** -0.5),
        out_shape=jax.ShapeDtypeStruct((bsz, s, a), F32),
        grid=(bsz, a // LANES, nb),
        in_specs=[blk, seq, seq],
        out_specs=blk,
        scratch_shapes=[pltpu.VMEM((nb_pad, LANES), F32), pltpu.VMEM((s, LANES), BF16),
                        pltpu.VMEM((LANES, s), BF16),
                        pltpu.VMEM((2, MOBA_BLOCK, (LANES // ATT_HD) * MOBA_BLOCK), F32)],
        compiler_params=_cparams(("parallel", "parallel", "arbitrary")),
        name="moba_prompt",
    )(q, k, v)


def _moba_gate_kernel(pt_ref, q_ref, *refs, nb, pages_per_step, pages_per_block):
    del pt_ref
    page_refs = refs[:pages_per_step]
    sel_ref, qb_scr, g_scr = refs[pages_per_step:]
    j = pl.program_id(1)
    blocks_per_step = pages_per_step // pages_per_block

    @pl.when(j == 0)
    def _():
        qb_scr[...] = jnp.broadcast_to(q_ref[0], qb_scr.shape)

    qb = qb_scr[...]
    for bl in range(blocks_per_step):
        ksum = page_refs[bl * pages_per_block][0]
        for pg in range(1, pages_per_block):
            ksum = ksum + page_refs[bl * pages_per_block + pg][0]
        g_scr[j * blocks_per_step + bl] = jnp.sum(ksum * qb, axis=1)

    @pl.when(j == nb // blocks_per_step - 1)
    def _():
        gate = jnp.sum(g_scr[...], axis=2, keepdims=True) * (1.0 / MOBA_BLOCK)
        blk = lax.broadcasted_iota(jnp.int32, gate.shape, 0).astype(F32)
        lane = lax.broadcasted_iota(jnp.int32, sel_ref.shape[1:], 1)
        sel = jnp.zeros(sel_ref.shape[1:], F32)
        for r in range(MOBA_TOPK):
            m = jnp.max(gate, axis=0, keepdims=True)
            idx = jnp.min(jnp.where(gate == m, blk, 1e9), axis=0, keepdims=True)
            sel = jnp.where(lane == r, idx[0], sel)
            gate = jnp.where(blk == idx, -jnp.inf, gate)
        sel_ref[0] = sel.astype(jnp.int32)


def _moba_gate(page_table, q_col, cache_kt, nb, pages_per_block, pages_per_step):
    bsz, heads, hd, _ = q_col.shape
    page = cache_kt.shape[3]
    steps = nb * pages_per_block // pages_per_step

    def page_spec(w):
        return pl.BlockSpec((1, heads, hd, page), lambda b, j, pt: (pt[b, j * pages_per_step + w], 0, 0, 0))

    grid_spec = pltpu.PrefetchScalarGridSpec(
        num_scalar_prefetch=1,
        grid=(bsz, steps),
        in_specs=[pl.BlockSpec((1, heads, hd, 1), lambda b, j, pt: (b, 0, 0, 0))]
        + [page_spec(w) for w in range(pages_per_step)],
        out_specs=pl.BlockSpec((1, heads, LANES), lambda b, j, pt: (b, 0, 0)),
        scratch_shapes=[pltpu.VMEM((heads, hd, page), F32), pltpu.VMEM((nb, heads, page), F32)],
    )
    return pl.pallas_call(
        functools.partial(_moba_gate_kernel, nb=nb, pages_per_step=pages_per_step, pages_per_block=pages_per_block),
        out_shape=jax.ShapeDtypeStruct((bsz, heads, LANES), jnp.int32),
        grid_spec=grid_spec,
        compiler_params=_cparams(("parallel", "arbitrary")),
        name="moba_gate",
    )(page_table, q_col, *([cache_kt] * pages_per_step))


def _moba_decode_kernel(pt_ref, sel_ref, q_ref, kn_ref, vn_ref, *refs, n_pages, scale):
    del pt_ref, sel_ref
    k_refs = refs[:n_pages]
    v_refs = refs[n_pages:2 * n_pages]
    o_ref = refs[2 * n_pages]
    q = q_ref[0, 0]
    s_own = jnp.sum(q * kn_ref[0, 0], axis=0, keepdims=True) * scale
    s = [jnp.sum(kr[0, 0] * q, axis=0, keepdims=True) * scale for kr in k_refs]
    m = s_own
    for sp in s:
        m = jnp.maximum(m, jnp.max(sp, axis=1, keepdims=True))
    p_own = jnp.exp(s_own - m)
    l = p_own
    acc = p_own * vn_ref[0, 0]
    for sp, vr in zip(s, v_refs):
        p = jnp.exp(sp - m)
        l = l + jnp.sum(p, axis=1, keepdims=True)
        acc = acc + jnp.sum(vr[0, 0] * p, axis=1, keepdims=True)
    o_ref[0, 0] = acc / l


def _moba_decode(page_table, sel, q_col, kn_col, vn_col, cache_kt, cache_vt, pages_per_block):
    bsz, heads, hd, _ = q_col.shape
    page = cache_kt.shape[3]
    n_pages = MOBA_TOPK * pages_per_block
    vec = pl.BlockSpec((1, 1, hd, 1), lambda b, h, pt, sl: (b, h, 0, 0))

    def page_spec(r, w):
        return pl.BlockSpec((1, 1, hd, page),
                            lambda b, h, pt, sl: (pt[b, pages_per_block * sl[b, h, r] + w], h, 0, 0))

    pages = [page_spec(r, w) for r in range(MOBA_TOPK) for w in range(pages_per_block)]
    grid_spec = pltpu.PrefetchScalarGridSpec(
        num_scalar_prefetch=2,
        grid=(bsz, heads),
        in_specs=[vec, vec, vec] + pages + pages,
        out_specs=vec,
    )
    return pl.pallas_call(
        functools.partial(_moba_decode_kernel, n_pages=n_pages, scale=ATT_HD ** -0.5),
        out_shape=jax.ShapeDtypeStruct((bsz, heads, hd, 1), F32),
        grid_spec=grid_spec,
        compiler_params=_cparams(("parallel", "parallel")),
        name="moba_decode",
    )(page_table, sel, q_col, kn_col, vn_col, *([cache_kt] * n_pages), *([cache_vt] * n_pages))


def _tail_kernel(*refs, n_mix, nb, s, ff_chunk):
    h_ref = refs[0]
    mix_refs = refs[1:1 + n_mix]
    wmix_refs = refs[1 + n_mix:1 + 2 * n_mix]
    g_ref, wcq_ref, mk_ref, mv_ref, wco_ref, wup_ref, wdn_ref, o_ref, ca_scr = refs[1 + 2 * n_mix:]

    mix = _mm(mix_refs[0][...], wmix_refs[0][...])
    for mr, wr in zip(mix_refs[1:], wmix_refs[1:]):
        mix = mix + _mm(mr[...], wr[...])
    h = h_ref[...] + _rms(mix, g_ref[1:2, :])

    q = _mm(_rms(h, g_ref[2:3, :]), wcq_ref[...])
    rows = max(s, SUBLANES)
    for bi in range(nb):
        qb = q[bi * s:(bi + 1) * s, :]
        if s < rows:
            qb = jnp.broadcast_to(qb[0:1, :], (rows, qb.shape[1]))
        for hh in range(MEM_HEADS):
            sl = slice(hh * MEM_HD, (hh + 1) * MEM_HD)
            sc = _mm_nt(qb[:, sl], mk_ref[bi, :, sl]) * (MEM_HD ** -0.5)
            p = jnp.exp(sc - jnp.max(sc, axis=1, keepdims=True))
            oh = _mm(p, mv_ref[bi, :, sl]) / jnp.sum(p, axis=1, keepdims=True)
            ca_scr[bi * s:(bi + 1) * s, sl] = oh[0:s, :]
    h = h + _rms(_mm(ca_scr[...], wco_ref[...]), g_ref[3:4, :])

    hn = _rms(h, g_ref[4:5, :]).astype(BF16)
    d_ff = wup_ref.shape[1]
    acc = jnp.zeros(h.shape, F32)
    for c0 in range(0, d_ff, ff_chunk):
        up = jnp.dot(hn, wup_ref[:, c0:c0 + ff_chunk], preferred_element_type=F32)
        up = jnp.square(jnp.maximum(up, 0.0))
        acc = acc + jnp.dot(up.astype(BF16), wdn_ref[c0:c0 + ff_chunk, :], preferred_element_type=F32)
    o_ref[...] = h + _rms(acc, g_ref[5:6, :])


def _tail(h, mixes, w_mixes, gains, w_cq, mk, mv, w_co, w_up, w_down, tm, seq_len):
    m, d = h.shape
    s = min(seq_len, tm)
    nb = tm // s
    n_mix = len(mixes)
    mem = mk.shape[1]
    mem_dim = mk.shape[2]
    row = lambda i: (i, 0)
    tiles_per_seq = max(seq_len // tm, 1)
    mem_map = lambda i: (i // tiles_per_seq, 0, 0)
    in_specs = [pl.BlockSpec((tm, d), row)]
    in_specs += [pl.BlockSpec((tm, x.shape[1]), row) for x in mixes]
    in_specs += [_const_spec(w.shape) for w in w_mixes]
    in_specs += [_const_spec(gains.shape), _const_spec(w_cq.shape),
                 pl.BlockSpec((nb, mem, mem_dim), mem_map), pl.BlockSpec((nb, mem, mem_dim), mem_map),
                 _const_spec(w_co.shape), _const_spec(w_up.shape), _const_spec(w_down.shape)]
    return pl.pallas_call(
        functools.partial(_tail_kernel, n_mix=n_mix, nb=nb, s=s, ff_chunk=min(1024, w_up.shape[1])),
        out_shape=jax.ShapeDtypeStruct((m, d), F32),
        grid=(m // tm,),
        in_specs=in_specs,
        out_specs=pl.BlockSpec((tm, d), row),
        scratch_shapes=[pltpu.VMEM((tm, mem_dim), F32)],
        compiler_params=_cparams(("parallel",)),
        name="tail",
    )(h, *mixes, *w_mixes, gains, w_cq, mk, mv, w_co, w_up, w_down)


def _front1_kernel(h_ref, hprev_ref, xlast_ref, g_ref, mu_ref, wr_ref, wk_ref, wv_ref, w0_ref, w1_ref, w2_ref,
                   a0_ref, a1_ref, a2_ref, g1_ref, g2_ref, kk_ref, ka_ref, seg_ref,
                   r_out, ld_out, k_out, v_out, kk_out, a_out, gt_out, sh_out, *, seq, tiles_per_seq):
    i = pl.program_id(0)
    g = g_ref[...]
    hn = _rms(h_ref[...], g)
    if seq:
        prev_row = _rms(hprev_ref[SUBLANES - 1:SUBLANES, :], g)
        prev_row = jnp.where(i % tiles_per_seq == 0, xlast_ref[0], prev_row)
        rowid = lax.broadcasted_iota(jnp.int32, hn.shape, 0)
        xprev = jnp.where(rowid == 0, prev_row, pltpu.roll(hn, 1, axis=0))
        sh_out[0] = hn[hn.shape[0] - 1:, :]
    else:
        xprev = xlast_ref[...]
        sh_out[...] = hn
    xx = xprev - hn
    mix = lambda n: hn + xx * mu_ref[n:n + 1, :]
    r = _mm(mix(0), wr_ref[...])
    k = _mm(mix(2), wk_ref[...])
    v = _mm(mix(3), wv_ref[...])
    wl = w0_ref[...] + _mm(jnp.tanh(_mm(mix(1), w1_ref[...])), w2_ref[...])
    z = -wl
    softplus = jnp.maximum(z, 0.0) + jnp.log(1.0 + jnp.exp(-jnp.abs(z)))
    ld_out[...] = -jnp.exp(-softplus - 0.5)
    a = _sigmoid(a0_ref[...] + _mm(_mm(mix(4), a1_ref[...]), a2_ref[...]))
    gt_out[...] = _mm(_sigmoid(_mm(mix(5), g1_ref[...])), g2_ref[...])
    kk = k * kk_ref[...]
    ss = _mm2(kk * kk, seg_ref[...])
    kk_out[...] = kk / jnp.maximum(jnp.sqrt(ss), 1e-12)
    k_out[...] = k * (1.0 + (a - 1.0) * ka_ref[...])
    r_out[...] = r
    v_out[...] = v
    a_out[...] = a


def _front1(h, x_last, g, p, tm, s):
    m, d = h.shape
    bsz = m // s
    seq = s > 1
    row = lambda i: (i, 0)
    big = jax.ShapeDtypeStruct((m, d), F32)
    if seq:
        tps = s // tm
        per = tm // SUBLANES
        hprev_spec = pl.BlockSpec((SUBLANES, d), lambda i: (jnp.maximum(i * per - 1, 0), 0))
        xlast = x_last.reshape(bsz, 1, d)
        xlast_spec = pl.BlockSpec((1, 1, d), lambda i: (i // tps, 0, 0))
        sh_shape = jax.ShapeDtypeStruct((bsz, 1, d), F32)
        sh_spec = pl.BlockSpec((1, 1, d), lambda i: (i // tps, 0, 0))
        sem = ("arbitrary",)
    else:
        tps = 1
        hprev_spec = pl.BlockSpec((SUBLANES, d), lambda i: (0, 0))
        xlast = x_last
        xlast_spec = pl.BlockSpec((tm, d), row)
        sh_shape = big
        sh_spec = pl.BlockSpec((tm, d), row)
        sem = ("parallel",)
    consts = [g, p["mu"], p["w_r"], p["w_k"], p["w_v"], p["w0"], p["w1"], p["w2"], p["a0"], p["a1"], p["a2"],
              p["g1"], p["g2"], p["k_k"], p["k_a"], p["seg"]]
    outs = pl.pallas_call(
        functools.partial(_front1_kernel, seq=seq, tiles_per_seq=tps),
        out_shape=(big,) * 7 + (sh_shape,),
        grid=(m // tm,),
        in_specs=[pl.BlockSpec((tm, d), row), hprev_spec, xlast_spec] + [_const_spec(c.shape) for c in consts],
        out_specs=(pl.BlockSpec((tm, d), row),) * 7 + (sh_spec,),
        compiler_params=_cparams(sem),
        name="front1",
    )(h, h, xlast, *consts)
    return outs[:7], outs[7].reshape(bsz, d)


def _scan_kernel(r_ref, ld_ref, k_ref, v_ref, kk_ref, a_ref, gt_ref, s0_ref, rk_ref, lg_ref, lb_ref,
                 o_ref, st_ref, s_scr, *, n_chunks, n_groups):
    c = pl.program_id(1)
    grp = SCAN_GROUP
    w = grp * RWKV_HD
    sbd = (lax.broadcasted_iota(jnp.int32, (w, w), 0) // RWKV_HD
           == lax.broadcasted_iota(jnp.int32, (w, w), 1) // RWKV_HD)

    @pl.when(c == 0)
    def _():
        for q in range(n_groups):
            s0 = s0_ref[0, q]
            s_scr[q] = jnp.where(sbd, jnp.concatenate([s0] * grp, axis=1), 0.0)

    _scan_chunk(r_ref[0], ld_ref[0], k_ref[0], v_ref[0], kk_ref[0], a_ref[0], gt_ref[0],
                rk_ref[...], lg_ref[...], lb_ref[...], o_ref, s_scr, n_groups)

    @pl.when(c == n_chunks - 1)
    def _():
        for q in range(n_groups):
            full = s_scr[q]
            acc = full[:, 0:RWKV_HD]
            for hh in range(1, grp):
                acc = acc + full[:, hh * RWKV_HD:(hh + 1) * RWKV_HD]
            st_ref[0, q] = acc


def _scan_chunk(r, ld, k, v, kk, a_gate, gt, r_k, lnx_g, lnx_b, o_ref, s_scr, n_groups):
    ch = SCAN_CHUNK
    grp = SCAN_GROUP
    w = grp * RWKV_HD
    rows = grp * ch
    gs = range(n_groups)
    rhead = lax.broadcasted_iota(jnp.int32, (rows, w), 0) // ch
    chead = lax.broadcasted_iota(jnp.int32, (rows, w), 1) // RWKV_HD
    bd = rhead == chead
    tile_r = lambda x: jnp.concatenate([x] * grp, axis=0)
    diag = lambda x: jnp.where(bd, tile_r(x), 0.0)
    part = lambda x, q: x[:, q * w:(q + 1) * w]

    b = kk * a_gate
    trow = lax.broadcasted_iota(jnp.int32, ld.shape, 0)
    cum = ld
    sh = 1
    while sh < ch:
        cum = cum + jnp.where(trow >= sh, pltpu.roll(cum, sh, axis=0), 0.0)
        sh *= 2
    clast = cum[ch - 1:ch, :]
    p_inv = jnp.exp(-cum)
    p_to = jnp.exp(clast - cum)
    rt = r * jnp.exp(cum)
    at = -kk * jnp.exp(cum - ld)
    bt = b * p_inv
    kt = k * p_inv
    bh = b * p_to
    kh = k * p_to
    p_end = jnp.exp(clast)

    tr = lax.broadcasted_iota(jnp.int32, (rows, rows), 0)
    tc = lax.broadcasted_iota(jnp.int32, (rows, rows), 1)
    same = (tr // ch) == (tc // ch)
    strict = same & (tc < tr)
    incl = same & (tc <= tr)
    eye = jnp.where(tr == tc, 1.0, 0.0)

    lhs = [jnp.concatenate([diag(part(at, q)), diag(part(rt, q))], axis=0).astype(BF16) for q in gs]
    rhs = [jnp.concatenate([tile_r(part(bt, q)), tile_r(part(kt, q))], axis=0) for q in gs]
    mall = [_mm_nt(lhs[q], rhs[q]) for q in gs]
    lp = [jnp.where(strict, mall[q][0:rows, 0:rows], 0.0) for q in gs]
    mak = [jnp.where(strict, mall[q][0:rows, rows:], 0.0) for q in gs]
    mrb = [jnp.where(incl, mall[q][rows:, 0:rows], 0.0) for q in gs]
    mrk = [jnp.where(incl, mall[q][rows:, rows:], 0.0) for q in gs]

    tinv = [eye + lp[q] for q in gs]
    n = 2
    while n < ch:
        lp = [_mm(lp[q], lp[q]) for q in gs]
        tinv = [tinv[q] + _mm(tinv[q], lp[q]) for q in gs]
        n *= 2

    vd = [diag(part(v, q)).astype(BF16) for q in gs]
    state = [s_scr[q] for q in gs]
    s0t = [_mm_nt(lhs[q], state[q]) for q in gs]
    wmat = [s0t[q][0:rows] + _mm(mak[q], vd[q]) for q in gs]
    u = [_mm(tinv[q], wmat[q]) for q in gs]
    ybd = [s0t[q][rows:] + _mm(mrb[q], u[q]) + _mm(mrk[q], vd[q]) for q in gs]
    upd = [_mm_tn(jnp.concatenate([u[q].astype(BF16), vd[q]], axis=0),
                  jnp.concatenate([diag(part(bh, q)), diag(part(kh, q))], axis=0)) for q in gs]
    for q in gs:
        s_scr[q] = state[q] * part(p_end, q) + upd[q]

    fa = lax.broadcasted_iota(jnp.int32, (w, w), 0) // RWKV_HD
    fb = lax.broadcasted_iota(jnp.int32, (w, w), 1) // RWKV_HD
    seg = jnp.where(fa == fb, 1.0, 0.0).astype(BF16)
    rkr = r * k * r_k
    for q in gs:
        y = ybd[q][0:ch]
        for hh in range(1, grp):
            y = y + ybd[q][hh * ch:(hh + 1) * ch]
        mean = _mm2(y, seg) * (1.0 / RWKV_HD)
        yc = y - mean
        var = _mm2(yc * yc, seg) * (1.0 / RWKV_HD)
        yn = yc * lax.rsqrt(var + GN_EPS) * part(lnx_g, q) + part(lnx_b, q)
        bonus = _mm2(part(rkr, q), seg) * part(v, q)
        o_ref[0, :, q * w:(q + 1) * w] = (yn + bonus) * part(gt, q)


def _scan(streams, gate, s0, r_k, lnx_g, lnx_b):
    bsz, t, d = gate.shape
    w = SCAN_GROUP * RWKV_HD
    nq = d // w
    nc = t // SCAN_CHUNK
    tok = pl.BlockSpec((1, SCAN_CHUNK, d), lambda b, c: (b, c, 0))
    st = pl.BlockSpec((1, nq, w, RWKV_HD), lambda b, c: (b, 0, 0, 0))
    s0g = s0.reshape(bsz, nq, w, RWKV_HD)
    out, s_t = pl.pallas_call(
        functools.partial(_scan_kernel, n_chunks=nc, n_groups=nq),
        out_shape=(jax.ShapeDtypeStruct((bsz, t, d), F32), jax.ShapeDtypeStruct(s0g.shape, F32)),
        grid=(bsz, nc),
        in_specs=[tok] * 7 + [st, _const_spec((1, d)), _const_spec((1, d)), _const_spec((1, d))],
        out_specs=(tok, st),
        scratch_shapes=[pltpu.VMEM((nq, w, w), F32)],
        compiler_params=_cparams(("parallel", "arbitrary")),
        name="rwkv_scan",
    )(*streams, gate, s0g, r_k, lnx_g, lnx_b)
    return out, s_t.reshape(s0.shape)


def _row_tile(m, cap):
    t = min(m, cap)
    assert m % t == 0
    return t


def _largest_divisor(n, cap):
    return max(t for t in range(1, cap + 1) if n % t == 0)


def _pad_cols(w, to):
    return jnp.pad(w, ((0, 0), (0, to - w.shape[1])))


def _pad_rows(w, to):
    return jnp.pad(w, ((0, to - w.shape[0]), (0, 0)))


def kernel(x_prompt, x_sample, mem_prompt, cache_k, cache_v, page_table, cache_mem_k, cache_mem_v, state_conv, state_shift, state_wkv, norm_gain, w_in, conv_w, conv_b, conv_ln_g, conv_ln_b, w_out, rwkv_mu, rwkv_w_r, rwkv_w_k, rwkv_w_v, rwkv_w_o, rwkv_w0, rwkv_w1, rwkv_w2, rwkv_a0, rwkv_a1, rwkv_a2, rwkv_g1, rwkv_g2, rwkv_k_k, rwkv_k_a, rwkv_r_k, rwkv_lnx_g, rwkv_lnx_b, w_cq, w_ck, w_cv, w_co, w_up, w_down):
    bp, sp, d = x_prompt.shape
    bs, ss, _ = x_sample.shape
    assert ss == 1
    depth = norm_gain.shape[0]
    conv_dim = conv_w.shape[2]
    att_dim = (w_in.shape[2] - 2 * conv_dim) // 3
    mem = mem_prompt.shape[1]
    mem_dim = w_ck.shape[2]
    n_pages, page = page_table.shape[1], cache_k.shape[2]
    past_len = n_pages * page
    pages_per_block = MOBA_BLOCK // page
    att_heads = att_dim // ATT_HD
    nb_past = past_len // MOBA_BLOCK
    assert past_len % MOBA_BLOCK == 0 and nb_past >= MOBA_TOPK and sp % MOBA_BLOCK == 0
    heads = d // RWKV_HD

    tm_p = _row_tile(sp, 512)
    tm_s = bs
    bf = lambda x: x.astype(BF16)

    w_mem = bf(jnp.concatenate([w for l in range(depth) for w in (w_ck[l], w_cv[l])], axis=1))
    memkv = _proj(mem_prompt.reshape(bp * mem, d), w_mem, _row_tile(bp * mem, 512))
    memkv = memkv.reshape(bp, mem, depth, 2, mem_dim)
    mem_k_p = [memkv[:, :, l, 0] for l in range(depth)]
    mem_v_p = [memkv[:, :, l, 1] for l in range(depth)]

    cos_p, sin_p = _rope_tables(jnp.arange(sp, dtype=jnp.int32))
    cos_s, sin_s = _rope_tables(jnp.full((tm_s,), past_len, dtype=jnp.int32))

    hp = x_prompt.reshape(bp * sp, d)
    hs = x_sample.reshape(bs, d)
    k_p, v_p, k_s, v_s, conv_p, conv_s, shift_p, shift_s, wkv_p, wkv_s = ([] for _ in range(10))

    for l in range(depth):
        gains = norm_gain[l]
        g0 = gains[0:1]
        tail_w = (gains, bf(w_cq[l]))
        tail_w2 = (bf(w_co[l]), bf(w_up[l]), bf(w_down[l]))
        mk_s = cache_mem_k[l].reshape(bs, mem, mem_dim)
        mv_s = cache_mem_v[l].reshape(bs, mem, mem_dim)
        if l % 2 == 0:
            e = l // 2
            w_in_b = bf(w_in[e])
            w_o = bf(w_out[e])
            cb, lg, lb = conv_b[e][None], conv_ln_g[e][None], conv_ln_b[e][None]

            u, q, k, v = _front0(hp, g0, w_in_b, cos_p, sin_p, conv_dim, att_dim, tm_p)
            u3 = u.reshape(bp, sp, conv_dim)
            c = _conv_prompt(u3, conv_w[e], cb, lg, lb, _row_tile(sp, 256))
            att = _moba_prompt(q.reshape(bp, sp, att_dim), k.reshape(bp, sp, att_dim), v.reshape(bp, sp, att_dim))
            hp = _tail(hp, [c.reshape(bp * sp, conv_dim), att.reshape(bp * sp, att_dim)],
                       [w_o[:conv_dim], w_o[conv_dim:]], *tail_w, mem_k_p[l], mem_v_p[l], *tail_w2, tm_p, sp)
            k_p.append(k.reshape(bp, sp, att_dim // ATT_HD, ATT_HD))
            v_p.append(v.reshape(bp, sp, att_dim // ATT_HD, ATT_HD))
            conv_p.append(u3[:, sp - (CONV_WIDTH - 1):])

            u, q, k, v = _front0(hs, g0, w_in_b, cos_s, sin_s, conv_dim, att_dim, tm_s)
            c = _conv_step(jnp.swapaxes(state_conv[e], 0, 1), u, conv_w[e], cb, lg, lb)
            ckt = jnp.transpose(cache_k[e], (0, 2, 3, 1))
            cvt = jnp.transpose(cache_v[e], (0, 2, 3, 1))
            col = lambda x: x.reshape(bs, att_heads, ATT_HD, 1)
            pages_per_step = pages_per_block * _largest_divisor(nb_past, 4)
            sel = _moba_gate(page_table, col(q), ckt, nb_past, pages_per_block, pages_per_step)
            att = _moba_decode(page_table, sel[:, :, :MOBA_TOPK], col(q), col(k), col(v), ckt, cvt,
                               pages_per_block).reshape(bs, att_dim)
            hs = _tail(hs, [c, att], [w_o[:conv_dim], w_o[conv_dim:]], *tail_w, mk_s, mv_s, *tail_w2, tm_s, 1)
            k_s.append(k.reshape(bs, 1, att_dim // ATT_HD, ATT_HD))
            v_s.append(v.reshape(bs, 1, att_dim // ATT_HD, ATT_HD))
            conv_s.append(jnp.concatenate([state_conv[e][:, 1:], u[:, None]], axis=1))
        else:
            o = l // 2
            lora = LANES
            gl = MXU_DIM
            feat = jnp.arange(d) // RWKV_HD
            prm = dict(
                mu=rwkv_mu[o], w_r=bf(rwkv_w_r[o]), w_k=bf(rwkv_w_k[o]), w_v=bf(rwkv_w_v[o]),
                w0=rwkv_w0[o][None], w1=bf(_pad_cols(rwkv_w1[o], lora)), w2=bf(_pad_rows(rwkv_w2[o], lora)),
                a0=rwkv_a0[o][None], a1=bf(_pad_cols(rwkv_a1[o], lora)), a2=bf(_pad_rows(rwkv_a2[o], lora)),
                g1=bf(_pad_cols(rwkv_g1[o], gl)), g2=bf(_pad_rows(rwkv_g2[o], gl)),
                k_k=rwkv_k_k[o][None], k_a=rwkv_k_a[o][None],
                seg=(feat[:, None] == feat[None, :]).astype(BF16))
            r_k = rwkv_r_k[o].reshape(1, d)
            lnx_g, lnx_b = rwkv_lnx_g[o][None], rwkv_lnx_b[o][None]
            w_o = bf(rwkv_w_o[o])

            streams, shift = _front1(hp, jnp.zeros((bp, d), F32), g0, prm, tm_p, sp)
            to3 = lambda x: x.reshape(bp, sp, d)
            gated, s_t = _scan([to3(x) for x in streams[:6]], to3(streams[6]),
                               jnp.zeros((bp, heads, RWKV_HD, RWKV_HD), F32), r_k, lnx_g, lnx_b)
            hp = _tail(hp, [gated.reshape(bp * sp, d)], [w_o], *tail_w, mem_k_p[l], mem_v_p[l], *tail_w2, tm_p, sp)
            shift_p.append(shift)
            wkv_p.append(s_t)

            streams, shift = _front1(hs, state_shift[o], g0, prm, tm_s, 1)
            pad = lambda x: jnp.pad(x[:, None, :], ((0, 0), (0, SCAN_CHUNK - 1), (0, 0)))
            gated, s_t = _scan([pad(x) for x in streams[:6]], pad(streams[6]), state_wkv[o], r_k, lnx_g, lnx_b)
            hs = _tail(hs, [gated[:, 0]], [w_o], *tail_w, mk_s, mv_s, *tail_w2, tm_s, 1)
            shift_s.append(shift)
            wkv_s.append(s_t)

    mem_shape = (bp, mem, MEM_HEADS, mem_dim // MEM_HEADS)
    return (hp.reshape(bp, sp, d), hs.reshape(bs, 1, d),
            jnp.stack(k_p), jnp.stack(v_p), jnp.stack(k_s), jnp.stack(v_s),
            jnp.stack(conv_p), jnp.stack(conv_s), jnp.stack(shift_p), jnp.stack(shift_s),
            jnp.stack(wkv_p), jnp.stack(wkv_s),
            jnp.stack([x.reshape(mem_shape) for x in mem_k_p]), jnp.stack([x.reshape(mem_shape) for x in mem_v_p]))
```

```python
import functools

import jax
import jax.numpy as jnp
from jax import lax
from jax.experimental import pallas as pl
from jax.experimental.pallas import tpu as pltpu

F32 = jnp.float32
BF16 = jnp.bfloat16

NORM_EPS = 1e-6
LN_EPS = 1e-5
GN_EPS = 64e-5
NEG_INF = -1e30
ATT_HD = 64
MEM_HEADS = 4
MEM_HD = 128
RWKV_HD = 64
MOBA_BLOCK = 256
MOBA_TOPK = 3
CONV_WIDTH = 31
ROPE_THETA = 10000.0

LANES = 128
SUBLANES = 8
MXU_DIM = 256
VMEM_LIMIT_BYTES = 56 * 1024 * 1024
SCAN_CHUNK = 64
SCAN_GROUP = MXU_DIM // RWKV_HD
HALO = 32
LOG2_E = 1.4426950408889634
GATE_BLOCKS_PER_STEP = 8


def _cparams(sem):
    return pltpu.CompilerParams(dimension_semantics=sem, vmem_limit_bytes=VMEM_LIMIT_BYTES)


def _const_spec(shape):
    nd = len(shape)
    return pl.BlockSpec(shape, lambda *_: (0,) * nd, pipeline_mode=pl.Buffered(1))


def _rms(x, g):
    return x * lax.rsqrt(jnp.mean(x * x, axis=-1, keepdims=True) + NORM_EPS) * g


def _sigmoid(x):
    return 1.0 / (1.0 + jnp.exp(-x))


def _mm(a, b):
    return jnp.dot(a.astype(BF16), b.astype(BF16), preferred_element_type=F32)


def _mm_nt(a, b):
    return lax.dot_general(a.astype(BF16), b.astype(BF16), (((1,), (1,)), ((), ())),
                           preferred_element_type=F32)


def _mm_tn(a, b):
    return lax.dot_general(a.astype(BF16), b.astype(BF16), (((0,), (0,)), ((), ())),
                           preferred_element_type=F32)


def _split(x):
    hi = x.astype(BF16)
    lo = (x - hi.astype(F32)).astype(BF16)
    return hi, lo


def _mm2(a, b):
    hi, lo = _split(a)
    return (jnp.dot(hi, b, preferred_element_type=F32)
            + jnp.dot(lo, b, preferred_element_type=F32))


def _proj_kernel(x_ref, w_ref, o_ref):
    o_ref[...] = _mm(x_ref[...], w_ref[...])


def _proj(x, w, tm):
    m, kdim = x.shape
    n = w.shape[1]
    return pl.pallas_call(
        _proj_kernel,
        out_shape=jax.ShapeDtypeStruct((m, n), F32),
        grid=(m // tm,),
        in_specs=[pl.BlockSpec((tm, kdim), lambda i: (i, 0)), _const_spec((kdim, n))],
        out_specs=pl.BlockSpec((tm, n), lambda i: (i, 0)),
        compiler_params=_cparams(("parallel",)),
        name="proj",
    )(x, w)


def _rope(x, cos, sin):
    width = x.shape[1]
    lane = lax.broadcasted_iota(jnp.int32, x.shape, 1)
    first_half = (lane % ATT_HD) < (ATT_HD // 2)
    partner = jnp.where(first_half,
                        pltpu.roll(x, width - ATT_HD // 2, axis=1),
                        pltpu.roll(x, ATT_HD // 2, axis=1))
    return x * cos + partner * sin


def _front0_kernel(x_ref, g_ref, w_ref, cos_ref, sin_ref, u_ref, q_ref, k_ref, v_ref, *, conv_dim, att_dim):
    hn = _rms(x_ref[...], g_ref[...]).astype(BF16)
    c0 = conv_dim
    val = jnp.dot(hn, w_ref[:, 0:c0], preferred_element_type=F32)
    gate = jnp.dot(hn, w_ref[:, c0:2 * c0], preferred_element_type=F32)
    u_ref[...] = val * _sigmoid(gate)
    reps = att_dim // LANES
    cos = jnp.concatenate([cos_ref[...]] * reps, axis=1)
    sin = jnp.concatenate([sin_ref[...]] * reps, axis=1)
    o = 2 * c0
    q = jnp.dot(hn, w_ref[:, o:o + att_dim], preferred_element_type=F32)
    q_ref[...] = _rope(q, cos, sin)
    k = jnp.dot(hn, w_ref[:, o + att_dim:o + 2 * att_dim], preferred_element_type=F32)
    k_ref[...] = _rope(k, cos, sin)
    v_ref[...] = jnp.dot(hn, w_ref[:, o + 2 * att_dim:o + 3 * att_dim], preferred_element_type=F32)


def _front0(h, g, w_in, cos_tab, sin_tab, conv_dim, att_dim, tm):
    m, d = h.shape
    n_tab = cos_tab.shape[0] // tm
    row = lambda i: (i, 0)
    tab = lambda i: (i % n_tab, 0)
    return pl.pallas_call(
        functools.partial(_front0_kernel, conv_dim=conv_dim, att_dim=att_dim),
        out_shape=(jax.ShapeDtypeStruct((m, conv_dim), F32),) + (jax.ShapeDtypeStruct((m, att_dim), F32),) * 3,
        grid=(m // tm,),
        in_specs=[pl.BlockSpec((tm, d), row), _const_spec((1, d)), _const_spec(w_in.shape),
                  pl.BlockSpec((tm, LANES), tab), pl.BlockSpec((tm, LANES), tab)],
        out_specs=(pl.BlockSpec((tm, conv_dim), row),) + (pl.BlockSpec((tm, att_dim), row),) * 3,
        compiler_params=_cparams(("parallel",)),
        name="front0",
    )(h, g, w_in, cos_tab, sin_tab)


def _rope_tables(pos):
    half = ATT_HD // 2
    inv_freq = ROPE_THETA ** (-jnp.arange(half, dtype=F32) / half)
    ang = pos.astype(F32)[:, None] * inv_freq[None, :]
    cos, sin = jnp.cos(ang), jnp.sin(ang)
    reps = LANES // ATT_HD
    return (jnp.tile(jnp.concatenate([cos, cos], axis=1), (1, reps)),
            jnp.tile(jnp.concatenate([-sin, sin], axis=1), (1, reps)))


def _ln_silu(y, g, b):
    mu = jnp.mean(y, axis=-1, keepdims=True)
    yc = y - mu
    var = jnp.mean(yc * yc, axis=-1, keepdims=True)
    ln = yc * lax.rsqrt(var + LN_EPS) * g + b
    return ln * _sigmoid(ln)


def _conv_kernel(u_ref, uprev_ref, w_ref, b_ref, lg_ref, lb_ref, c_ref, full_scr, *, tc, sub):
    i = pl.program_id(1)
    full_scr[0:HALO, :] = jnp.where(i > 0, uprev_ref[0], 0.0)
    full_scr[HALO:HALO + tc, :] = u_ref[0]
    first = HALO - (CONV_WIDTH - 1)
    for r0 in range(0, tc, sub):
        acc = jnp.broadcast_to(b_ref[...], (sub, b_ref.shape[1]))
        for res in range(SUBLANES):
            rows = sub + (SUBLANES if res else 0)
            z = None
            for j in range(CONV_WIDTH):
                if (first + j) % SUBLANES != res:
                    continue
                base = r0 + first + j - res
                term = w_ref[j:j + 1, :] * full_scr[base:base + rows, :]
                z = term if z is None else z + term
            acc = acc + z[res:res + sub, :]
        c_ref[0, r0:r0 + sub, :] = _ln_silu(acc, lg_ref[...], lb_ref[...])


def _conv_prompt(u, w, b, lg, lb, tc):
    bsz, s, c = u.shape
    per = tc // HALO
    wp = jnp.pad(w, ((0, HALO - CONV_WIDTH), (0, 0)))
    return pl.pallas_call(
        functools.partial(_conv_kernel, tc=tc, sub=HALO),
        out_shape=jax.ShapeDtypeStruct((bsz, s, c), F32),
        grid=(bsz, s // tc),
        in_specs=[pl.BlockSpec((1, tc, c), lambda bi, i: (bi, i, 0)),
                  pl.BlockSpec((1, HALO, c), lambda bi, i: (bi, jnp.maximum(i * per - 1, 0), 0)),
                  _const_spec((HALO, c)), _const_spec((1, c)), _const_spec((1, c)), _const_spec((1, c))],
        out_specs=pl.BlockSpec((1, tc, c), lambda bi, i: (bi, i, 0)),
        scratch_shapes=[pltpu.VMEM((HALO + tc, c), F32)],
        compiler_params=_cparams(("parallel", "parallel")),
        name="conv_prompt",
    )(u, u, wp, b, lg, lb)


def _conv_step_kernel(st_ref, u_ref, w_ref, b_ref, lg_ref, lb_ref, c_ref):
    acc = b_ref[...] + w_ref[CONV_WIDTH - 1:CONV_WIDTH, :] * u_ref[...]
    for j in range(CONV_WIDTH - 1):
        acc = acc + w_ref[j:j + 1, :] * st_ref[j]
    c_ref[...] = _ln_silu(acc, lg_ref[...], lb_ref[...])


def _conv_step(state_t, u, w, b, lg, lb):
    bsz, c = u.shape
    wp = jnp.pad(w, ((0, HALO - CONV_WIDTH), (0, 0)))
    return pl.pallas_call(
        _conv_step_kernel,
        out_shape=jax.ShapeDtypeStruct((bsz, c), F32),
        grid=(1,),
        in_specs=[_const_spec(state_t.shape), _const_spec((bsz, c)), _const_spec((HALO, c)),
                  _const_spec((1, c)), _const_spec((1, c)), _const_spec((1, c))],
        out_specs=pl.BlockSpec((bsz, c), lambda i: (0, 0)),
        compiler_params=_cparams(("arbitrary",)),
        name="conv_step",
    )(state_t, u, wp, b, lg, lb)


def _top_blocks(gate, n_past):
    blk = lax.broadcasted_iota(jnp.int32, gate.shape, 0).astype(F32)
    g = jnp.where(blk < n_past.astype(F32), gate, NEG_INF)
    picks = []
    for r in range(MOBA_TOPK):
        m = jnp.max(g, axis=0, keepdims=True)
        idx = jnp.min(jnp.where(g == m, blk, 1e9), axis=0, keepdims=True)
        picks.append(jnp.where(r < n_past, idx, -1.0))
        g = jnp.where(blk == idx, -jnp.inf, g)
    return picks


def _moba_kernel(q_ref, k_ref, v_ref, o_ref, kmean_scr, kb_scr, vt_scr, s_scr, *, nb, scale):
    i = pl.program_id(2)
    blk = MOBA_BLOCK
    hpp = LANES // ATT_HD

    @pl.when(i == 0)
    def _():
        kmean_scr[...] = jnp.zeros_like(kmean_scr)
        for j in range(nb):
            rows = slice(j * blk, (j + 1) * blk)
            kj = k_ref[0, rows, :]
            kmean_scr[j:j + 1, :] = jnp.mean(kj, axis=0, keepdims=True)
            kb_scr[rows, :] = kj.astype(BF16)
            vt_scr[:, rows] = v_ref[0, rows, :].T.astype(BF16)

    q = q_ref[0] * (scale * LOG2_E)
    lane = lax.broadcasted_iota(jnp.int32, q.shape, 1)
    q2 = jnp.concatenate([jnp.where((lane >= hh * ATT_HD) & (lane < (hh + 1) * ATT_HD), q, 0.0)
                          for hh in range(hpp)], axis=0)
    q_hi, q_lo = _split(q2)
    km_hi, km_lo = _split(kmean_scr[...])
    nt = (((1,), (1,)), ((), ()))
    gate = (lax.dot_general(km_hi, q_hi, nt, preferred_element_type=F32)
            + lax.dot_general(km_lo, q_hi, nt, preferred_element_type=F32)
            + lax.dot_general(km_hi, q_lo, nt, preferred_element_type=F32))
    picks = _top_blocks(gate, i)

    ones = jnp.ones((SUBLANES, blk), BF16)

    def pv(off, p):
        vt_j = vt_scr[:, pl.ds(off, blk)]
        upd = [jnp.dot(vt_j[hh * ATT_HD:(hh + 1) * ATT_HD, :], p[:, hh * blk:(hh + 1) * blk],
                       preferred_element_type=F32) for hh in range(hpp)]
        return upd, jnp.dot(ones, p, preferred_element_type=F32)[0:1, :]

    def scores(off):
        return lax.dot_general(kb_scr[pl.ds(off, blk), :], q_hi, nt, preferred_element_type=F32)

    def offset(j):
        return pl.multiple_of(jnp.minimum(j, nb - 1) * blk, blk)

    def past_scores(j):
        jf = j.astype(F32)
        chosen = (picks[0] == jf) | (picks[1] == jf) | (picks[2] == jf)
        return jnp.where(chosen, scores(offset(j)), NEG_INF)

    start = offset(i)
    s = scores(start)
    key = lax.broadcasted_iota(jnp.int32, s.shape, 0)
    qry = lax.broadcasted_iota(jnp.int32, s.shape, 1) % blk
    s = jnp.where(key <= qry, s, NEG_INF)
    s_scr[0] = past_scores(jnp.int32(0))
    m0 = jnp.max(s, axis=0, keepdims=True)
    acc0, l0 = pv(start, jnp.exp2(s - m0).astype(BF16))

    def fold(j, slot, carry):
        m, l, acc = carry
        sj = s_scr[slot]
        m_new = jnp.maximum(m, jnp.max(sj, axis=0, keepdims=True))
        alpha = jnp.exp2(m - m_new)
        upd, psum = pv(offset(j), jnp.exp2(sj - m_new).astype(BF16))
        l = alpha * l + psum
        acc = [alpha[:, hh * blk:(hh + 1) * blk] * acc[hh] + upd[hh] for hh in range(hpp)]
        return m_new, l, acc

    def body(jj, carry):
        j = 2 * jj
        s_scr[1] = past_scores(j + 1)
        carry = fold(j, 0, carry)
        s_scr[0] = past_scores(j + 2)
        return fold(j + 1, 1, carry)

    _, l, acc = lax.fori_loop(0, (i + 1) // 2, body, (m0, l0, acc0))
    out_t = jnp.concatenate([acc[hh] / l[:, hh * blk:(hh + 1) * blk] for hh in range(hpp)], axis=0)
    o_ref[0] = out_t.T


def _moba_prompt(q, k, v):
    bsz, s, a = q.shape
    nb = s // MOBA_BLOCK
    blk = pl.BlockSpec((1, MOBA_BLOCK, LANES), lambda b, hp, i: (b, i, hp))
    seq = pl.BlockSpec((1, s, LANES), lambda b, hp, i: (b, 0, hp))
    nb_pad = -(-nb // SUBLANES) * SUBLANES
    return pl.pallas_call(
        functools.partial(_moba_kernel, nb=nb, scale=ATT_HD ** -0.5),
        out_shape=jax.ShapeDtypeStruct((bsz, s, a), F32),
        grid=(bsz, a // LANES, nb),
        in_specs=[blk, seq, seq],
        out_specs=blk,
        scratch_shapes=[pltpu.VMEM((nb_pad, LANES), F32), pltpu.VMEM((s, LANES), BF16),
                        pltpu.VMEM((LANES, s), BF16),
                        pltpu.VMEM((2, MOBA_BLOCK, (LANES // ATT_HD) * MOBA_BLOCK), F32)],
        compiler_params=_cparams(("parallel", "parallel", "arbitrary")),
        name="moba_prompt",
    )(q, k, v)


def _moba_gate_kernel(pt_ref, q_ref, *refs, nb, pages_per_step, pages_per_block):
    del pt_ref
    page_refs = refs[:pages_per_step]
    sel_ref, qb_scr, g_scr = refs[pages_per_step:]
    j = pl.program_id(1)
    blocks_per_step = pages_per_step // pages_per_block

    @pl.when(j == 0)
    def _():
        qb_scr[...] = jnp.broadcast_to(q_ref[0], qb_scr.shape)

    qb = qb_scr[...]
    for bl in range(blocks_per_step):
        ksum = page_refs[bl * pages_per_block][0]
        for pg in range(1, pages_per_block):
            ksum = ksum + page_refs[bl * pages_per_block + pg][0]
        g_scr[j * blocks_per_step + bl] = jnp.sum(ksum * qb, axis=1)

    @pl.when(j == nb // blocks_per_step - 1)
    def _():
        gate = jnp.sum(g_scr[...], axis=2, keepdims=True) * (1.0 / MOBA_BLOCK)
        blk = lax.broadcasted_iota(jnp.int32, gate.shape, 0).astype(F32)
        lane = lax.broadcasted_iota(jnp.int32, sel_ref.shape[1:], 1)
        sel = jnp.zeros(sel_ref.shape[1:], F32)
        for r in range(MOBA_TOPK):
            m = jnp.max(gate, axis=0, keepdims=True)
            idx = jnp.min(jnp.where(gate == m, blk, 1e9), axis=0, keepdims=True)
            sel = jnp.where(lane == r, idx[0], sel)
            gate = jnp.where(blk == idx, -jnp.inf, gate)
        sel_ref[0] = sel.astype(jnp.int32)


def _moba_gate(page_table, q_col, cache_kt, nb, pages_per_block, pages_per_step):
    bsz, heads, hd, _ = q_col.shape
    page = cache_kt.shape[3]
    steps = nb * pages_per_block // pages_per_step

    def page_spec(w):
        return pl.BlockSpec((1, heads, hd, page), lambda b, j, pt: (pt[b, j * pages_per_step + w], 0, 0, 0))

    grid_spec = pltpu.PrefetchScalarGridSpec(
        num_scalar_prefetch=1,
        grid=(bsz, steps),
        in_specs=[pl.BlockSpec((1, heads, hd, 1), lambda b, j, pt: (b, 0, 0, 0))]
        + [page_spec(w) for w in range(pages_per_step)],
        out_specs=pl.BlockSpec((1, heads, LANES), lambda b, j, pt: (b, 0, 0)),
        scratch_shapes=[pltpu.VMEM((heads, hd, page), F32), pltpu.VMEM((nb, heads, page), F32)],
    )
    return pl.pallas_call(
        functools.partial(_moba_gate_kernel, nb=nb, pages_per_step=pages_per_step, pages_per_block=pages_per_block),
        out_shape=jax.ShapeDtypeStruct((bsz, heads, LANES), jnp.int32),
        grid_spec=grid_spec,
        compiler_params=_cparams(("parallel", "arbitrary")),
        name="moba_gate",
    )(page_table, q_col, *([cache_kt] * pages_per_step))


def _moba_decode_kernel(pt_ref, sel_ref, q_ref, kn_ref, vn_ref, *refs, n_pages, scale):
    del pt_ref, sel_ref
    k_refs = refs[:n_pages]
    v_refs = refs[n_pages:2 * n_pages]
    o_ref = refs[2 * n_pages]
    q = q_ref[0, 0]
    s_own = jnp.sum(q * kn_ref[0, 0], axis=0, keepdims=True) * scale
    s = [jnp.sum(kr[0, 0] * q, axis=0, keepdims=True) * scale for kr in k_refs]
    m = s_own
    for sp in s:
        m = jnp.maximum(m, jnp.max(sp, axis=1, keepdims=True))
    p_own = jnp.exp(s_own - m)
    l = p_own
    acc = p_own * vn_ref[0, 0]
    for sp, vr in zip(s, v_refs):
        p = jnp.exp(sp - m)
        l = l + jnp.sum(p, axis=1, keepdims=True)
        acc = acc + jnp.sum(vr[0, 0] * p, axis=1, keepdims=True)
    o_ref[0, 0] = acc / l


def _moba_decode(page_table, sel, q_col, kn_col, vn_col, cache_kt, cache_vt, pages_per_block):
    bsz, heads, hd, _ = q_col.shape
    page = cache_kt.shape[3]
    n_pages = MOBA_TOPK * pages_per_block
    vec = pl.BlockSpec((1, 1, hd, 1), lambda b, h, pt, sl: (b, h, 0, 0))

    def page_spec(r, w):
        return pl.BlockSpec((1, 1, hd, page),
                            lambda b, h, pt, sl: (pt[b, pages_per_block * sl[b, h, r] + w], h, 0, 0))

    pages = [page_spec(r, w) for r in range(MOBA_TOPK) for w in range(pages_per_block)]
    grid_spec = pltpu.PrefetchScalarGridSpec(
        num_scalar_prefetch=2,
        grid=(bsz, heads),
        in_specs=[vec, vec, vec] + pages + pages,
        out_specs=vec,
    )
    return pl.pallas_call(
        functools.partial(_moba_decode_kernel, n_pages=n_pages, scale=ATT_HD ** -0.5),
        out_shape=jax.ShapeDtypeStruct((bsz, heads, hd, 1), F32),
        grid_spec=grid_spec,
        compiler_params=_cparams(("parallel", "parallel")),
        name="moba_decode",
    )(page_table, sel, q_col, kn_col, vn_col, *([cache_kt] * n_pages), *([cache_vt] * n_pages))


def _tail_kernel(*refs, n_mix, nb, s, ff_chunk):
    h_ref = refs[0]
    mix_refs = refs[1:1 + n_mix]
    wmix_refs = refs[1 + n_mix:1 + 2 * n_mix]
    g_ref, wcq_ref, mk_ref, mv_ref, wco_ref, wup_ref, wdn_ref, o_ref, ca_scr = refs[1 + 2 * n_mix:]

    mix = _mm(mix_refs[0][...], wmix_refs[0][...])
    for mr, wr in zip(mix_refs[1:], wmix_refs[1:]):
        mix = mix + _mm(mr[...], wr[...])
    h = h_ref[...] + _rms(mix, g_ref[1:2, :])

    q = _mm(_rms(h, g_ref[2:3, :]), wcq_ref[...])
    rows = max(s, SUBLANES)
    mem = mk_ref.shape[1] // MEM_HEADS
    for bi in range(nb):
        qb = q[bi * s:(bi + 1) * s, :]
        if s < rows:
            qb = jnp.broadcast_to(qb[0:1, :], (rows, qb.shape[1]))
        for hh in range(MEM_HEADS):
            sl = slice(hh * MEM_HD, (hh + 1) * MEM_HD)
            tokens = pl.ds(hh, mem, stride=MEM_HEADS)
            sc = _mm_nt(qb[:, sl], mk_ref[bi, tokens, :]) * (MEM_HD ** -0.5)
            p = jnp.exp(sc - jnp.max(sc, axis=1, keepdims=True))
            oh = _mm(p, mv_ref[bi, tokens, :]) / jnp.sum(p, axis=1, keepdims=True)
            ca_scr[bi * s:(bi + 1) * s, sl] = oh[0:s, :]
    h = h + _rms(_mm(ca_scr[...], wco_ref[...]), g_ref[3:4, :])

    hn = _rms(h, g_ref[4:5, :]).astype(BF16)
    d_ff = wup_ref.shape[1]
    acc = jnp.zeros(h.shape, F32)
    for c0 in range(0, d_ff, ff_chunk):
        up = jnp.dot(hn, wup_ref[:, c0:c0 + ff_chunk], preferred_element_type=F32)
        up = jnp.square(jnp.maximum(up, 0.0))
        acc = acc + jnp.dot(up.astype(BF16), wdn_ref[c0:c0 + ff_chunk, :], preferred_element_type=F32)
    o_ref[...] = h + _rms(acc, g_ref[5:6, :])


def _tail(h, mixes, w_mixes, gains, w_cq, mk, mv, w_co, w_up, w_down, tm, seq_len, layer=0):
    m, d = h.shape
    s = min(seq_len, tm)
    nb = tm // s
    n_mix = len(mixes)
    mem_rows = mk.shape[-2]
    mem_dim = MEM_HEADS * MEM_HD
    row = lambda i: (i, 0)
    tiles_per_seq = max(seq_len // tm, 1)
    if mk.ndim == 4:
        mem_spec = pl.BlockSpec((None, nb, mem_rows, MEM_HD), lambda i: (layer, i // tiles_per_seq, 0, 0))
    else:
        mem_spec = pl.BlockSpec((nb, mem_rows, MEM_HD), lambda i: (i // tiles_per_seq, 0, 0))
    in_specs = [pl.BlockSpec((tm, d), row)]
    in_specs += [pl.BlockSpec((tm, x.shape[1]), row) for x in mixes]
    in_specs += [_const_spec(w.shape) for w in w_mixes]
    in_specs += [_const_spec(gains.shape), _const_spec(w_cq.shape),
                 mem_spec, mem_spec,
                 _const_spec(w_co.shape), _const_spec(w_up.shape), _const_spec(w_down.shape)]
    return pl.pallas_call(
        functools.partial(_tail_kernel, n_mix=n_mix, nb=nb, s=s, ff_chunk=min(1024, w_up.shape[1])),
        out_shape=jax.ShapeDtypeStruct((m, d), F32),
        grid=(m // tm,),
        in_specs=in_specs,
        out_specs=pl.BlockSpec((tm, d), row),
        scratch_shapes=[pltpu.VMEM((tm, mem_dim), F32)],
        compiler_params=_cparams(("parallel",)),
        name="tail",
    )(h, *mixes, *w_mixes, gains, w_cq, mk, mv, w_co, w_up, w_down)


def _front1_kernel(h_ref, hprev_ref, xlast_ref, g_ref, mu_ref, wr_ref, wk_ref, wv_ref, w0_ref, w1_ref, w2_ref,
                   a0_ref, a1_ref, a2_ref, g1_ref, g2_ref, kk_ref, ka_ref, seg_ref,
                   r_out, ld_out, k_out, v_out, kk_out, a_out, gt_out, sh_out, *, seq, tiles_per_seq):
    i = pl.program_id(0)
    g = g_ref[...]
    hn = _rms(h_ref[...], g)
    if seq:
        prev_row = _rms(hprev_ref[SUBLANES - 1:SUBLANES, :], g)
        prev_row = jnp.where(i % tiles_per_seq == 0, xlast_ref[0], prev_row)
        rowid = lax.broadcasted_iota(jnp.int32, hn.shape, 0)
        xprev = jnp.where(rowid == 0, prev_row, pltpu.roll(hn, 1, axis=0))
        sh_out[0] = hn[hn.shape[0] - 1:, :]
    else:
        xprev = xlast_ref[...]
        sh_out[...] = hn
    xx = xprev - hn
    mix = lambda n: hn + xx * mu_ref[n:n + 1, :]
    r = _mm(mix(0), wr_ref[...])
    k = _mm(mix(2), wk_ref[...])
    v = _mm(mix(3), wv_ref[...])
    wl = w0_ref[...] + _mm(jnp.tanh(_mm(mix(1), w1_ref[...])), w2_ref[...])
    z = -wl
    softplus = jnp.maximum(z, 0.0) + jnp.log(1.0 + jnp.exp(-jnp.abs(z)))
    ld_out[...] = -jnp.exp(-softplus - 0.5)
    a = _sigmoid(a0_ref[...] + _mm(_mm(mix(4), a1_ref[...]), a2_ref[...]))
    gt_out[...] = _mm(_sigmoid(_mm(mix(5), g1_ref[...])), g2_ref[...])
    kk = k * kk_ref[...]
    ss = _mm2(kk * kk, seg_ref[...])
    kk_out[...] = kk / jnp.maximum(jnp.sqrt(ss), 1e-12)
    k_out[...] = k * (1.0 + (a - 1.0) * ka_ref[...])
    r_out[...] = r
    v_out[...] = v
    a_out[...] = a


def _front1(h, x_last, g, p, tm, s):
    m, d = h.shape
    bsz = m // s
    seq = s > 1
    row = lambda i: (i, 0)
    big = jax.ShapeDtypeStruct((m, d), F32)
    if seq:
        tps = s // tm
        per = tm // SUBLANES
        hprev_spec = pl.BlockSpec((SUBLANES, d), lambda i: (jnp.maximum(i * per - 1, 0), 0))
        xlast = x_last.reshape(bsz, 1, d)
        xlast_spec = pl.BlockSpec((1, 1, d), lambda i: (i // tps, 0, 0))
        sh_shape = jax.ShapeDtypeStruct((bsz, 1, d), F32)
        sh_spec = pl.BlockSpec((1, 1, d), lambda i: (i // tps, 0, 0))
        sem = ("arbitrary",)
    else:
        tps = 1
        hprev_spec = pl.BlockSpec((SUBLANES, d), lambda i: (0, 0))
        xlast = x_last
        xlast_spec = pl.BlockSpec((tm, d), row)
        sh_shape = big
        sh_spec = pl.BlockSpec((tm, d), row)
        sem = ("parallel",)
    consts = [g, p["mu"], p["w_r"], p["w_k"], p["w_v"], p["w0"], p["w1"], p["w2"], p["a0"], p["a1"], p["a2"],
              p["g1"], p["g2"], p["k_k"], p["k_a"], p["seg"]]
    outs = pl.pallas_call(
        functools.partial(_front1_kernel, seq=seq, tiles_per_seq=tps),
        out_shape=(big,) * 7 + (sh_shape,),
        grid=(m // tm,),
        in_specs=[pl.BlockSpec((tm, d), row), hprev_spec, xlast_spec] + [_const_spec(c.shape) for c in consts],
        out_specs=(pl.BlockSpec((tm, d), row),) * 7 + (sh_spec,),
        compiler_params=_cparams(sem),
        name="front1",
    )(h, h, xlast, *consts)
    return outs[:7], outs[7].reshape(bsz, d)


def _scan_kernel(r_ref, ld_ref, k_ref, v_ref, kk_ref, a_ref, gt_ref, s0_ref, rk_ref, lg_ref, lb_ref,
                 o_ref, st_ref, s_scr, *, n_chunks, n_groups):
    c = pl.program_id(1)
    grp = SCAN_GROUP
    w = grp * RWKV_HD
    sbd = (lax.broadcasted_iota(jnp.int32, (w, w), 0) // RWKV_HD
           == lax.broadcasted_iota(jnp.int32, (w, w), 1) // RWKV_HD)

    @pl.when(c == 0)
    def _():
        for q in range(n_groups):
            s0 = s0_ref[0, q]
            s_scr[q] = jnp.where(sbd, jnp.concatenate([s0] * grp, axis=1), 0.0)

    _scan_chunk(r_ref[0], ld_ref[0], k_ref[0], v_ref[0], kk_ref[0], a_ref[0], gt_ref[0],
                rk_ref[...], lg_ref[...], lb_ref[...], o_ref, s_scr, n_groups)

    @pl.when(c == n_chunks - 1)
    def _():
        for q in range(n_groups):
            full = s_scr[q]
            acc = full[:, 0:RWKV_HD]
            for hh in range(1, grp):
                acc = acc + full[:, hh * RWKV_HD:(hh + 1) * RWKV_HD]
            st_ref[0, q] = acc


def _scan_chunk(r, ld, k, v, kk, a_gate, gt, r_k, lnx_g, lnx_b, o_ref, s_scr, n_groups):
    ch = SCAN_CHUNK
    grp = SCAN_GROUP
    w = grp * RWKV_HD
    rows = grp * ch
    gs = range(n_groups)
    rhead = lax.broadcasted_iota(jnp.int32, (rows, w), 0) // ch
    chead = lax.broadcasted_iota(jnp.int32, (rows, w), 1) // RWKV_HD
    bd = rhead == chead
    tile_r = lambda x: jnp.concatenate([x] * grp, axis=0)
    diag = lambda x: jnp.where(bd, tile_r(x), 0.0)
    part = lambda x, q: x[:, q * w:(q + 1) * w]

    b = kk * a_gate
    trow = lax.broadcasted_iota(jnp.int32, ld.shape, 0)
    cum = ld
    sh = 1
    while sh < ch:
        cum = cum + jnp.where(trow >= sh, pltpu.roll(cum, sh, axis=0), 0.0)
        sh *= 2
    clast = cum[ch - 1:ch, :]
    p_inv = jnp.exp(-cum)
    p_to = jnp.exp(clast - cum)
    rt = r * jnp.exp(cum)
    at = -kk * jnp.exp(cum - ld)
    bt = b * p_inv
    kt = k * p_inv
    bh = b * p_to
    kh = k * p_to
    p_end = jnp.exp(clast)

    tr = lax.broadcasted_iota(jnp.int32, (rows, rows), 0)
    tc = lax.broadcasted_iota(jnp.int32, (rows, rows), 1)
    same = (tr // ch) == (tc // ch)
    strict = same & (tc < tr)
    incl = same & (tc <= tr)
    eye = jnp.where(tr == tc, 1.0, 0.0)

    lhs = [jnp.concatenate([diag(part(at, q)), diag(part(rt, q))], axis=0).astype(BF16) for q in gs]
    rhs = [jnp.concatenate([tile_r(part(bt, q)), tile_r(part(kt, q))], axis=0) for q in gs]
    mall = [_mm_nt(lhs[q], rhs[q]) for q in gs]
    lp = [jnp.where(strict, mall[q][0:rows, 0:rows], 0.0) for q in gs]
    mak = [jnp.where(strict, mall[q][0:rows, rows:], 0.0) for q in gs]
    mrb = [jnp.where(incl, mall[q][rows:, 0:rows], 0.0) for q in gs]
    mrk = [jnp.where(incl, mall[q][rows:, rows:], 0.0) for q in gs]

    tinv = [eye + lp[q] for q in gs]
    n = 2
    while n < ch:
        lp = [_mm(lp[q], lp[q]) for q in gs]
        tinv = [tinv[q] + _mm(tinv[q], lp[q]) for q in gs]
        n *= 2

    vd = [diag(part(v, q)).astype(BF16) for q in gs]
    state = [s_scr[q] for q in gs]
    s0t = [_mm_nt(lhs[q], state[q]) for q in gs]
    wmat = [s0t[q][0:rows] + _mm(mak[q], vd[q]) for q in gs]
    u = [_mm(tinv[q], wmat[q]) for q in gs]
    ybd = [s0t[q][rows:] + _mm(mrb[q], u[q]) + _mm(mrk[q], vd[q]) for q in gs]
    upd = [_mm_tn(jnp.concatenate([u[q].astype(BF16), vd[q]], axis=0),
                  jnp.concatenate([diag(part(bh, q)), diag(part(kh, q))], axis=0)) for q in gs]
    for q in gs:
        s_scr[q] = state[q] * part(p_end, q) + upd[q]

    fa = lax.broadcasted_iota(jnp.int32, (w, w), 0) // RWKV_HD
    fb = lax.broadcasted_iota(jnp.int32, (w, w), 1) // RWKV_HD
    seg = jnp.where(fa == fb, 1.0, 0.0).astype(BF16)
    rkr = r * k * r_k
    for q in gs:
        y = ybd[q][0:ch]
        for hh in range(1, grp):
            y = y + ybd[q][hh * ch:(hh + 1) * ch]
        mean = _mm2(y, seg) * (1.0 / RWKV_HD)
        yc = y - mean
        var = _mm2(yc * yc, seg) * (1.0 / RWKV_HD)
        yn = yc * lax.rsqrt(var + GN_EPS) * part(lnx_g, q) + part(lnx_b, q)
        bonus = _mm2(part(rkr, q), seg) * part(v, q)
        o_ref[0, :, q * w:(q + 1) * w] = (yn + bonus) * part(gt, q)


def _scan(streams, gate, s0, r_k, lnx_g, lnx_b):
    bsz, t, d = gate.shape
    w = SCAN_GROUP * RWKV_HD
    nq = d // w
    nc = t // SCAN_CHUNK
    tok = pl.BlockSpec((1, SCAN_CHUNK, d), lambda b, c: (b, c, 0))
    st = pl.BlockSpec((1, nq, w, RWKV_HD), lambda b, c: (b, 0, 0, 0))
    s0g = s0.reshape(bsz, nq, w, RWKV_HD)
    out, s_t = pl.pallas_call(
        functools.partial(_scan_kernel, n_chunks=nc, n_groups=nq),
        out_shape=(jax.ShapeDtypeStruct((bsz, t, d), F32), jax.ShapeDtypeStruct(s0g.shape, F32)),
        grid=(bsz, nc),
        in_specs=[tok] * 7 + [st, _const_spec((1, d)), _const_spec((1, d)), _const_spec((1, d))],
        out_specs=(tok, st),
        scratch_shapes=[pltpu.VMEM((nq, w, w), F32)],
        compiler_params=_cparams(("parallel", "arbitrary")),
        name="rwkv_scan",
    )(*streams, gate, s0g, r_k, lnx_g, lnx_b)
    return out, s_t.reshape(s0.shape)


def _wkv_step_kernel(r_ref, ld_ref, k_ref, kk_ref, a_ref, v_ref, gt_ref, s_ref, rk_ref, lg_ref, lb_ref,
                     o_ref, st_ref):
    r, k, kk = r_ref[0], k_ref[0], kk_ref[0]
    v = v_ref[0]
    state = s_ref[0]
    sa = -jnp.sum(state * kk, axis=2, keepdims=True)
    state = state * jnp.exp(ld_ref[0]) + sa * (kk * a_ref[0]) + v * k
    st_ref[0] = state
    y = jnp.sum(state * r, axis=2, keepdims=True)
    mean = jnp.mean(y, axis=1, keepdims=True)
    yc = y - mean
    var = jnp.mean(yc * yc, axis=1, keepdims=True)
    yn = yc * lax.rsqrt(var + GN_EPS) * lg_ref[...] + lb_ref[...]
    bonus = jnp.sum(r * k * rk_ref[...], axis=2, keepdims=True) * v
    o_ref[0] = (yn + bonus) * gt_ref[0]


def _wkv_step(streams, gate, s0, r_k, lnx_g, lnx_b):
    r, ld, k, v, kk, a = streams
    bsz, d = r.shape
    n = RWKV_HD
    h = d // n
    row = lambda x: x.reshape(-1, h, 1, n)
    col = lambda x: x.reshape(-1, h, n, 1)
    row_spec = pl.BlockSpec((1, h, 1, n), lambda b: (b, 0, 0, 0))
    col_spec = pl.BlockSpec((1, h, n, 1), lambda b: (b, 0, 0, 0))
    st_spec = pl.BlockSpec((1, h, n, n), lambda b: (b, 0, 0, 0))
    out, s_t = pl.pallas_call(
        _wkv_step_kernel,
        out_shape=(jax.ShapeDtypeStruct((bsz, h, n, 1), F32), jax.ShapeDtypeStruct(s0.shape, F32)),
        grid=(bsz,),
        in_specs=[row_spec] * 5 + [col_spec] * 2 + [st_spec, _const_spec((h, 1, n)), _const_spec((h, n, 1)),
                                                   _const_spec((h, n, 1))],
        out_specs=(col_spec, st_spec),
        compiler_params=_cparams(("parallel",)),
        name="wkv_step",
    )(row(r), row(ld), row(k), row(kk), row(a), col(v), col(gate), s0,
      r_k.reshape(h, 1, n), lnx_g.reshape(h, n, 1), lnx_b.reshape(h, n, 1))
    return out.reshape(bsz, d), s_t


def _row_tile(m, cap):
    t = min(m, cap)
    assert m % t == 0
    return t


def _largest_divisor(n, cap):
    return max(t for t in range(1, cap + 1) if n % t == 0)


def _pad_cols(w, to):
    return jnp.pad(w, ((0, 0), (0, to - w.shape[1])))


def _pad_rows(w, to):
    return jnp.pad(w, ((0, to - w.shape[0]), (0, 0)))


def kernel(x_prompt, x_sample, mem_prompt, cache_k, cache_v, page_table, cache_mem_k, cache_mem_v, state_conv, state_shift, state_wkv, norm_gain, w_in, conv_w, conv_b, conv_ln_g, conv_ln_b, w_out, rwkv_mu, rwkv_w_r, rwkv_w_k, rwkv_w_v, rwkv_w_o, rwkv_w0, rwkv_w1, rwkv_w2, rwkv_a0, rwkv_a1, rwkv_a2, rwkv_g1, rwkv_g2, rwkv_k_k, rwkv_k_a, rwkv_r_k, rwkv_lnx_g, rwkv_lnx_b, w_cq, w_ck, w_cv, w_co, w_up, w_down):
    bp, sp, d = x_prompt.shape
    bs, ss, _ = x_sample.shape
    assert ss == 1
    depth = norm_gain.shape[0]
    conv_dim = conv_w.shape[2]
    att_dim = (w_in.shape[2] - 2 * conv_dim) // 3
    mem = mem_prompt.shape[1]
    mem_dim = w_ck.shape[2]
    n_pages, page = page_table.shape[1], cache_k.shape[2]
    past_len = n_pages * page
    pages_per_block = MOBA_BLOCK // page
    att_heads = att_dim // ATT_HD
    nb_past = past_len // MOBA_BLOCK
    assert past_len % MOBA_BLOCK == 0 and nb_past >= MOBA_TOPK and sp % MOBA_BLOCK == 0
    heads = d // RWKV_HD

    tm_p = _row_tile(sp, 512)
    tm_s = bs
    bf = lambda x: x.astype(BF16)

    w_mem = bf(jnp.concatenate([w for l in range(depth) for w in (w_ck[l], w_cv[l])], axis=1))
    memkv = _proj(mem_prompt.reshape(bp * mem, d), w_mem, _row_tile(bp * mem, 512))
    memkv = memkv.reshape(bp, mem, depth, 2, mem_dim)
    interleave = lambda x: x.reshape(x.shape[0], mem * MEM_HEADS, MEM_HD)
    mem_k_p = [interleave(memkv[:, :, l, 0]) for l in range(depth)]
    mem_v_p = [interleave(memkv[:, :, l, 1]) for l in range(depth)]

    cos_p, sin_p = _rope_tables(jnp.arange(sp, dtype=jnp.int32))
    cos_s, sin_s = _rope_tables(jnp.full((tm_s,), past_len, dtype=jnp.int32))

    hp = x_prompt.reshape(bp * sp, d)
    hs = x_sample.reshape(bs, d)
    k_p, v_p, k_s, v_s, conv_p, conv_s, shift_p, shift_s, wkv_p, wkv_s = ([] for _ in range(10))

    for l in range(depth):
        gains = norm_gain[l]
        g0 = gains[0:1]
        tail_w = (gains, bf(w_cq[l]))
        tail_w2 = (bf(w_co[l]), bf(w_up[l]), bf(w_down[l]))
        mk_s = cache_mem_k.reshape(depth, bs, mem * MEM_HEADS, MEM_HD)
        mv_s = cache_mem_v.reshape(depth, bs, mem * MEM_HEADS, MEM_HD)
        if l % 2 == 0:
            e = l // 2
            w_in_b = bf(w_in[e])
            w_o = bf(w_out[e])
            cb, lg, lb = conv_b[e][None], conv_ln_g[e][None], conv_ln_b[e][None]

            u, q, k, v = _front0(hp, g0, w_in_b, cos_p, sin_p, conv_dim, att_dim, tm_p)
            u3 = u.reshape(bp, sp, conv_dim)
            c = _conv_prompt(u3, conv_w[e], cb, lg, lb, _row_tile(sp, 256))
            att = _moba_prompt(q.reshape(bp, sp, att_dim), k.reshape(bp, sp, att_dim), v.reshape(bp, sp, att_dim))
            hp = _tail(hp, [c.reshape(bp * sp, conv_dim), att.reshape(bp * sp, att_dim)],
                       [w_o[:conv_dim], w_o[conv_dim:]], *tail_w, mem_k_p[l], mem_v_p[l], *tail_w2, tm_p, sp)
            k_p.append(k.reshape(bp, sp, att_dim // ATT_HD, ATT_HD))
            v_p.append(v.reshape(bp, sp, att_dim // ATT_HD, ATT_HD))
            conv_p.append(u3[:, sp - (CONV_WIDTH - 1):])

            u, q, k, v = _front0(hs, g0, w_in_b, cos_s, sin_s, conv_dim, att_dim, tm_s)
            c = _conv_step(jnp.swapaxes(state_conv[e], 0, 1), u, conv_w[e], cb, lg, lb)
            ckt = jnp.transpose(cache_k[e], (0, 2, 3, 1))
            cvt = jnp.transpose(cache_v[e], (0, 2, 3, 1))
            col = lambda x: x.reshape(bs, att_heads, ATT_HD, 1)
            pages_per_step = pages_per_block * _largest_divisor(nb_past, GATE_BLOCKS_PER_STEP)
            sel = _moba_gate(page_table, col(q), ckt, nb_past, pages_per_block, pages_per_step)
            att = _moba_decode(page_table, sel[:, :, :MOBA_TOPK], col(q), col(k), col(v), ckt, cvt,
                               pages_per_block).reshape(bs, att_dim)
            hs = _tail(hs, [c, att], [w_o[:conv_dim], w_o[conv_dim:]], *tail_w, mk_s, mv_s, *tail_w2, tm_s, 1, l)
            k_s.append(k.reshape(bs, 1, att_dim // ATT_HD, ATT_HD))
            v_s.append(v.reshape(bs, 1, att_dim // ATT_HD, ATT_HD))
            conv_s.append(jnp.concatenate([state_conv[e][:, 1:], u[:, None]], axis=1))
        else:
            o = l // 2
            lora = LANES
            gl = MXU_DIM
            feat = jnp.arange(d) // RWKV_HD
            prm = dict(
                mu=rwkv_mu[o], w_r=bf(rwkv_w_r[o]), w_k=bf(rwkv_w_k[o]), w_v=bf(rwkv_w_v[o]),
                w0=rwkv_w0[o][None], w1=bf(_pad_cols(rwkv_w1[o], lora)), w2=bf(_pad_rows(rwkv_w2[o], lora)),
                a0=rwkv_a0[o][None], a1=bf(_pad_cols(rwkv_a1[o], lora)), a2=bf(_pad_rows(rwkv_a2[o], lora)),
                g1=bf(_pad_cols(rwkv_g1[o], gl)), g2=bf(_pad_rows(rwkv_g2[o], gl)),
                k_k=rwkv_k_k[o][None], k_a=rwkv_k_a[o][None],
                seg=(feat[:, None] == feat[None, :]).astype(BF16))
            r_k = rwkv_r_k[o].reshape(1, d)
            lnx_g, lnx_b = rwkv_lnx_g[o][None], rwkv_lnx_b[o][None]
            w_o = bf(rwkv_w_o[o])

            streams, shift = _front1(hp, jnp.zeros((bp, d), F32), g0, prm, tm_p, sp)
            to3 = lambda x: x.reshape(bp, sp, d)
            gated, s_t = _scan([to3(x) for x in streams[:6]], to3(streams[6]),
                               jnp.zeros((bp, heads, RWKV_HD, RWKV_HD), F32), r_k, lnx_g, lnx_b)
            hp = _tail(hp, [gated.reshape(bp * sp, d)], [w_o], *tail_w, mem_k_p[l], mem_v_p[l], *tail_w2, tm_p, sp)
            shift_p.append(shift)
            wkv_p.append(s_t)

            streams, shift = _front1(hs, state_shift[o], g0, prm, tm_s, 1)
            gated, s_t = _wkv_step(streams[:6], streams[6], state_wkv[o], r_k, lnx_g, lnx_b)
            hs = _tail(hs, [gated], [w_o], *tail_w, mk_s, mv_s, *tail_w2, tm_s, 1, l)
            shift_s.append(shift)
            wkv_s.append(s_t)

    mem_shape = (bp, mem, MEM_HEADS, mem_dim // MEM_HEADS)
    return (hp.reshape(bp, sp, d), hs.reshape(bs, 1, d),
            jnp.stack(k_p), jnp.stack(v_p), jnp.stack(k_s), jnp.stack(v_s),
            jnp.stack(conv_p), jnp.stack(conv_s), jnp.stack(shift_p), jnp.stack(shift_s),
            jnp.stack(wkv_p), jnp.stack(wkv_s),
            jnp.stack([x.reshape(mem_shape) for x in mem_k_p]), jnp.stack([x.reshape(mem_shape) for x in mem_v_p]))
```

```python
import functools

import jax
import jax.numpy as jnp
from jax import lax
from jax.experimental import pallas as pl
from jax.experimental.pallas import tpu as pltpu

F32 = jnp.float32
BF16 = jnp.bfloat16

NORM_EPS = 1e-6
LN_EPS = 1e-5
GN_EPS = 64e-5
NEG_INF = -1e30
ATT_HD = 64
MEM_HEADS = 4
MEM_HD = 128
RWKV_HD = 64
MOBA_BLOCK = 256
MOBA_TOPK = 3
CONV_WIDTH = 31
ROPE_THETA = 10000.0

LANES = 128
SUBLANES = 8
MXU_DIM = 256
VMEM_LIMIT_BYTES = 56 * 1024 * 1024
SCAN_CHUNK = 64
SCAN_GROUP = MXU_DIM // RWKV_HD
HALO = 32
LOG2_E = 1.4426950408889634
SCAN_SEQS_PER_STEP = 1
DECODE_HEADS_PER_STEP = 4
GATE_BLOCKS_PER_STEP = 16


def _cparams(sem):
    return pltpu.CompilerParams(dimension_semantics=sem, vmem_limit_bytes=VMEM_LIMIT_BYTES)


def _const_spec(shape):
    nd = len(shape)
    return pl.BlockSpec(shape, lambda *_: (0,) * nd, pipeline_mode=pl.Buffered(1))


def _rms(x, g):
    return x * lax.rsqrt(jnp.mean(x * x, axis=-1, keepdims=True) + NORM_EPS) * g


def _sigmoid(x):
    return 1.0 / (1.0 + jnp.exp(-x))


def _mm(a, b):
    return jnp.dot(a.astype(BF16), b.astype(BF16), preferred_element_type=F32)


def _mm_nt(a, b):
    return lax.dot_general(a.astype(BF16), b.astype(BF16), (((1,), (1,)), ((), ())),
                           preferred_element_type=F32)


def _mm_tn(a, b):
    return lax.dot_general(a.astype(BF16), b.astype(BF16), (((0,), (0,)), ((), ())),
                           preferred_element_type=F32)


def _split(x):
    hi = x.astype(BF16)
    lo = (x - hi.astype(F32)).astype(BF16)
    return hi, lo


def _mm2(a, b):
    hi, lo = _split(a)
    return (jnp.dot(hi, b, preferred_element_type=F32)
            + jnp.dot(lo, b, preferred_element_type=F32))


def _proj_kernel(x_ref, w_ref, o_ref):
    o_ref[...] = _mm(x_ref[...], w_ref[...])


def _proj(x, w, tm):
    m, kdim = x.shape
    n = w.shape[1]
    return pl.pallas_call(
        _proj_kernel,
        out_shape=jax.ShapeDtypeStruct((m, n), F32),
        grid=(m // tm,),
        in_specs=[pl.BlockSpec((tm, kdim), lambda i: (i, 0)), _const_spec((kdim, n))],
        out_specs=pl.BlockSpec((tm, n), lambda i: (i, 0)),
        compiler_params=_cparams(("parallel",)),
        name="proj",
    )(x, w)


def _rope(x, cos, sin):
    width = x.shape[1]
    lane = lax.broadcasted_iota(jnp.int32, x.shape, 1)
    first_half = (lane % ATT_HD) < (ATT_HD // 2)
    partner = jnp.where(first_half,
                        pltpu.roll(x, width - ATT_HD // 2, axis=1),
                        pltpu.roll(x, ATT_HD // 2, axis=1))
    return x * cos + partner * sin


def _front0_kernel(x_ref, g_ref, w_ref, cos_ref, sin_ref, u_ref, q_ref, k_ref, v_ref, v_scr, *, conv_dim,
                   att_dim, kv_transposed):
    hn = _rms(x_ref[...], g_ref[...]).astype(BF16)
    c0 = conv_dim
    val = jnp.dot(hn, w_ref[:, 0:c0], preferred_element_type=F32)
    gate = jnp.dot(hn, w_ref[:, c0:2 * c0], preferred_element_type=F32)
    u_ref[...] = val * _sigmoid(gate)
    reps = att_dim // LANES
    cos = jnp.concatenate([cos_ref[...]] * reps, axis=1)
    sin = jnp.concatenate([sin_ref[...]] * reps, axis=1)
    o = 2 * c0
    q = jnp.dot(hn, w_ref[:, o:o + att_dim], preferred_element_type=F32)
    q_ref[...] = _rope(q, cos, sin)
    k = _rope(jnp.dot(hn, w_ref[:, o + att_dim:o + 2 * att_dim], preferred_element_type=F32), cos, sin)
    v = jnp.dot(hn, w_ref[:, o + 2 * att_dim:o + 3 * att_dim], preferred_element_type=F32)
    if kv_transposed:
        k_ref[0] = k.T
        v_scr[...] = v
        v_ref[0] = v_scr[...].T
    else:
        k_ref[...] = k
        v_ref[...] = v


def _front0(h, g, w_in, cos_tab, sin_tab, conv_dim, att_dim, tm, seq_len):
    m, d = h.shape
    n_tab = cos_tab.shape[0] // tm
    row = lambda i: (i, 0)
    tab = lambda i: (i % n_tab, 0)
    kv_transposed = seq_len > 1
    if kv_transposed:
        tps = seq_len // tm
        kv_shape = jax.ShapeDtypeStruct((m // seq_len, att_dim, seq_len), F32)
        kv_spec = pl.BlockSpec((1, att_dim, tm), lambda i: (i // tps, 0, i % tps))
    else:
        kv_shape = jax.ShapeDtypeStruct((m, att_dim), F32)
        kv_spec = pl.BlockSpec((tm, att_dim), row)
    return pl.pallas_call(
        functools.partial(_front0_kernel, conv_dim=conv_dim, att_dim=att_dim, kv_transposed=kv_transposed),
        out_shape=(jax.ShapeDtypeStruct((m, conv_dim), F32), jax.ShapeDtypeStruct((m, att_dim), F32),
                   kv_shape, kv_shape),
        grid=(m // tm,),
        in_specs=[pl.BlockSpec((tm, d), row), _const_spec((1, d)), _const_spec(w_in.shape),
                  pl.BlockSpec((tm, LANES), tab), pl.BlockSpec((tm, LANES), tab)],
        out_specs=(pl.BlockSpec((tm, conv_dim), row), pl.BlockSpec((tm, att_dim), row), kv_spec, kv_spec),
        scratch_shapes=[pltpu.VMEM((tm, att_dim), F32)],
        compiler_params=_cparams(("parallel",)),
        name="front0",
    )(h, g, w_in, cos_tab, sin_tab)


def _rope_tables(pos):
    half = ATT_HD // 2
    inv_freq = ROPE_THETA ** (-jnp.arange(half, dtype=F32) / half)
    ang = pos.astype(F32)[:, None] * inv_freq[None, :]
    cos, sin = jnp.cos(ang), jnp.sin(ang)
    reps = LANES // ATT_HD
    return (jnp.tile(jnp.concatenate([cos, cos], axis=1), (1, reps)),
            jnp.tile(jnp.concatenate([-sin, sin], axis=1), (1, reps)))


def _ln_silu(y, g, b):
    mu = jnp.mean(y, axis=-1, keepdims=True)
    yc = y - mu
    var = jnp.mean(yc * yc, axis=-1, keepdims=True)
    ln = yc * lax.rsqrt(var + LN_EPS) * g + b
    return ln * _sigmoid(ln)


def _conv_kernel(u_ref, uprev_ref, w_ref, b_ref, lg_ref, lb_ref, c_ref, full_scr, *, tc, sub):
    i = pl.program_id(1)
    full_scr[0:HALO, :] = jnp.where(i > 0, uprev_ref[0], 0.0)
    full_scr[HALO:HALO + tc, :] = u_ref[0]
    first = HALO - (CONV_WIDTH - 1)
    for r0 in range(0, tc, sub):
        acc = jnp.broadcast_to(b_ref[...], (sub, b_ref.shape[1]))
        for res in range(SUBLANES):
            rows = sub + (SUBLANES if res else 0)
            z = None
            for j in range(CONV_WIDTH):
                if (first + j) % SUBLANES != res:
                    continue
                base = r0 + first + j - res
                term = w_ref[j:j + 1, :] * full_scr[base:base + rows, :]
                z = term if z is None else z + term
            acc = acc + z[res:res + sub, :]
        c_ref[0, r0:r0 + sub, :] = _ln_silu(acc, lg_ref[...], lb_ref[...])


def _conv_prompt(u, w, b, lg, lb, tc):
    bsz, s, c = u.shape
    per = tc // HALO
    wp = jnp.pad(w, ((0, HALO - CONV_WIDTH), (0, 0)))
    return pl.pallas_call(
        functools.partial(_conv_kernel, tc=tc, sub=HALO),
        out_shape=jax.ShapeDtypeStruct((bsz, s, c), F32),
        grid=(bsz, s // tc),
        in_specs=[pl.BlockSpec((1, tc, c), lambda bi, i: (bi, i, 0)),
                  pl.BlockSpec((1, HALO, c), lambda bi, i: (bi, jnp.maximum(i * per - 1, 0), 0)),
                  _const_spec((HALO, c)), _const_spec((1, c)), _const_spec((1, c)), _const_spec((1, c))],
        out_specs=pl.BlockSpec((1, tc, c), lambda bi, i: (bi, i, 0)),
        scratch_shapes=[pltpu.VMEM((HALO + tc, c), F32)],
        compiler_params=_cparams(("parallel", "parallel")),
        name="conv_prompt",
    )(u, u, wp, b, lg, lb)


def _conv_step_kernel(st_ref, u_ref, w_ref, b_ref, lg_ref, lb_ref, c_ref):
    acc = b_ref[...] + w_ref[CONV_WIDTH - 1:CONV_WIDTH, :] * u_ref[...]
    for j in range(CONV_WIDTH - 1):
        acc = acc + w_ref[j:j + 1, :] * st_ref[j]
    c_ref[...] = _ln_silu(acc, lg_ref[...], lb_ref[...])


def _conv_step(state_t, u, w, b, lg, lb):
    bsz, c = u.shape
    wp = jnp.pad(w, ((0, HALO - CONV_WIDTH), (0, 0)))
    return pl.pallas_call(
        _conv_step_kernel,
        out_shape=jax.ShapeDtypeStruct((bsz, c), F32),
        grid=(1,),
        in_specs=[_const_spec(state_t.shape), _const_spec((bsz, c)), _const_spec((HALO, c)),
                  _const_spec((1, c)), _const_spec((1, c)), _const_spec((1, c))],
        out_specs=pl.BlockSpec((bsz, c), lambda i: (0, 0)),
        compiler_params=_cparams(("arbitrary",)),
        name="conv_step",
    )(state_t, u, wp, b, lg, lb)


def _top_blocks(gate, n_past):
    blk = lax.broadcasted_iota(jnp.int32, gate.shape, 0).astype(F32)
    g = jnp.where(blk < n_past.astype(F32), gate, NEG_INF)
    picks = []
    for r in range(MOBA_TOPK):
        m = jnp.max(g, axis=0, keepdims=True)
        idx = jnp.min(jnp.where(g == m, blk, 1e9), axis=0, keepdims=True)
        picks.append(jnp.where(r < n_past, idx, -1.0))
        g = jnp.where(blk == idx, -jnp.inf, g)
    return picks


def _moba_kernel(q_ref, k_ref, v_ref, o_ref, kmean_scr, kb_scr, vt_scr, s_scr, *, nb, scale):
    i = pl.program_id(2)
    blk = MOBA_BLOCK
    hpp = LANES // ATT_HD

    @pl.when(i == 0)
    def _():
        kmean_scr[...] = jnp.zeros_like(kmean_scr)
        vt_scr[...] = v_ref[0].astype(BF16)
        for j in range(nb):
            rows = slice(j * blk, (j + 1) * blk)
            kj = k_ref[0, :, rows].T
            kmean_scr[j:j + 1, :] = jnp.mean(kj, axis=0, keepdims=True)
            kb_scr[rows, :] = kj.astype(BF16)

    q = q_ref[0] * (scale * LOG2_E)
    lane = lax.broadcasted_iota(jnp.int32, q.shape, 1)
    q2 = jnp.concatenate([jnp.where((lane >= hh * ATT_HD) & (lane < (hh + 1) * ATT_HD), q, 0.0)
                          for hh in range(hpp)], axis=0)
    q_hi, q_lo = _split(q2)
    km_hi, km_lo = _split(kmean_scr[...])
    nt = (((1,), (1,)), ((), ()))
    gate = (lax.dot_general(km_hi, q_hi, nt, preferred_element_type=F32)
            + lax.dot_general(km_lo, q_hi, nt, preferred_element_type=F32)
            + lax.dot_general(km_hi, q_lo, nt, preferred_element_type=F32))
    picks = _top_blocks(gate, i)

    ones = jnp.ones((SUBLANES, blk), BF16)

    def pv(off, p):
        vt_j = vt_scr[:, pl.ds(off, blk)]
        upd = [jnp.dot(vt_j[hh * ATT_HD:(hh + 1) * ATT_HD, :], p[:, hh * blk:(hh + 1) * blk],
                       preferred_element_type=F32) for hh in range(hpp)]
        return upd, jnp.dot(ones, p, preferred_element_type=F32)[0:1, :]

    def scores(off):
        return lax.dot_general(kb_scr[pl.ds(off, blk), :], q_hi, nt, preferred_element_type=F32)

    def offset(j):
        return pl.multiple_of(jnp.minimum(j, nb - 1) * blk, blk)

    def past_scores(j):
        jf = j.astype(F32)
        chosen = (picks[0] == jf) | (picks[1] == jf) | (picks[2] == jf)
        return jnp.where(chosen, scores(offset(j)), NEG_INF)

    start = offset(i)
    s = scores(start)
    key = lax.broadcasted_iota(jnp.int32, s.shape, 0)
    qry = lax.broadcasted_iota(jnp.int32, s.shape, 1) % blk
    s = jnp.where(key <= qry, s, NEG_INF)
    s_scr[0] = past_scores(jnp.int32(0))
    m0 = jnp.max(s, axis=0, keepdims=True)
    acc0, l0 = pv(start, jnp.exp2(s - m0).astype(BF16))

    def fold(j, slot, carry):
        m, l, acc = carry
        sj = s_scr[slot]
        m_new = jnp.maximum(m, jnp.max(sj, axis=0, keepdims=True))
        alpha = jnp.exp2(m - m_new)
        upd, psum = pv(offset(j), jnp.exp2(sj - m_new).astype(BF16))
        l = alpha * l + psum
        acc = [alpha[:, hh * blk:(hh + 1) * blk] * acc[hh] + upd[hh] for hh in range(hpp)]
        return m_new, l, acc

    def body(jj, carry):
        j = 2 * jj
        s_scr[1] = past_scores(j + 1)
        carry = fold(j, 0, carry)
        s_scr[0] = past_scores(j + 2)
        return fold(j + 1, 1, carry)

    _, l, acc = lax.fori_loop(0, (i + 1) // 2, body, (m0, l0, acc0))
    out_t = jnp.concatenate([acc[hh] / l[:, hh * blk:(hh + 1) * blk] for hh in range(hpp)], axis=0)
    o_ref[0] = out_t.T


def _moba_prompt(q, kt, vt):
    bsz, s, a = q.shape
    nb = s // MOBA_BLOCK
    blk = pl.BlockSpec((1, MOBA_BLOCK, LANES), lambda b, hp, i: (b, i, hp))
    seq = pl.BlockSpec((1, LANES, s), lambda b, hp, i: (b, hp, 0))
    nb_pad = -(-nb // SUBLANES) * SUBLANES
    return pl.pallas_call(
        functools.partial(_moba_kernel, nb=nb, scale=ATT_HD ** -0.5),
        out_shape=jax.ShapeDtypeStruct((bsz, s, a), F32),
        grid=(bsz, a // LANES, nb),
        in_specs=[blk, seq, seq],
        out_specs=blk,
        scratch_shapes=[pltpu.VMEM((nb_pad, LANES), F32), pltpu.VMEM((s, LANES), BF16),
                        pltpu.VMEM((LANES, s), BF16),
                        pltpu.VMEM((2, MOBA_BLOCK, (LANES // ATT_HD) * MOBA_BLOCK), F32)],
        compiler_params=_cparams(("parallel", "parallel", "arbitrary")),
        name="moba_prompt",
    )(q, kt, vt)


def _moba_gate_kernel(pt_ref, q_ref, *refs, nb, pages_per_step, pages_per_block):
    del pt_ref
    page_refs = refs[:pages_per_step]
    sel_ref, qb_scr, g_scr = refs[pages_per_step:]
    j = pl.program_id(1)
    blocks_per_step = pages_per_step // pages_per_block

    @pl.when(j == 0)
    def _():
        qb_scr[...] = jnp.broadcast_to(q_ref[0], qb_scr.shape)

    qb = qb_scr[...]
    for bl in range(blocks_per_step):
        ksum = page_refs[bl * pages_per_block][0]
        for pg in range(1, pages_per_block):
            ksum = ksum + page_refs[bl * pages_per_block + pg][0]
        g_scr[j * blocks_per_step + bl] = jnp.sum(ksum * qb, axis=1)

    @pl.when(j == nb // blocks_per_step - 1)
    def _():
        gate = jnp.sum(g_scr[...], axis=2, keepdims=True) * (1.0 / MOBA_BLOCK)
        blk = lax.broadcasted_iota(jnp.int32, gate.shape, 0).astype(F32)
        lane = lax.broadcasted_iota(jnp.int32, sel_ref.shape[1:], 1)
        sel = jnp.zeros(sel_ref.shape[1:], F32)
        for r in range(MOBA_TOPK):
            m = jnp.max(gate, axis=0, keepdims=True)
            idx = jnp.min(jnp.where(gate == m, blk, 1e9), axis=0, keepdims=True)
            sel = jnp.where(lane == r, idx[0], sel)
            gate = jnp.where(blk == idx, -jnp.inf, gate)
        sel_ref[0] = sel.astype(jnp.int32)


def _moba_gate(page_table, q_col, cache_kt, nb, pages_per_block, pages_per_step):
    bsz, heads, hd, _ = q_col.shape
    page = cache_kt.shape[3]
    steps = nb * pages_per_block // pages_per_step

    def page_spec(w):
        return pl.BlockSpec((1, heads, hd, page), lambda b, j, pt: (pt[b, j * pages_per_step + w], 0, 0, 0))

    grid_spec = pltpu.PrefetchScalarGridSpec(
        num_scalar_prefetch=1,
        grid=(bsz, steps),
        in_specs=[pl.BlockSpec((1, heads, hd, 1), lambda b, j, pt: (b, 0, 0, 0))]
        + [page_spec(w) for w in range(pages_per_step)],
        out_specs=pl.BlockSpec((1, heads, LANES), lambda b, j, pt: (b, 0, 0)),
        scratch_shapes=[pltpu.VMEM((heads, hd, page), F32), pltpu.VMEM((nb, heads, page), F32)],
    )
    return pl.pallas_call(
        functools.partial(_moba_gate_kernel, nb=nb, pages_per_step=pages_per_step, pages_per_block=pages_per_block),
        out_shape=jax.ShapeDtypeStruct((bsz, heads, LANES), jnp.int32),
        grid_spec=grid_spec,
        compiler_params=_cparams(("parallel", "arbitrary")),
        name="moba_gate",
    )(page_table, q_col, *([cache_kt] * pages_per_step))


def _moba_decode_kernel(pt_ref, sel_ref, q_ref, kn_ref, vn_ref, *refs, n_pages, heads_per_step, scale):
    del pt_ref, sel_ref
    o_ref = refs[2 * n_pages * heads_per_step]
    for u in range(heads_per_step):
        k_refs = refs[u * n_pages:(u + 1) * n_pages]
        v_refs = refs[(heads_per_step + u) * n_pages:(heads_per_step + u + 1) * n_pages]
        q = q_ref[0, u]
        s_own = jnp.sum(q * kn_ref[0, u], axis=0, keepdims=True) * scale
        s = [jnp.sum(kr[0, 0] * q, axis=0, keepdims=True) * scale for kr in k_refs]
        m = s_own
        for sp in s:
            m = jnp.maximum(m, jnp.max(sp, axis=1, keepdims=True))
        p_own = jnp.exp(s_own - m)
        l = p_own
        acc = p_own * vn_ref[0, u]
        for sp, vr in zip(s, v_refs):
            p = jnp.exp(sp - m)
            l = l + jnp.sum(p, axis=1, keepdims=True)
            acc = acc + jnp.sum(vr[0, 0] * p, axis=1, keepdims=True)
        o_ref[0, u] = acc / l


def _moba_decode(page_table, sel, q_col, kn_col, vn_col, cache_kt, cache_vt, pages_per_block):
    bsz, heads, hd, _ = q_col.shape
    page = cache_kt.shape[3]
    n_pages = MOBA_TOPK * pages_per_block
    hps = _largest_divisor(heads, DECODE_HEADS_PER_STEP)
    vec = pl.BlockSpec((1, hps, hd, 1), lambda b, hg, pt, sl: (b, hg, 0, 0))

    def page_spec(u, r, w):
        return pl.BlockSpec(
            (1, 1, hd, page),
            lambda b, hg, pt, sl: (pt[b, pages_per_block * sl[b, hg * hps + u, r] + w], hg * hps + u, 0, 0))

    pages = [page_spec(u, r, w) for u in range(hps) for r in range(MOBA_TOPK) for w in range(pages_per_block)]
    grid_spec = pltpu.PrefetchScalarGridSpec(
        num_scalar_prefetch=2,
        grid=(bsz, heads // hps),
        in_specs=[vec, vec, vec] + pages + pages,
        out_specs=vec,
    )
    return pl.pallas_call(
        functools.partial(_moba_decode_kernel, n_pages=n_pages, heads_per_step=hps, scale=ATT_HD ** -0.5),
        out_shape=jax.ShapeDtypeStruct((bsz, heads, hd, 1), F32),
        grid_spec=grid_spec,
        compiler_params=_cparams(("parallel", "parallel")),
        name="moba_decode",
    )(page_table, sel, q_col, kn_col, vn_col, *([cache_kt] * len(pages)), *([cache_vt] * len(pages)))


def _tail_kernel(*refs, n_mix, nb, s, ff_chunk):
    h_ref = refs[0]
    mix_refs = refs[1:1 + n_mix]
    wmix_refs = refs[1 + n_mix:1 + 2 * n_mix]
    g_ref, wcq_ref, mk_ref, mv_ref, wco_ref, wup_ref, wdn_ref, o_ref, ca_scr = refs[1 + 2 * n_mix:]

    mix = _mm(mix_refs[0][...], wmix_refs[0][...])
    for mr, wr in zip(mix_refs[1:], wmix_refs[1:]):
        mix = mix + _mm(mr[...], wr[...])
    h = h_ref[...] + _rms(mix, g_ref[1:2, :])

    q = _mm(_rms(h, g_ref[2:3, :]), wcq_ref[...])
    rows = max(s, SUBLANES)
    mem = mk_ref.shape[1] // MEM_HEADS
    for bi in range(nb):
        qb = q[bi * s:(bi + 1) * s, :]
        if s < rows:
            qb = jnp.broadcast_to(qb[0:1, :], (rows, qb.shape[1]))
        for hh in range(MEM_HEADS):
            sl = slice(hh * MEM_HD, (hh + 1) * MEM_HD)
            tokens = pl.ds(hh, mem, stride=MEM_HEADS)
            sc = _mm_nt(qb[:, sl], mk_ref[bi, tokens, :]) * (MEM_HD ** -0.5)
            p = jnp.exp(sc - jnp.max(sc, axis=1, keepdims=True))
            oh = _mm(p, mv_ref[bi, tokens, :]) / jnp.sum(p, axis=1, keepdims=True)
            ca_scr[bi * s:(bi + 1) * s, sl] = oh[0:s, :]
    h = h + _rms(_mm(ca_scr[...], wco_ref[...]), g_ref[3:4, :])

    hn = _rms(h, g_ref[4:5, :]).astype(BF16)
    d_ff = wup_ref.shape[1]
    acc = jnp.zeros(h.shape, F32)
    for c0 in range(0, d_ff, ff_chunk):
        up = jnp.dot(hn, wup_ref[:, c0:c0 + ff_chunk], preferred_element_type=F32)
        up = jnp.square(jnp.maximum(up, 0.0))
        acc = acc + jnp.dot(up.astype(BF16), wdn_ref[c0:c0 + ff_chunk, :], preferred_element_type=F32)
    o_ref[...] = h + _rms(acc, g_ref[5:6, :])


def _tail(h, mixes, w_mixes, gains, w_cq, mk, mv, w_co, w_up, w_down, tm, seq_len, layer=0):
    m, d = h.shape
    s = min(seq_len, tm)
    nb = tm // s
    n_mix = len(mixes)
    mem_rows = mk.shape[-2]
    mem_dim = MEM_HEADS * MEM_HD
    row = lambda i: (i, 0)
    tiles_per_seq = max(seq_len // tm, 1)
    if mk.ndim == 4:
        mem_spec = pl.BlockSpec((None, nb, mem_rows, MEM_HD), lambda i: (layer, i // tiles_per_seq, 0, 0))
    else:
        mem_spec = pl.BlockSpec((nb, mem_rows, MEM_HD), lambda i: (i // tiles_per_seq, 0, 0))
    in_specs = [pl.BlockSpec((tm, d), row)]
    in_specs += [pl.BlockSpec((tm, x.shape[1]), row) for x in mixes]
    in_specs += [_const_spec(w.shape) for w in w_mixes]
    in_specs += [_const_spec(gains.shape), _const_spec(w_cq.shape),
                 mem_spec, mem_spec,
                 _const_spec(w_co.shape), _const_spec(w_up.shape), _const_spec(w_down.shape)]
    return pl.pallas_call(
        functools.partial(_tail_kernel, n_mix=n_mix, nb=nb, s=s, ff_chunk=min(1024, w_up.shape[1])),
        out_shape=jax.ShapeDtypeStruct((m, d), F32),
        grid=(m // tm,),
        in_specs=in_specs,
        out_specs=pl.BlockSpec((tm, d), row),
        scratch_shapes=[pltpu.VMEM((tm, mem_dim), F32)],
        compiler_params=_cparams(("parallel",)),
        name="tail",
    )(h, *mixes, *w_mixes, gains, w_cq, mk, mv, w_co, w_up, w_down)


def _front1_kernel(h_ref, hprev_ref, xlast_ref, g_ref, mu_ref, wr_ref, wk_ref, wv_ref, w0_ref, w1_ref, w2_ref,
                   a0_ref, a1_ref, a2_ref, g1_ref, g2_ref, kk_ref, ka_ref, seg_ref,
                   r_out, ld_out, k_out, v_out, kk_out, a_out, gt_out, sh_out, *, seq, tiles_per_seq):
    i = pl.program_id(0)
    g = g_ref[...]
    hn = _rms(h_ref[...], g)
    if seq:
        prev_row = _rms(hprev_ref[SUBLANES - 1:SUBLANES, :], g)
        prev_row = jnp.where(i % tiles_per_seq == 0, xlast_ref[0], prev_row)
        rowid = lax.broadcasted_iota(jnp.int32, hn.shape, 0)
        xprev = jnp.where(rowid == 0, prev_row, pltpu.roll(hn, 1, axis=0))
        sh_out[0] = hn[hn.shape[0] - 1:, :]
    else:
        xprev = xlast_ref[...]
        sh_out[...] = hn
    xx = xprev - hn
    mix = lambda n: hn + xx * mu_ref[n:n + 1, :]
    r = _mm(mix(0), wr_ref[...])
    k = _mm(mix(2), wk_ref[...])
    v = _mm(mix(3), wv_ref[...])
    wl = w0_ref[...] + _mm(jnp.tanh(_mm(mix(1), w1_ref[...])), w2_ref[...])
    z = -wl
    softplus = jnp.maximum(z, 0.0) + jnp.log(1.0 + jnp.exp(-jnp.abs(z)))
    ld_out[...] = -jnp.exp(-softplus - 0.5)
    a = _sigmoid(a0_ref[...] + _mm(_mm(mix(4), a1_ref[...]), a2_ref[...]))
    gt_out[...] = _mm(_sigmoid(_mm(mix(5), g1_ref[...])), g2_ref[...])
    kk = k * kk_ref[...]
    ss = _mm2(kk * kk, seg_ref[...])
    kk_out[...] = kk / jnp.maximum(jnp.sqrt(ss), 1e-12)
    k_out[...] = k * (1.0 + (a - 1.0) * ka_ref[...])
    r_out[...] = r
    v_out[...] = v
    a_out[...] = a


def _front1(h, x_last, g, p, tm, s):
    m, d = h.shape
    bsz = m // s
    seq = s > 1
    row = lambda i: (i, 0)
    big = jax.ShapeDtypeStruct((m, d), F32)
    if seq:
        tps = s // tm
        per = tm // SUBLANES
        hprev_spec = pl.BlockSpec((SUBLANES, d), lambda i: (jnp.maximum(i * per - 1, 0), 0))
        xlast = x_last.reshape(bsz, 1, d)
        xlast_spec = pl.BlockSpec((1, 1, d), lambda i: (i // tps, 0, 0))
        sh_shape = jax.ShapeDtypeStruct((bsz, 1, d), F32)
        sh_spec = pl.BlockSpec((1, 1, d), lambda i: (i // tps, 0, 0))
        sem = ("arbitrary",)
    else:
        tps = 1
        hprev_spec = pl.BlockSpec((SUBLANES, d), lambda i: (0, 0))
        xlast = x_last
        xlast_spec = pl.BlockSpec((tm, d), row)
        sh_shape = big
        sh_spec = pl.BlockSpec((tm, d), row)
        sem = ("parallel",)
    consts = [g, p["mu"], p["w_r"], p["w_k"], p["w_v"], p["w0"], p["w1"], p["w2"], p["a0"], p["a1"], p["a2"],
              p["g1"], p["g2"], p["k_k"], p["k_a"], p["seg"]]
    outs = pl.pallas_call(
        functools.partial(_front1_kernel, seq=seq, tiles_per_seq=tps),
        out_shape=(big,) * 7 + (sh_shape,),
        grid=(m // tm,),
        in_specs=[pl.BlockSpec((tm, d), row), hprev_spec, xlast_spec] + [_const_spec(c.shape) for c in consts],
        out_specs=(pl.BlockSpec((tm, d), row),) * 7 + (sh_spec,),
        compiler_params=_cparams(sem),
        name="front1",
    )(h, h, xlast, *consts)
    return outs[:7], outs[7].reshape(bsz, d)


def _scan_kernel(r_ref, ld_ref, k_ref, v_ref, kk_ref, a_ref, gt_ref, s0_ref, rk_ref, lg_ref, lb_ref,
                 o_ref, st_ref, s_scr, *, n_chunks, n_groups, n_seq):
    c = pl.program_id(1)
    grp = SCAN_GROUP
    w = grp * RWKV_HD
    sbd = (lax.broadcasted_iota(jnp.int32, (w, w), 0) // RWKV_HD
           == lax.broadcasted_iota(jnp.int32, (w, w), 1) // RWKV_HD)
    chains = [(bb, q) for bb in range(n_seq) for q in range(n_groups)]

    @pl.when(c == 0)
    def _():
        for g, (bb, q) in enumerate(chains):
            s0 = s0_ref[bb, q]
            s_scr[g] = jnp.where(sbd, jnp.concatenate([s0] * grp, axis=1), 0.0)

    seqs = lambda ref: jnp.concatenate([ref[bb] for bb in range(n_seq)], axis=1)
    rep = lambda ref: jnp.concatenate([ref[...]] * n_seq, axis=1)

    def write(g, val):
        bb, q = chains[g]
        o_ref[bb, :, q * w:(q + 1) * w] = val

    _scan_chunk(seqs(r_ref), seqs(ld_ref), seqs(k_ref), seqs(v_ref), seqs(kk_ref), seqs(a_ref), seqs(gt_ref),
                rep(rk_ref), rep(lg_ref), rep(lb_ref), write, s_scr, len(chains))

    @pl.when(c == n_chunks - 1)
    def _():
        for g, (bb, q) in enumerate(chains):
            full = s_scr[g]
            acc = full[:, 0:RWKV_HD]
            for hh in range(1, grp):
                acc = acc + full[:, hh * RWKV_HD:(hh + 1) * RWKV_HD]
            st_ref[bb, q] = acc


def _scan_chunk(r, ld, k, v, kk, a_gate, gt, r_k, lnx_g, lnx_b, write, s_scr, n_groups):
    ch = SCAN_CHUNK
    grp = SCAN_GROUP
    w = grp * RWKV_HD
    rows = grp * ch
    gs = range(n_groups)
    rhead = lax.broadcasted_iota(jnp.int32, (rows, w), 0) // ch
    chead = lax.broadcasted_iota(jnp.int32, (rows, w), 1) // RWKV_HD
    bd = rhead == chead
    tile_r = lambda x: jnp.concatenate([x] * grp, axis=0)
    diag = lambda x: jnp.where(bd, tile_r(x), 0.0)
    part = lambda x, q: x[:, q * w:(q + 1) * w]

    b = kk * a_gate
    trow = lax.broadcasted_iota(jnp.int32, ld.shape, 0)
    cum = ld
    sh = 1
    while sh < ch:
        cum = cum + jnp.where(trow >= sh, pltpu.roll(cum, sh, axis=0), 0.0)
        sh *= 2
    clast = cum[ch - 1:ch, :]
    p_inv = jnp.exp(-cum)
    p_to = jnp.exp(clast - cum)
    rt = r * jnp.exp(cum)
    at = -kk * jnp.exp(cum - ld)
    bt = b * p_inv
    kt = k * p_inv
    bh = b * p_to
    kh = k * p_to
    p_end = jnp.exp(clast)

    tr = lax.broadcasted_iota(jnp.int32, (rows, rows), 0)
    tc = lax.broadcasted_iota(jnp.int32, (rows, rows), 1)
    same = (tr // ch) == (tc // ch)
    strict = same & (tc < tr)
    incl = same & (tc <= tr)
    eye = jnp.where(tr == tc, 1.0, 0.0)

    lhs = [jnp.concatenate([diag(part(at, q)), diag(part(rt, q))], axis=0).astype(BF16) for q in gs]
    rhs = [jnp.concatenate([tile_r(part(bt, q)), tile_r(part(kt, q))], axis=0) for q in gs]
    mall = [_mm_nt(lhs[q], rhs[q]) for q in gs]
    lp = [jnp.where(strict, mall[q][0:rows, 0:rows], 0.0) for q in gs]
    mak = [jnp.where(strict, mall[q][0:rows, rows:], 0.0) for q in gs]
    mrb = [jnp.where(incl, mall[q][rows:, 0:rows], 0.0) for q in gs]
    mrk = [jnp.where(incl, mall[q][rows:, rows:], 0.0) for q in gs]

    tinv = [eye + lp[q] for q in gs]
    n = 2
    while n < ch:
        lp = [_mm(lp[q], lp[q]) for q in gs]
        tinv = [tinv[q] + _mm(tinv[q], lp[q]) for q in gs]
        n *= 2

    vd = [diag(part(v, q)).astype(BF16) for q in gs]
    state = [s_scr[q] for q in gs]
    s0t = [_mm_nt(lhs[q], state[q]) for q in gs]
    wmat = [s0t[q][0:rows] + _mm(mak[q], vd[q]) for q in gs]
    u = [_mm(tinv[q], wmat[q]) for q in gs]
    ybd = [s0t[q][rows:] + _mm(mrb[q], u[q]) + _mm(mrk[q], vd[q]) for q in gs]
    upd = [_mm_tn(jnp.concatenate([u[q].astype(BF16), vd[q]], axis=0),
                  jnp.concatenate([diag(part(bh, q)), diag(part(kh, q))], axis=0)) for q in gs]
    for q in gs:
        s_scr[q] = state[q] * part(p_end, q) + upd[q]

    fa = lax.broadcasted_iota(jnp.int32, (w, w), 0) // RWKV_HD
    fb = lax.broadcasted_iota(jnp.int32, (w, w), 1) // RWKV_HD
    seg = jnp.where(fa == fb, 1.0, 0.0).astype(BF16)
    rkr = r * k * r_k
    for q in gs:
        y = ybd[q][0:ch]
        for hh in range(1, grp):
            y = y + ybd[q][hh * ch:(hh + 1) * ch]
        mean = _mm2(y, seg) * (1.0 / RWKV_HD)
        yc = y - mean
        var = _mm2(yc * yc, seg) * (1.0 / RWKV_HD)
        yn = yc * lax.rsqrt(var + GN_EPS) * part(lnx_g, q) + part(lnx_b, q)
        bonus = _mm2(part(rkr, q), seg) * part(v, q)
        write(q, (yn + bonus) * part(gt, q))


def _scan(streams, gate, s0, r_k, lnx_g, lnx_b):
    bsz, t, d = gate.shape
    w = SCAN_GROUP * RWKV_HD
    nq = d // w
    nc = t // SCAN_CHUNK
    n_seq = _largest_divisor(bsz, SCAN_SEQS_PER_STEP)
    tok = pl.BlockSpec((n_seq, SCAN_CHUNK, d), lambda b, c: (b, c, 0))
    st = pl.BlockSpec((n_seq, nq, w, RWKV_HD), lambda b, c: (b, 0, 0, 0))
    s0g = s0.reshape(bsz, nq, w, RWKV_HD)
    out, s_t = pl.pallas_call(
        functools.partial(_scan_kernel, n_chunks=nc, n_groups=nq, n_seq=n_seq),
        out_shape=(jax.ShapeDtypeStruct((bsz, t, d), F32), jax.ShapeDtypeStruct(s0g.shape, F32)),
        grid=(bsz // n_seq, nc),
        in_specs=[tok] * 7 + [st, _const_spec((1, d)), _const_spec((1, d)), _const_spec((1, d))],
        out_specs=(tok, st),
        scratch_shapes=[pltpu.VMEM((n_seq * nq, w, w), F32)],
        compiler_params=_cparams(("parallel", "arbitrary")),
        name="rwkv_scan",
    )(*streams, gate, s0g, r_k, lnx_g, lnx_b)
    return out, s_t.reshape(s0.shape)


def _wkv_step_kernel(r_ref, ld_ref, k_ref, kk_ref, a_ref, v_ref, gt_ref, s_ref, rk_ref, lg_ref, lb_ref,
                     o_ref, st_ref):
    r, k, kk = r_ref[0], k_ref[0], kk_ref[0]
    v = v_ref[0]
    state = s_ref[0]
    sa = -jnp.sum(state * kk, axis=2, keepdims=True)
    state = state * jnp.exp(ld_ref[0]) + sa * (kk * a_ref[0]) + v * k
    st_ref[0] = state
    y = jnp.sum(state * r, axis=2, keepdims=True)
    mean = jnp.mean(y, axis=1, keepdims=True)
    yc = y - mean
    var = jnp.mean(yc * yc, axis=1, keepdims=True)
    yn = yc * lax.rsqrt(var + GN_EPS) * lg_ref[...] + lb_ref[...]
    bonus = jnp.sum(r * k * rk_ref[...], axis=2, keepdims=True) * v
    o_ref[0] = (yn + bonus) * gt_ref[0]


def _wkv_step(streams, gate, s0, r_k, lnx_g, lnx_b):
    r, ld, k, v, kk, a = streams
    bsz, d = r.shape
    n = RWKV_HD
    h = d // n
    row = lambda x: x.reshape(-1, h, 1, n)
    col = lambda x: x.reshape(-1, h, n, 1)
    row_spec = pl.BlockSpec((1, h, 1, n), lambda b: (b, 0, 0, 0))
    col_spec = pl.BlockSpec((1, h, n, 1), lambda b: (b, 0, 0, 0))
    st_spec = pl.BlockSpec((1, h, n, n), lambda b: (b, 0, 0, 0))
    out, s_t = pl.pallas_call(
        _wkv_step_kernel,
        out_shape=(jax.ShapeDtypeStruct((bsz, h, n, 1), F32), jax.ShapeDtypeStruct(s0.shape, F32)),
        grid=(bsz,),
        in_specs=[row_spec] * 5 + [col_spec] * 2 + [st_spec, _const_spec((h, 1, n)), _const_spec((h, n, 1)),
                                                   _const_spec((h, n, 1))],
        out_specs=(col_spec, st_spec),
        compiler_params=_cparams(("parallel",)),
        name="wkv_step",
    )(row(r), row(ld), row(k), row(kk), row(a), col(v), col(gate), s0,
      r_k.reshape(h, 1, n), lnx_g.reshape(h, n, 1), lnx_b.reshape(h, n, 1))
    return out.reshape(bsz, d), s_t


def _row_tile(m, cap):
    t = min(m, cap)
    assert m % t == 0
    return t


def _largest_divisor(n, cap):
    return max(t for t in range(1, cap + 1) if n % t == 0)


def _pad_cols(w, to):
    return jnp.pad(w, ((0, 0), (0, to - w.shape[1])))


def _pad_rows(w, to):
    return jnp.pad(w, ((0, to - w.shape[0]), (0, 0)))


def kernel(x_prompt, x_sample, mem_prompt, cache_k, cache_v, page_table, cache_mem_k, cache_mem_v, state_conv, state_shift, state_wkv, norm_gain, w_in, conv_w, conv_b, conv_ln_g, conv_ln_b, w_out, rwkv_mu, rwkv_w_r, rwkv_w_k, rwkv_w_v, rwkv_w_o, rwkv_w0, rwkv_w1, rwkv_w2, rwkv_a0, rwkv_a1, rwkv_a2, rwkv_g1, rwkv_g2, rwkv_k_k, rwkv_k_a, rwkv_r_k, rwkv_lnx_g, rwkv_lnx_b, w_cq, w_ck, w_cv, w_co, w_up, w_down):
    bp, sp, d = x_prompt.shape
    bs, ss, _ = x_sample.shape
    assert ss == 1
    depth = norm_gain.shape[0]
    conv_dim = conv_w.shape[2]
    att_dim = (w_in.shape[2] - 2 * conv_dim) // 3
    mem = mem_prompt.shape[1]
    mem_dim = w_ck.shape[2]
    n_pages, page = page_table.shape[1], cache_k.shape[2]
    past_len = n_pages * page
    pages_per_block = MOBA_BLOCK // page
    att_heads = att_dim // ATT_HD
    nb_past = past_len // MOBA_BLOCK
    assert past_len % MOBA_BLOCK == 0 and nb_past >= MOBA_TOPK and sp % MOBA_BLOCK == 0
    heads = d // RWKV_HD

    tm_p = _row_tile(sp, 512)
    tm_s = bs
    bf = lambda x: x.astype(BF16)

    w_mem = bf(jnp.concatenate([w for l in range(depth) for w in (w_ck[l], w_cv[l])], axis=1))
    memkv = _proj(mem_prompt.reshape(bp * mem, d), w_mem, _row_tile(bp * mem, 512))
    memkv = memkv.reshape(bp, mem, depth, 2, mem_dim)
    interleave = lambda x: x.reshape(x.shape[0], mem * MEM_HEADS, MEM_HD)
    mem_k_p = [interleave(memkv[:, :, l, 0]) for l in range(depth)]
    mem_v_p = [interleave(memkv[:, :, l, 1]) for l in range(depth)]

    cos_p, sin_p = _rope_tables(jnp.arange(sp, dtype=jnp.int32))
    cos_s, sin_s = _rope_tables(jnp.full((tm_s,), past_len, dtype=jnp.int32))

    hp = x_prompt.reshape(bp * sp, d)
    hs = x_sample.reshape(bs, d)
    k_p, v_p, k_s, v_s, conv_p, conv_s, shift_p, shift_s, wkv_p, wkv_s = ([] for _ in range(10))

    for l in range(depth):
        gains = norm_gain[l]
        g0 = gains[0:1]
        tail_w = (gains, bf(w_cq[l]))
        tail_w2 = (bf(w_co[l]), bf(w_up[l]), bf(w_down[l]))
        mk_s = cache_mem_k.reshape(depth, bs, mem * MEM_HEADS, MEM_HD)
        mv_s = cache_mem_v.reshape(depth, bs, mem * MEM_HEADS, MEM_HD)
        if l % 2 == 0:
            e = l // 2
            w_in_b = bf(w_in[e])
            w_o = bf(w_out[e])
            cb, lg, lb = conv_b[e][None], conv_ln_g[e][None], conv_ln_b[e][None]

            u, q, kt, vt = _front0(hp, g0, w_in_b, cos_p, sin_p, conv_dim, att_dim, tm_p, sp)
            u3 = u.reshape(bp, sp, conv_dim)
            c = _conv_prompt(u3, conv_w[e], cb, lg, lb, _row_tile(sp, 256))
            att = _moba_prompt(q.reshape(bp, sp, att_dim), kt, vt)
            hp = _tail(hp, [c.reshape(bp * sp, conv_dim), att.reshape(bp * sp, att_dim)],
                       [w_o[:conv_dim], w_o[conv_dim:]], *tail_w, mem_k_p[l], mem_v_p[l], *tail_w2, tm_p, sp)
            token_major = lambda x: jnp.transpose(x.reshape(bp, att_heads, ATT_HD, sp), (0, 3, 1, 2))
            k_p.append(token_major(kt))
            v_p.append(token_major(vt))
            conv_p.append(u3[:, sp - (CONV_WIDTH - 1):])

            u, q, k, v = _front0(hs, g0, w_in_b, cos_s, sin_s, conv_dim, att_dim, tm_s, 1)
            c = _conv_step(jnp.swapaxes(state_conv[e], 0, 1), u, conv_w[e], cb, lg, lb)
            ckt = jnp.transpose(cache_k[e], (0, 2, 3, 1))
            cvt = jnp.transpose(cache_v[e], (0, 2, 3, 1))
            col = lambda x: x.reshape(bs, att_heads, ATT_HD, 1)
            pages_per_step = pages_per_block * _largest_divisor(nb_past, GATE_BLOCKS_PER_STEP)
            sel = _moba_gate(page_table, col(q), ckt, nb_past, pages_per_block, pages_per_step)
            att = _moba_decode(page_table, sel[:, :, :MOBA_TOPK], col(q), col(k), col(v), ckt, cvt,
                               pages_per_block).reshape(bs, att_dim)
            hs = _tail(hs, [c, att], [w_o[:conv_dim], w_o[conv_dim:]], *tail_w, mk_s, mv_s, *tail_w2, tm_s, 1, l)
            k_s.append(k.reshape(bs, 1, att_dim // ATT_HD, ATT_HD))
            v_s.append(v.reshape(bs, 1, att_dim // ATT_HD, ATT_HD))
            conv_s.append(jnp.concatenate([state_conv[e][:, 1:], u[:, None]], axis=1))
        else:
            o = l // 2
            lora = LANES
            gl = MXU_DIM
            feat = jnp.arange(d) // RWKV_HD
            prm = dict(
                mu=rwkv_mu[o], w_r=bf(rwkv_w_r[o]), w_k=bf(rwkv_w_k[o]), w_v=bf(rwkv_w_v[o]),
                w0=rwkv_w0[o][None], w1=bf(_pad_cols(rwkv_w1[o], lora)), w2=bf(_pad_rows(rwkv_w2[o], lora)),
                a0=rwkv_a0[o][None], a1=bf(_pad_cols(rwkv_a1[o], lora)), a2=bf(_pad_rows(rwkv_a2[o], lora)),
                g1=bf(_pad_cols(rwkv_g1[o], gl)), g2=bf(_pad_rows(rwkv_g2[o], gl)),
                k_k=rwkv_k_k[o][None], k_a=rwkv_k_a[o][None],
                seg=(feat[:, None] == feat[None, :]).astype(BF16))
            r_k = rwkv_r_k[o].reshape(1, d)
            lnx_g, lnx_b = rwkv_lnx_g[o][None], rwkv_lnx_b[o][None]
            w_o = bf(rwkv_w_o[o])

            streams, shift = _front1(hp, jnp.zeros((bp, d), F32), g0, prm, tm_p, sp)
            to3 = lambda x: x.reshape(bp, sp, d)
            gated, s_t = _scan([to3(x) for x in streams[:6]], to3(streams[6]),
                               jnp.zeros((bp, heads, RWKV_HD, RWKV_HD), F32), r_k, lnx_g, lnx_b)
            hp = _tail(hp, [gated.reshape(bp * sp, d)], [w_o], *tail_w, mem_k_p[l], mem_v_p[l], *tail_w2, tm_p, sp)
            shift_p.append(shift)
            wkv_p.append(s_t)

            streams, shift = _front1(hs, state_shift[o], g0, prm, tm_s, 1)
            gated, s_t = _wkv_step(streams[:6], streams[6], state_wkv[o], r_k, lnx_g, lnx_b)
            hs = _tail(hs, [gated], [w_o], *tail_w, mk_s, mv_s, *tail_w2, tm_s, 1, l)
            shift_s.append(shift)
            wkv_s.append(s_t)

    mem_shape = (bp, mem, MEM_HEADS, mem_dim // MEM_HEADS)
    return (hp.reshape(bp, sp, d), hs.reshape(bs, 1, d),
            jnp.stack(k_p), jnp.stack(v_p), jnp.stack(k_s), jnp.stack(v_s),
            jnp.stack(conv_p), jnp.stack(conv_s), jnp.stack(shift_p), jnp.stack(shift_s),
            jnp.stack(wkv_p), jnp.stack(wkv_s),
            jnp.stack([x.reshape(mem_shape) for x in mem_k_p]), jnp.stack([x.reshape(mem_shape) for x in mem_v_p]))
```

```python
import functools

import jax
import jax.numpy as jnp
from jax import lax
from jax.experimental import pallas as pl
from jax.experimental.pallas import tpu as pltpu

F32 = jnp.float32
BF16 = jnp.bfloat16

NORM_EPS = 1e-6
LN_EPS = 1e-5
GN_EPS = 64e-5
NEG_INF = -1e30
ATT_HD = 64
MEM_HEADS = 4
MEM_HD = 128
RWKV_HD = 64
MOBA_BLOCK = 256
MOBA_TOPK = 3
CONV_WIDTH = 31
ROPE_THETA = 10000.0

LANES = 128
SUBLANES = 8
BF16_ROWS = 16
MXU_DIM = 256
VMEM_LIMIT_BYTES = 56 * 1024 * 1024
SCAN_CHUNK = 64
SCAN_GROUP = MXU_DIM // RWKV_HD
HALO = 32
LOG2_E = 1.4426950408889634
SCAN_SEQS_PER_STEP = 1
DECODE_HEADS_PER_STEP = 4
GATE_BLOCKS_PER_STEP = 16


def _cparams(sem):
    return pltpu.CompilerParams(dimension_semantics=sem, vmem_limit_bytes=VMEM_LIMIT_BYTES)


def _const_spec(shape):
    nd = len(shape)
    return pl.BlockSpec(shape, lambda *_: (0,) * nd, pipeline_mode=pl.Buffered(1))


def _rms(x, g):
    return x * lax.rsqrt(jnp.mean(x * x, axis=-1, keepdims=True) + NORM_EPS) * g


def _sigmoid(x):
    return 1.0 / (1.0 + jnp.exp(-x))


def _mm(a, b):
    return jnp.dot(a.astype(BF16), b.astype(BF16), preferred_element_type=F32)


def _mm_nt(a, b):
    return lax.dot_general(a.astype(BF16), b.astype(BF16), (((1,), (1,)), ((), ())),
                           preferred_element_type=F32)


def _mm_tn(a, b):
    return lax.dot_general(a.astype(BF16), b.astype(BF16), (((0,), (0,)), ((), ())),
                           preferred_element_type=F32)


def _split(x):
    hi = x.astype(BF16)
    lo = (x - hi.astype(F32)).astype(BF16)
    return hi, lo


def _mm2(a, b):
    hi, lo = _split(a)
    return (jnp.dot(hi, b, preferred_element_type=F32)
            + jnp.dot(lo, b, preferred_element_type=F32))


def _proj_kernel(x_ref, w_ref, o_ref):
    o_ref[...] = _mm(x_ref[...], w_ref[...])


def _proj(x, w, tm):
    m, kdim = x.shape
    n = w.shape[1]
    return pl.pallas_call(
        _proj_kernel,
        out_shape=jax.ShapeDtypeStruct((m, n), F32),
        grid=(m // tm,),
        in_specs=[pl.BlockSpec((tm, kdim), lambda i: (i, 0)), _const_spec((kdim, n))],
        out_specs=pl.BlockSpec((tm, n), lambda i: (i, 0)),
        compiler_params=_cparams(("parallel",)),
        name="proj",
    )(x, w)


def _rope(x, cos, sin):
    width = x.shape[1]
    lane = lax.broadcasted_iota(jnp.int32, x.shape, 1)
    first_half = (lane % ATT_HD) < (ATT_HD // 2)
    partner = jnp.where(first_half,
                        pltpu.roll(x, width - ATT_HD // 2, axis=1),
                        pltpu.roll(x, ATT_HD // 2, axis=1))
    return x * cos + partner * sin


def _in_proj(x_ref, g_ref, w_ref, cos_ref, sin_ref, conv_dim, att_dim):
    hn = _rms(x_ref[...], g_ref[...]).astype(BF16)
    c0 = conv_dim
    val = jnp.dot(hn, w_ref[:, 0:c0], preferred_element_type=F32)
    gate = jnp.dot(hn, w_ref[:, c0:2 * c0], preferred_element_type=F32)
    u = val * _sigmoid(gate)
    reps = att_dim // LANES
    cos = jnp.concatenate([cos_ref[...]] * reps, axis=1)
    sin = jnp.concatenate([sin_ref[...]] * reps, axis=1)
    o = 2 * c0
    q = _rope(jnp.dot(hn, w_ref[:, o:o + att_dim], preferred_element_type=F32), cos, sin)
    k = _rope(jnp.dot(hn, w_ref[:, o + att_dim:o + 2 * att_dim], preferred_element_type=F32), cos, sin)
    v = jnp.dot(hn, w_ref[:, o + 2 * att_dim:o + 3 * att_dim], preferred_element_type=F32)
    return u, q, k, v


def _front0_step_kernel(x_ref, g_ref, w_ref, cos_ref, sin_ref, u_ref, q_ref, k_ref, v_ref, *, conv_dim, att_dim):
    u_ref[...], q_ref[...], k_ref[...], v_ref[...] = _in_proj(x_ref, g_ref, w_ref, cos_ref, sin_ref,
                                                               conv_dim, att_dim)


def _front0_step(h, g, w_in, cos_tab, sin_tab, conv_dim, att_dim):
    m, d = h.shape
    whole = lambda shape: pl.BlockSpec(shape, lambda i: (0,) * len(shape))
    return pl.pallas_call(
        functools.partial(_front0_step_kernel, conv_dim=conv_dim, att_dim=att_dim),
        out_shape=(jax.ShapeDtypeStruct((m, conv_dim), F32),) + (jax.ShapeDtypeStruct((m, att_dim), F32),) * 3,
        grid=(1,),
        in_specs=[whole((m, d)), _const_spec((1, d)), _const_spec(w_in.shape), whole((m, LANES)), whole((m, LANES))],
        out_specs=(whole((m, conv_dim)),) + (whole((m, att_dim)),) * 3,
        compiler_params=_cparams(("arbitrary",)),
        name="front0_step",
    )(h, g, w_in, cos_tab, sin_tab)


def _front0_seq_kernel(x_ref, g_ref, w_ref, cos_ref, sin_ref, cw_ref, cb_ref, lg_ref, lb_ref,
                       c_ref, ulast_ref, q_ref, k_ref, v_ref, v_scr, full_scr, *, conv_dim, att_dim, tiles_per_seq):
    i = pl.program_id(0)
    tm = x_ref.shape[0]

    @pl.when(i % tiles_per_seq == 0)
    def _():
        full_scr[0:HALO, :] = jnp.zeros((HALO, conv_dim), F32)

    u, q, k, v = _in_proj(x_ref, g_ref, w_ref, cos_ref, sin_ref, conv_dim, att_dim)
    full_scr[HALO:HALO + tm, :] = u
    ulast_ref[0] = u[tm - HALO:, :]
    q_ref[...] = q
    k_ref[0] = k.T
    v_scr[...] = v
    v_ref[0] = v_scr[...].T
    for r0 in range(0, tm, HALO):
        c_ref[r0:r0 + HALO, :] = _conv_block(full_scr, cw_ref, cb_ref, lg_ref, lb_ref, r0, HALO)
    full_scr[0:HALO, :] = full_scr[tm:tm + HALO, :]


def _front0_seq(h, g, w_in, cos_tab, sin_tab, conv_w, conv_b, ln_g, ln_b, att_dim, tm, seq_len):
    m, d = h.shape
    conv_dim = conv_w.shape[1]
    tps = seq_len // tm
    n_tab = cos_tab.shape[0] // tm
    row = lambda i: (i, 0)
    tab = lambda i: (i % n_tab, 0)
    kv_shape = jax.ShapeDtypeStruct((m // seq_len, att_dim, seq_len), F32)
    kv_spec = pl.BlockSpec((1, att_dim, tm), lambda i: (i // tps, 0, i % tps))
    wp = jnp.pad(conv_w, ((0, HALO - CONV_WIDTH), (0, 0)))
    return pl.pallas_call(
        functools.partial(_front0_seq_kernel, conv_dim=conv_dim, att_dim=att_dim, tiles_per_seq=tps),
        out_shape=(jax.ShapeDtypeStruct((m, conv_dim), F32), jax.ShapeDtypeStruct((m // seq_len, HALO, conv_dim), F32),
                   jax.ShapeDtypeStruct((m, att_dim), F32), kv_shape, kv_shape),
        grid=(m // tm,),
        in_specs=[pl.BlockSpec((tm, d), row), _const_spec((1, d)), _const_spec(w_in.shape),
                  pl.BlockSpec((tm, LANES), tab), pl.BlockSpec((tm, LANES), tab),
                  _const_spec((HALO, conv_dim)), _const_spec((1, conv_dim)), _const_spec((1, conv_dim)),
                  _const_spec((1, conv_dim))],
        out_specs=(pl.BlockSpec((tm, conv_dim), row), pl.BlockSpec((1, HALO, conv_dim), lambda i: (i // tps, 0, 0)),
                   pl.BlockSpec((tm, att_dim), row), kv_spec, kv_spec),
        scratch_shapes=[pltpu.VMEM((tm, att_dim), F32), pltpu.VMEM((HALO + tm, conv_dim), F32)],
        compiler_params=_cparams(("arbitrary",)),
        name="front0",
    )(h, g, w_in, cos_tab, sin_tab, wp, conv_b, ln_g, ln_b)


def _rope_tables(pos):
    half = ATT_HD // 2
    inv_freq = ROPE_THETA ** (-jnp.arange(half, dtype=F32) / half)
    ang = pos.astype(F32)[:, None] * inv_freq[None, :]
    cos, sin = jnp.cos(ang), jnp.sin(ang)
    reps = LANES // ATT_HD
    return (jnp.tile(jnp.concatenate([cos, cos], axis=1), (1, reps)),
            jnp.tile(jnp.concatenate([-sin, sin], axis=1), (1, reps)))


def _ln_silu(y, g, b):
    mu = jnp.mean(y, axis=-1, keepdims=True)
    yc = y - mu
    var = jnp.mean(yc * yc, axis=-1, keepdims=True)
    ln = yc * lax.rsqrt(var + LN_EPS) * g + b
    return ln * _sigmoid(ln)


def _conv_block(full_scr, w_ref, b_ref, lg_ref, lb_ref, r0, sub):
    first = HALO - (CONV_WIDTH - 1)
    acc = jnp.broadcast_to(b_ref[...], (sub, b_ref.shape[1]))
    for res in range(SUBLANES):
        rows = sub + (SUBLANES if res else 0)
        z = None
        for j in range(CONV_WIDTH):
            if (first + j) % SUBLANES != res:
                continue
            base = r0 + first + j - res
            term = w_ref[j:j + 1, :] * full_scr[base:base + rows, :]
            z = term if z is None else z + term
        acc = acc + z[res:res + sub, :]
    return _ln_silu(acc, lg_ref[...], lb_ref[...])


def _conv_step_kernel(st_ref, u_ref, w_ref, b_ref, lg_ref, lb_ref, c_ref):
    acc = b_ref[...] + w_ref[CONV_WIDTH - 1:CONV_WIDTH, :] * u_ref[...]
    for j in range(CONV_WIDTH - 1):
        acc = acc + w_ref[j:j + 1, :] * st_ref[j]
    c_ref[...] = _ln_silu(acc, lg_ref[...], lb_ref[...])


def _conv_step(state_t, u, w, b, lg, lb):
    bsz, c = u.shape
    wp = jnp.pad(w, ((0, HALO - CONV_WIDTH), (0, 0)))
    return pl.pallas_call(
        _conv_step_kernel,
        out_shape=jax.ShapeDtypeStruct((bsz, c), F32),
        grid=(1,),
        in_specs=[_const_spec(state_t.shape), _const_spec((bsz, c)), _const_spec((HALO, c)),
                  _const_spec((1, c)), _const_spec((1, c)), _const_spec((1, c))],
        out_specs=pl.BlockSpec((bsz, c), lambda i: (0, 0)),
        compiler_params=_cparams(("arbitrary",)),
        name="conv_step",
    )(state_t, u, wp, b, lg, lb)


def _top_blocks(gate, n_past):
    blk = lax.broadcasted_iota(jnp.int32, gate.shape, 0).astype(F32)
    g = jnp.where(blk < n_past.astype(F32), gate, NEG_INF)
    picks = []
    for r in range(MOBA_TOPK):
        m = jnp.max(g, axis=0, keepdims=True)
        idx = jnp.min(jnp.where(g == m, blk, 1e9), axis=0, keepdims=True)
        picks.append(jnp.where(r < n_past, idx, -1.0))
        g = jnp.where(blk == idx, -jnp.inf, g)
    return picks


def _moba_kernel(q_ref, k_ref, v_ref, o_ref, kmean_scr, kb_scr, vt_scr, s_scr, *, nb, scale):
    i = pl.program_id(2)
    blk = MOBA_BLOCK
    hpp = LANES // ATT_HD

    @pl.when(i == 0)
    def _():
        kmean_scr[...] = jnp.zeros_like(kmean_scr)
        vt_scr[...] = v_ref[0].astype(BF16)
        for j in range(nb):
            rows = slice(j * blk, (j + 1) * blk)
            kj = k_ref[0, :, rows].T
            kmean_scr[j:j + 1, :] = jnp.mean(kj, axis=0, keepdims=True)
            kb_scr[rows, :] = kj.astype(BF16)

    q = q_ref[0] * (scale * LOG2_E)
    lane = lax.broadcasted_iota(jnp.int32, q.shape, 1)
    q2 = jnp.concatenate([jnp.where((lane >= hh * ATT_HD) & (lane < (hh + 1) * ATT_HD), q, 0.0)
                          for hh in range(hpp)], axis=0)
    q_hi, q_lo = _split(q2)
    km_hi, km_lo = _split(kmean_scr[...])
    nbp = km_hi.shape[0]
    nt = (((1,), (1,)), ((), ()))

    def offset(j):
        return pl.multiple_of(jnp.minimum(j, nb - 1) * blk, blk)

    start = offset(i)
    stacked = lax.dot_general(jnp.concatenate([km_hi, km_lo, kb_scr[pl.ds(start, blk), :]], axis=0), q_hi, nt,
                              preferred_element_type=F32)
    gate = (stacked[0:nbp] + stacked[nbp:2 * nbp]
            + lax.dot_general(km_hi, q_lo, nt, preferred_element_type=F32))
    s = stacked[2 * nbp:]
    picks = _top_blocks(gate, i)

    ones = jnp.ones((SUBLANES, blk), BF16)

    def pv(off, p):
        vt_j = vt_scr[:, pl.ds(off, blk)]
        upd = [jnp.dot(vt_j[hh * ATT_HD:(hh + 1) * ATT_HD, :], p[:, hh * blk:(hh + 1) * blk],
                       preferred_element_type=F32) for hh in range(hpp)]
        return upd, jnp.dot(ones, p, preferred_element_type=F32)[0:1, :]

    def scores(off):
        return lax.dot_general(kb_scr[pl.ds(off, blk), :], q_hi, nt, preferred_element_type=F32)

    def past_scores(j):
        jf = j.astype(F32)
        chosen = (picks[0] == jf) | (picks[1] == jf) | (picks[2] == jf)
        return jnp.where(chosen, scores(offset(j)), NEG_INF)

    key = lax.broadcasted_iota(jnp.int32, s.shape, 0)
    qry = lax.broadcasted_iota(jnp.int32, s.shape, 1) % blk
    s = jnp.where(key <= qry, s, NEG_INF)
    s_scr[0] = past_scores(jnp.int32(0))
    m0 = jnp.max(s, axis=0, keepdims=True)
    acc0, l0 = pv(start, jnp.exp2(s - m0).astype(BF16))

    def fold(j, slot, carry):
        m, l, acc = carry
        sj = s_scr[slot]
        m_new = jnp.maximum(m, jnp.max(sj, axis=0, keepdims=True))
        alpha = jnp.exp2(m - m_new)
        upd, psum = pv(offset(j), jnp.exp2(sj - m_new).astype(BF16))
        l = alpha * l + psum
        acc = [alpha[:, hh * blk:(hh + 1) * blk] * acc[hh] + upd[hh] for hh in range(hpp)]
        return m_new, l, acc

    def body(jj, carry):
        j = 2 * jj
        s_scr[1] = past_scores(j + 1)
        carry = fold(j, 0, carry)
        s_scr[0] = past_scores(j + 2)
        return fold(j + 1, 1, carry)

    _, l, acc = lax.fori_loop(0, (i + 1) // 2, body, (m0, l0, acc0))
    out_t = jnp.concatenate([acc[hh] / l[:, hh * blk:(hh + 1) * blk] for hh in range(hpp)], axis=0)
    o_ref[0] = out_t.T


def _moba_prompt(q, kt, vt):
    bsz, s, a = q.shape
    nb = s // MOBA_BLOCK
    blk = pl.BlockSpec((1, MOBA_BLOCK, LANES), lambda b, hp, i: (b, i, hp))
    seq = pl.BlockSpec((1, LANES, s), lambda b, hp, i: (b, hp, 0))
    nb_pad = -(-nb // BF16_ROWS) * BF16_ROWS
    return pl.pallas_call(
        functools.partial(_moba_kernel, nb=nb, scale=ATT_HD ** -0.5),
        out_shape=jax.ShapeDtypeStruct((bsz, s, a), F32),
        grid=(bsz, a // LANES, nb),
        in_specs=[blk, seq, seq],
        out_specs=blk,
        scratch_shapes=[pltpu.VMEM((nb_pad, LANES), F32), pltpu.VMEM((s, LANES), BF16),
                        pltpu.VMEM((LANES, s), BF16),
                        pltpu.VMEM((2, MOBA_BLOCK, (LANES // ATT_HD) * MOBA_BLOCK), F32)],
        compiler_params=_cparams(("parallel", "parallel", "arbitrary")),
        name="moba_prompt",
    )(q, kt, vt)


def _moba_gate_kernel(pt_ref, q_ref, *refs, nb, pages_per_step, pages_per_block):
    del pt_ref
    page_refs = refs[:pages_per_step]
    sel_ref, qb_scr, g_scr = refs[pages_per_step:]
    j = pl.program_id(1)
    blocks_per_step = pages_per_step // pages_per_block

    @pl.when(j == 0)
    def _():
        qb_scr[...] = jnp.broadcast_to(q_ref[0], qb_scr.shape)

    qb = qb_scr[...]
    for bl in range(blocks_per_step):
        ksum = page_refs[bl * pages_per_block][0]
        for pg in range(1, pages_per_block):
            ksum = ksum + page_refs[bl * pages_per_block + pg][0]
        g_scr[j * blocks_per_step + bl] = jnp.sum(ksum * qb, axis=1)

    @pl.when(j == nb // blocks_per_step - 1)
    def _():
        gate = jnp.sum(g_scr[...], axis=2, keepdims=True) * (1.0 / MOBA_BLOCK)
        blk = lax.broadcasted_iota(jnp.int32, gate.shape, 0).astype(F32)
        lane = lax.broadcasted_iota(jnp.int32, sel_ref.shape[1:], 1)
        sel = jnp.zeros(sel_ref.shape[1:], F32)
        for r in range(MOBA_TOPK):
            m = jnp.max(gate, axis=0, keepdims=True)
            idx = jnp.min(jnp.where(gate == m, blk, 1e9), axis=0, keepdims=True)
            sel = jnp.where(lane == r, idx[0], sel)
            gate = jnp.where(blk == idx, -jnp.inf, gate)
        sel_ref[0] = sel.astype(jnp.int32)


def _moba_gate(page_table, q_col, cache_kt, nb, pages_per_block, pages_per_step):
    bsz, heads, hd, _ = q_col.shape
    page = cache_kt.shape[3]
    steps = nb * pages_per_block // pages_per_step

    def page_spec(w):
        return pl.BlockSpec((1, heads, hd, page), lambda b, j, pt: (pt[b, j * pages_per_step + w], 0, 0, 0))

    grid_spec = pltpu.PrefetchScalarGridSpec(
        num_scalar_prefetch=1,
        grid=(bsz, steps),
        in_specs=[pl.BlockSpec((1, heads, hd, 1), lambda b, j, pt: (b, 0, 0, 0))]
        + [page_spec(w) for w in range(pages_per_step)],
        out_specs=pl.BlockSpec((1, heads, LANES), lambda b, j, pt: (b, 0, 0)),
        scratch_shapes=[pltpu.VMEM((heads, hd, page), F32), pltpu.VMEM((nb, heads, page), F32)],
    )
    return pl.pallas_call(
        functools.partial(_moba_gate_kernel, nb=nb, pages_per_step=pages_per_step, pages_per_block=pages_per_block),
        out_shape=jax.ShapeDtypeStruct((bsz, heads, LANES), jnp.int32),
        grid_spec=grid_spec,
        compiler_params=_cparams(("parallel", "arbitrary")),
        name="moba_gate",
    )(page_table, q_col, *([cache_kt] * pages_per_step))


def _moba_decode_kernel(pt_ref, sel_ref, q_ref, kn_ref, vn_ref, *refs, n_pages, heads_per_step, scale):
    del pt_ref, sel_ref
    o_ref = refs[2 * n_pages * heads_per_step]
    for u in range(heads_per_step):
        k_refs = refs[u * n_pages:(u + 1) * n_pages]
        v_refs = refs[(heads_per_step + u) * n_pages:(heads_per_step + u + 1) * n_pages]
        q = q_ref[0, u]
        s_own = jnp.sum(q * kn_ref[0, u], axis=0, keepdims=True) * scale
        s = [jnp.sum(kr[0, 0] * q, axis=0, keepdims=True) * scale for kr in k_refs]
        m = s_own
        for sp in s:
            m = jnp.maximum(m, jnp.max(sp, axis=1, keepdims=True))
        p_own = jnp.exp(s_own - m)
        l = p_own
        acc = p_own * vn_ref[0, u]
        for sp, vr in zip(s, v_refs):
            p = jnp.exp(sp - m)
            l = l + jnp.sum(p, axis=1, keepdims=True)
            acc = acc + jnp.sum(vr[0, 0] * p, axis=1, keepdims=True)
        o_ref[0, u] = acc / l


def _moba_decode(page_table, sel, q_col, kn_col, vn_col, cache_kt, cache_vt, pages_per_block):
    bsz, heads, hd, _ = q_col.shape
    page = cache_kt.shape[3]
    n_pages = MOBA_TOPK * pages_per_block
    hps = _largest_divisor(heads, DECODE_HEADS_PER_STEP)
    vec = pl.BlockSpec((1, hps, hd, 1), lambda b, hg, pt, sl: (b, hg, 0, 0))

    def page_spec(u, r, w):
        return pl.BlockSpec(
            (1, 1, hd, page),
            lambda b, hg, pt, sl: (pt[b, pages_per_block * sl[b, hg * hps + u, r] + w], hg * hps + u, 0, 0))

    pages = [page_spec(u, r, w) for u in range(hps) for r in range(MOBA_TOPK) for w in range(pages_per_block)]
    grid_spec = pltpu.PrefetchScalarGridSpec(
        num_scalar_prefetch=2,
        grid=(bsz, heads // hps),
        in_specs=[vec, vec, vec] + pages + pages,
        out_specs=vec,
    )
    return pl.pallas_call(
        functools.partial(_moba_decode_kernel, n_pages=n_pages, heads_per_step=hps, scale=ATT_HD ** -0.5),
        out_shape=jax.ShapeDtypeStruct((bsz, heads, hd, 1), F32),
        grid_spec=grid_spec,
        compiler_params=_cparams(("parallel", "parallel")),
        name="moba_decode",
    )(page_table, sel, q_col, kn_col, vn_col, *([cache_kt] * len(pages)), *([cache_vt] * len(pages)))


def _tail_kernel(*refs, n_mix, nb, s, ff_chunk):
    h_ref = refs[0]
    mix_refs = refs[1:1 + n_mix]
    wmix_refs = refs[1 + n_mix:1 + 2 * n_mix]
    g_ref, wcq_ref, mk_ref, mv_ref, wco_ref, wup_ref, wdn_ref, o_ref, ca_scr = refs[1 + 2 * n_mix:]

    mix = _mm(mix_refs[0][...], wmix_refs[0][...])
    for mr, wr in zip(mix_refs[1:], wmix_refs[1:]):
        mix = mix + _mm(mr[...], wr[...])
    h = h_ref[...] + _rms(mix, g_ref[1:2, :])

    q = _mm(_rms(h, g_ref[2:3, :]), wcq_ref[...])
    rows = max(s, SUBLANES)
    mem = mk_ref.shape[1] // MEM_HEADS
    for bi in range(nb):
        qb = q[bi * s:(bi + 1) * s, :]
        if s < rows:
            qb = jnp.broadcast_to(qb[0:1, :], (rows, qb.shape[1]))
        for hh in range(MEM_HEADS):
            sl = slice(hh * MEM_HD, (hh + 1) * MEM_HD)
            tokens = pl.ds(hh, mem, stride=MEM_HEADS)
            sc = _mm_nt(qb[:, sl], mk_ref[bi, tokens, :]) * (MEM_HD ** -0.5)
            p = jnp.exp(sc - jnp.max(sc, axis=1, keepdims=True))
            oh = _mm(p, mv_ref[bi, tokens, :]) / jnp.sum(p, axis=1, keepdims=True)
            ca_scr[bi * s:(bi + 1) * s, sl] = oh[0:s, :]
    h = h + _rms(_mm(ca_scr[...], wco_ref[...]), g_ref[3:4, :])

    hn = _rms(h, g_ref[4:5, :]).astype(BF16)
    d_ff = wup_ref.shape[1]
    acc = jnp.zeros(h.shape, F32)
    for c0 in range(0, d_ff, ff_chunk):
        up = jnp.dot(hn, wup_ref[:, c0:c0 + ff_chunk], preferred_element_type=F32)
        up = jnp.square(jnp.maximum(up, 0.0))
        acc = acc + jnp.dot(up.astype(BF16), wdn_ref[c0:c0 + ff_chunk, :], preferred_element_type=F32)
    o_ref[...] = h + _rms(acc, g_ref[5:6, :])


def _tail(h, mixes, w_mixes, gains, w_cq, mk, mv, w_co, w_up, w_down, tm, seq_len, layer=0):
    m, d = h.shape
    s = min(seq_len, tm)
    nb = tm // s
    n_mix = len(mixes)
    mem_rows = mk.shape[-2]
    mem_dim = MEM_HEADS * MEM_HD
    row = lambda i: (i, 0)
    tiles_per_seq = max(seq_len // tm, 1)
    if mk.ndim == 4:
        mem_spec = pl.BlockSpec((None, nb, mem_rows, MEM_HD), lambda i: (layer, i // tiles_per_seq, 0, 0))
    else:
        mem_spec = pl.BlockSpec((nb, mem_rows, MEM_HD), lambda i: (i // tiles_per_seq, 0, 0))
    in_specs = [pl.BlockSpec((tm, d), row)]
    in_specs += [pl.BlockSpec((tm, x.shape[1]), row) for x in mixes]
    in_specs += [_const_spec(w.shape) for w in w_mixes]
    in_specs += [_const_spec(gains.shape), _const_spec(w_cq.shape),
                 mem_spec, mem_spec,
                 _const_spec(w_co.shape), _const_spec(w_up.shape), _const_spec(w_down.shape)]
    return pl.pallas_call(
        functools.partial(_tail_kernel, n_mix=n_mix, nb=nb, s=s, ff_chunk=min(1024, w_up.shape[1])),
        out_shape=jax.ShapeDtypeStruct((m, d), F32),
        grid=(m // tm,),
        in_specs=in_specs,
        out_specs=pl.BlockSpec((tm, d), row),
        scratch_shapes=[pltpu.VMEM((tm, mem_dim), F32)],
        compiler_params=_cparams(("parallel",)),
        name="tail",
    )(h, *mixes, *w_mixes, gains, w_cq, mk, mv, w_co, w_up, w_down)


def _front1_kernel(h_ref, hprev_ref, xlast_ref, g_ref, mu_ref, wr_ref, wk_ref, wv_ref, w0_ref, w1_ref, w2_ref,
                   a0_ref, a1_ref, a2_ref, g1_ref, g2_ref, kk_ref, ka_ref, seg_ref,
                   r_out, ld_out, k_out, v_out, kk_out, a_out, gt_out, sh_out, *, seq, tiles_per_seq):
    i = pl.program_id(0)
    g = g_ref[...]
    hn = _rms(h_ref[...], g)
    if seq:
        prev_row = _rms(hprev_ref[SUBLANES - 1:SUBLANES, :], g)
        prev_row = jnp.where(i % tiles_per_seq == 0, xlast_ref[0], prev_row)
        rowid = lax.broadcasted_iota(jnp.int32, hn.shape, 0)
        xprev = jnp.where(rowid == 0, prev_row, pltpu.roll(hn, 1, axis=0))
        sh_out[0] = hn[hn.shape[0] - 1:, :]
    else:
        xprev = xlast_ref[...]
        sh_out[...] = hn
    xx = xprev - hn
    mix = lambda n: hn + xx * mu_ref[n:n + 1, :]
    r = _mm(mix(0), wr_ref[...])
    k = _mm(mix(2), wk_ref[...])
    v = _mm(mix(3), wv_ref[...])
    wl = w0_ref[...] + _mm(jnp.tanh(_mm(mix(1), w1_ref[...])), w2_ref[...])
    z = -wl
    softplus = jnp.maximum(z, 0.0) + jnp.log(1.0 + jnp.exp(-jnp.abs(z)))
    ld_out[...] = -jnp.exp(-softplus - 0.5)
    a = _sigmoid(a0_ref[...] + _mm(_mm(mix(4), a1_ref[...]), a2_ref[...]))
    gt_out[...] = _mm(_sigmoid(_mm(mix(5), g1_ref[...])), g2_ref[...])
    kk = k * kk_ref[...]
    ss = _mm2(kk * kk, seg_ref[...])
    kk_out[...] = kk / jnp.maximum(jnp.sqrt(ss), 1e-12)
    k_out[...] = k * (1.0 + (a - 1.0) * ka_ref[...])
    r_out[...] = r
    v_out[...] = v
    a_out[...] = a


def _front1(h, x_last, g, p, tm, s):
    m, d = h.shape
    bsz = m // s
    seq = s > 1
    row = lambda i: (i, 0)
    big = jax.ShapeDtypeStruct((m, d), F32)
    if seq:
        tps = s // tm
        per = tm // SUBLANES
        hprev_spec = pl.BlockSpec((SUBLANES, d), lambda i: (jnp.maximum(i * per - 1, 0), 0))
        xlast = x_last.reshape(bsz, 1, d)
        xlast_spec = pl.BlockSpec((1, 1, d), lambda i: (i // tps, 0, 0))
        sh_shape = jax.ShapeDtypeStruct((bsz, 1, d), F32)
        sh_spec = pl.BlockSpec((1, 1, d), lambda i: (i // tps, 0, 0))
        sem = ("arbitrary",)
    else:
        tps = 1
        hprev_spec = pl.BlockSpec((SUBLANES, d), lambda i: (0, 0))
        xlast = x_last
        xlast_spec = pl.BlockSpec((tm, d), row)
        sh_shape = big
        sh_spec = pl.BlockSpec((tm, d), row)
        sem = ("parallel",)
    consts = [g, p["mu"], p["w_r"], p["w_k"], p["w_v"], p["w0"], p["w1"], p["w2"], p["a0"], p["a1"], p["a2"],
              p["g1"], p["g2"], p["k_k"], p["k_a"], p["seg"]]
    outs = pl.pallas_call(
        functools.partial(_front1_kernel, seq=seq, tiles_per_seq=tps),
        out_shape=(big,) * 7 + (sh_shape,),
        grid=(m // tm,),
        in_specs=[pl.BlockSpec((tm, d), row), hprev_spec, xlast_spec] + [_const_spec(c.shape) for c in consts],
        out_specs=(pl.BlockSpec((tm, d), row),) * 7 + (sh_spec,),
        compiler_params=_cparams(sem),
        name="front1",
    )(h, h, xlast, *consts)
    return outs[:7], outs[7].reshape(bsz, d)


def _scan_kernel(r_ref, ld_ref, k_ref, v_ref, kk_ref, a_ref, gt_ref, s0_ref, rk_ref, lg_ref, lb_ref,
                 o_ref, st_ref, s_scr, *, n_chunks, n_groups, n_seq):
    c = pl.program_id(1)
    grp = SCAN_GROUP
    w = grp * RWKV_HD
    sbd = (lax.broadcasted_iota(jnp.int32, (w, w), 0) // RWKV_HD
           == lax.broadcasted_iota(jnp.int32, (w, w), 1) // RWKV_HD)
    chains = [(bb, q) for bb in range(n_seq) for q in range(n_groups)]

    @pl.when(c == 0)
    def _():
        for g, (bb, q) in enumerate(chains):
            s0 = s0_ref[bb, q]
            s_scr[g] = jnp.where(sbd, jnp.concatenate([s0] * grp, axis=1), 0.0)

    seqs = lambda ref: jnp.concatenate([ref[bb] for bb in range(n_seq)], axis=1)
    rep = lambda ref: jnp.concatenate([ref[...]] * n_seq, axis=1)

    def write(g, val):
        bb, q = chains[g]
        o_ref[bb, :, q * w:(q + 1) * w] = val

    _scan_chunk(seqs(r_ref), seqs(ld_ref), seqs(k_ref), seqs(v_ref), seqs(kk_ref), seqs(a_ref), seqs(gt_ref),
                rep(rk_ref), rep(lg_ref), rep(lb_ref), write, s_scr, len(chains))

    @pl.when(c == n_chunks - 1)
    def _():
        for g, (bb, q) in enumerate(chains):
            full = s_scr[g]
            acc = full[:, 0:RWKV_HD]
            for hh in range(1, grp):
                acc = acc + full[:, hh * RWKV_HD:(hh + 1) * RWKV_HD]
            st_ref[bb, q] = acc


def _scan_chunk(r, ld, k, v, kk, a_gate, gt, r_k, lnx_g, lnx_b, write, s_scr, n_groups):
    ch = SCAN_CHUNK
    grp = SCAN_GROUP
    w = grp * RWKV_HD
    rows = grp * ch
    gs = range(n_groups)
    rhead = lax.broadcasted_iota(jnp.int32, (rows, w), 0) // ch
    chead = lax.broadcasted_iota(jnp.int32, (rows, w), 1) // RWKV_HD
    bd = rhead == chead
    tile_r = lambda x: jnp.concatenate([x] * grp, axis=0)
    diag = lambda x: jnp.where(bd, tile_r(x), 0.0)
    part = lambda x, q: x[:, q * w:(q + 1) * w]

    b = kk * a_gate
    trow = lax.broadcasted_iota(jnp.int32, ld.shape, 0)
    cum = ld
    sh = 1
    while sh < ch:
        cum = cum + jnp.where(trow >= sh, pltpu.roll(cum, sh, axis=0), 0.0)
        sh *= 2
    clast = cum[ch - 1:ch, :]
    p_inv = jnp.exp(-cum)
    p_to = jnp.exp(clast - cum)
    rt = r * jnp.exp(cum)
    at = -kk * jnp.exp(cum - ld)
    bt = b * p_inv
    kt = k * p_inv
    bh = b * p_to
    kh = k * p_to
    p_end = jnp.exp(clast)

    tr = lax.broadcasted_iota(jnp.int32, (rows, rows), 0)
    tc = lax.broadcasted_iota(jnp.int32, (rows, rows), 1)
    same = (tr // ch) == (tc // ch)
    strict = same & (tc < tr)
    incl = same & (tc <= tr)
    eye = jnp.where(tr == tc, 1.0, 0.0)

    lhs = [jnp.concatenate([diag(part(at, q)), diag(part(rt, q))], axis=0).astype(BF16) for q in gs]
    rhs = [jnp.concatenate([tile_r(part(bt, q)), tile_r(part(kt, q))], axis=0) for q in gs]
    mall = [_mm_nt(lhs[q], rhs[q]) for q in gs]
    lp = [jnp.where(strict, mall[q][0:rows, 0:rows], 0.0) for q in gs]
    mak = [jnp.where(strict, mall[q][0:rows, rows:], 0.0) for q in gs]
    mrb = [jnp.where(incl, mall[q][rows:, 0:rows], 0.0) for q in gs]
    mrk = [jnp.where(incl, mall[q][rows:, rows:], 0.0) for q in gs]

    tinv = [eye + lp[q] for q in gs]
    n = 2
    while n < ch:
        lp = [_mm(lp[q], lp[q]) for q in gs]
        tinv = [tinv[q] + _mm(tinv[q], lp[q]) for q in gs]
        n *= 2

    vd = [diag(part(v, q)).astype(BF16) for q in gs]
    state = [s_scr[q] for q in gs]
    s0t = [_mm_nt(lhs[q], state[q]) for q in gs]
    wmat = [s0t[q][0:rows] + _mm(mak[q], vd[q]) for q in gs]
    u = [_mm(tinv[q], wmat[q]) for q in gs]
    ybd = [s0t[q][rows:] + _mm(mrb[q], u[q]) + _mm(mrk[q], vd[q]) for q in gs]
    upd = [_mm_tn(jnp.concatenate([u[q].astype(BF16), vd[q]], axis=0),
                  jnp.concatenate([diag(part(bh, q)), diag(part(kh, q))], axis=0)) for q in gs]
    for q in gs:
        s_scr[q] = state[q] * part(p_end, q) + upd[q]

    fa = lax.broadcasted_iota(jnp.int32, (w, w), 0) // RWKV_HD
    fb = lax.broadcasted_iota(jnp.int32, (w, w), 1) // RWKV_HD
    seg = jnp.where(fa == fb, 1.0, 0.0).astype(BF16)
    rkr = r * k * r_k
    for q in gs:
        y = ybd[q][0:ch]
        for hh in range(1, grp):
            y = y + ybd[q][hh * ch:(hh + 1) * ch]
        mean = _mm2(y, seg) * (1.0 / RWKV_HD)
        yc = y - mean
        var = _mm2(yc * yc, seg) * (1.0 / RWKV_HD)
        yn = yc * lax.rsqrt(var + GN_EPS) * part(lnx_g, q) + part(lnx_b, q)
        bonus = _mm2(part(rkr, q), seg) * part(v, q)
        write(q, (yn + bonus) * part(gt, q))


def _scan(streams, gate, s0, r_k, lnx_g, lnx_b):
    bsz, t, d = gate.shape
    w = SCAN_GROUP * RWKV_HD
    nq = d // w
    nc = t // SCAN_CHUNK
    n_seq = _largest_divisor(bsz, SCAN_SEQS_PER_STEP)
    tok = pl.BlockSpec((n_seq, SCAN_CHUNK, d), lambda b, c: (b, c, 0))
    st = pl.BlockSpec((n_seq, nq, w, RWKV_HD), lambda b, c: (b, 0, 0, 0))
    s0g = s0.reshape(bsz, nq, w, RWKV_HD)
    out, s_t = pl.pallas_call(
        functools.partial(_scan_kernel, n_chunks=nc, n_groups=nq, n_seq=n_seq),
        out_shape=(jax.ShapeDtypeStruct((bsz, t, d), F32), jax.ShapeDtypeStruct(s0g.shape, F32)),
        grid=(bsz // n_seq, nc),
        in_specs=[tok] * 7 + [st, _const_spec((1, d)), _const_spec((1, d)), _const_spec((1, d))],
        out_specs=(tok, st),
        scratch_shapes=[pltpu.VMEM((n_seq * nq, w, w), F32)],
        compiler_params=_cparams(("parallel", "arbitrary")),
        name="rwkv_scan",
    )(*streams, gate, s0g, r_k, lnx_g, lnx_b)
    return out, s_t.reshape(s0.shape)


def _wkv_step_kernel(r_ref, ld_ref, k_ref, kk_ref, a_ref, v_ref, gt_ref, s_ref, rk_ref, lg_ref, lb_ref,
                     o_ref, st_ref):
    r, k, kk = r_ref[0], k_ref[0], kk_ref[0]
    v = v_ref[0]
    state = s_ref[0]
    sa = -jnp.sum(state * kk, axis=2, keepdims=True)
    state = state * jnp.exp(ld_ref[0]) + sa * (kk * a_ref[0]) + v * k
    st_ref[0] = state
    y = jnp.sum(state * r, axis=2, keepdims=True)
    mean = jnp.mean(y, axis=1, keepdims=True)
    yc = y - mean
    var = jnp.mean(yc * yc, axis=1, keepdims=True)
    yn = yc * lax.rsqrt(var + GN_EPS) * lg_ref[...] + lb_ref[...]
    bonus = jnp.sum(r * k * rk_ref[...], axis=2, keepdims=True) * v
    o_ref[0] = (yn + bonus) * gt_ref[0]


def _wkv_step(streams, gate, s0, r_k, lnx_g, lnx_b):
    r, ld, k, v, kk, a = streams
    bsz, d = r.shape
    n = RWKV_HD
    h = d // n
    row = lambda x: x.reshape(-1, h, 1, n)
    col = lambda x: x.reshape(-1, h, n, 1)
    row_spec = pl.BlockSpec((1, h, 1, n), lambda b: (b, 0, 0, 0))
    col_spec = pl.BlockSpec((1, h, n, 1), lambda b: (b, 0, 0, 0))
    st_spec = pl.BlockSpec((1, h, n, n), lambda b: (b, 0, 0, 0))
    out, s_t = pl.pallas_call(
        _wkv_step_kernel,
        out_shape=(jax.ShapeDtypeStruct((bsz, h, n, 1), F32), jax.ShapeDtypeStruct(s0.shape, F32)),
        grid=(bsz,),
        in_specs=[row_spec] * 5 + [col_spec] * 2 + [st_spec, _const_spec((h, 1, n)), _const_spec((h, n, 1)),
                                                   _const_spec((h, n, 1))],
        out_specs=(col_spec, st_spec),
        compiler_params=_cparams(("parallel",)),
        name="wkv_step",
    )(row(r), row(ld), row(k), row(kk), row(a), col(v), col(gate), s0,
      r_k.reshape(h, 1, n), lnx_g.reshape(h, n, 1), lnx_b.reshape(h, n, 1))
    return out.reshape(bsz, d), s_t


def _row_tile(m, cap):
    t = min(m, cap)
    assert m % t == 0
    return t


def _largest_divisor(n, cap):
    return max(t for t in range(1, cap + 1) if n % t == 0)


def _pad_cols(w, to):
    return jnp.pad(w, ((0, 0), (0, to - w.shape[1])))


def _pad_rows(w, to):
    return jnp.pad(w, ((0, to - w.shape[0]), (0, 0)))


def kernel(x_prompt, x_sample, mem_prompt, cache_k, cache_v, page_table, cache_mem_k, cache_mem_v, state_conv, state_shift, state_wkv, norm_gain, w_in, conv_w, conv_b, conv_ln_g, conv_ln_b, w_out, rwkv_mu, rwkv_w_r, rwkv_w_k, rwkv_w_v, rwkv_w_o, rwkv_w0, rwkv_w1, rwkv_w2, rwkv_a0, rwkv_a1, rwkv_a2, rwkv_g1, rwkv_g2, rwkv_k_k, rwkv_k_a, rwkv_r_k, rwkv_lnx_g, rwkv_lnx_b, w_cq, w_ck, w_cv, w_co, w_up, w_down):
    bp, sp, d = x_prompt.shape
    bs, ss, _ = x_sample.shape
    assert ss == 1
    depth = norm_gain.shape[0]
    conv_dim = conv_w.shape[2]
    att_dim = (w_in.shape[2] - 2 * conv_dim) // 3
    mem = mem_prompt.shape[1]
    mem_dim = w_ck.shape[2]
    n_pages, page = page_table.shape[1], cache_k.shape[2]
    past_len = n_pages * page
    pages_per_block = MOBA_BLOCK // page
    att_heads = att_dim // ATT_HD
    nb_past = past_len // MOBA_BLOCK
    assert past_len % MOBA_BLOCK == 0 and nb_past >= MOBA_TOPK and sp % MOBA_BLOCK == 0
    heads = d // RWKV_HD

    tm_p = _row_tile(sp, 512)
    tm_s = bs
    bf = lambda x: x.astype(BF16)

    w_mem = bf(jnp.concatenate([w for l in range(depth) for w in (w_ck[l], w_cv[l])], axis=1))
    memkv = _proj(mem_prompt.reshape(bp * mem, d), w_mem, _row_tile(bp * mem, 512))
    memkv = memkv.reshape(bp, mem, depth, 2, mem_dim)
    interleave = lambda x: x.reshape(x.shape[0], mem * MEM_HEADS, MEM_HD)
    mem_k_p = [interleave(memkv[:, :, l, 0]) for l in range(depth)]
    mem_v_p = [interleave(memkv[:, :, l, 1]) for l in range(depth)]

    cos_p, sin_p = _rope_tables(jnp.arange(sp, dtype=jnp.int32))
    cos_s, sin_s = _rope_tables(jnp.full((tm_s,), past_len, dtype=jnp.int32))

    hp = x_prompt.reshape(bp * sp, d)
    hs = x_sample.reshape(bs, d)
    k_p, v_p, k_s, v_s, conv_p, conv_s, shift_p, shift_s, wkv_p, wkv_s = ([] for _ in range(10))

    for l in range(depth):
        gains = norm_gain[l]
        g0 = gains[0:1]
        tail_w = (gains, bf(w_cq[l]))
        tail_w2 = (bf(w_co[l]), bf(w_up[l]), bf(w_down[l]))
        mk_s = cache_mem_k.reshape(depth, bs, mem * MEM_HEADS, MEM_HD)
        mv_s = cache_mem_v.reshape(depth, bs, mem * MEM_HEADS, MEM_HD)
        if l % 2 == 0:
            e = l // 2
            w_in_b = bf(w_in[e])
            w_o = bf(w_out[e])
            cb, lg, lb = conv_b[e][None], conv_ln_g[e][None], conv_ln_b[e][None]

            c, u_last, q, kt, vt = _front0_seq(hp, g0, w_in_b, cos_p, sin_p, conv_w[e], cb, lg, lb, att_dim, tm_p, sp)
            att = _moba_prompt(q.reshape(bp, sp, att_dim), kt, vt)
            hp = _tail(hp, [c, att.reshape(bp * sp, att_dim)],
                       [w_o[:conv_dim], w_o[conv_dim:]], *tail_w, mem_k_p[l], mem_v_p[l], *tail_w2, tm_p, sp)
            token_major = lambda x: jnp.transpose(x.reshape(bp, att_heads, ATT_HD, sp), (0, 3, 1, 2))
            k_p.append(token_major(kt))
            v_p.append(token_major(vt))
            conv_p.append(u_last[:, HALO - (CONV_WIDTH - 1):])

            u, q, k, v = _front0_step(hs, g0, w_in_b, cos_s, sin_s, conv_dim, att_dim)
            c = _conv_step(jnp.swapaxes(state_conv[e], 0, 1), u, conv_w[e], cb, lg, lb)
            ckt = jnp.transpose(cache_k[e], (0, 2, 3, 1))
            cvt = jnp.transpose(cache_v[e], (0, 2, 3, 1))
            col = lambda x: x.reshape(bs, att_heads, ATT_HD, 1)
            pages_per_step = pages_per_block * _largest_divisor(nb_past, GATE_BLOCKS_PER_STEP)
            sel = _moba_gate(page_table, col(q), ckt, nb_past, pages_per_block, pages_per_step)
            att = _moba_decode(page_table, sel[:, :, :MOBA_TOPK], col(q), col(k), col(v), ckt, cvt,
                               pages_per_block).reshape(bs, att_dim)
            hs = _tail(hs, [c, att], [w_o[:conv_dim], w_o[conv_dim:]], *tail_w, mk_s, mv_s, *tail_w2, tm_s, 1, l)
            k_s.append(k.reshape(bs, 1, att_dim // ATT_HD, ATT_HD))
            v_s.append(v.reshape(bs, 1, att_dim // ATT_HD, ATT_HD))
            conv_s.append(jnp.concatenate([state_conv[e][:, 1:], u[:, None]], axis=1))
        else:
            o = l // 2
            lora = LANES
            gl = MXU_DIM
            feat = jnp.arange(d) // RWKV_HD
            prm = dict(
                mu=rwkv_mu[o], w_r=bf(rwkv_w_r[o]), w_k=bf(rwkv_w_k[o]), w_v=bf(rwkv_w_v[o]),
                w0=rwkv_w0[o][None], w1=bf(_pad_cols(rwkv_w1[o], lora)), w2=bf(_pad_rows(rwkv_w2[o], lora)),
                a0=rwkv_a0[o][None], a1=bf(_pad_cols(rwkv_a1[o], lora)), a2=bf(_pad_rows(rwkv_a2[o], lora)),
                g1=bf(_pad_cols(rwkv_g1[o], gl)), g2=bf(_pad_rows(rwkv_g2[o], gl)),
                k_k=rwkv_k_k[o][None], k_a=rwkv_k_a[o][None],
                seg=(feat[:, None] == feat[None, :]).astype(BF16))
            r_k = rwkv_r_k[o].reshape(1, d)
            lnx_g, lnx_b = rwkv_lnx_g[o][None], rwkv_lnx_b[o][None]
            w_o = bf(rwkv_w_o[o])

            streams, shift = _front1(hp, jnp.zeros((bp, d), F32), g0, prm, tm_p, sp)
            to3 = lambda x: x.reshape(bp, sp, d)
            gated, s_t = _scan([to3(x) for x in streams[:6]], to3(streams[6]),
                               jnp.zeros((bp, heads, RWKV_HD, RWKV_HD), F32), r_k, lnx_g, lnx_b)
            hp = _tail(hp, [gated.reshape(bp * sp, d)], [w_o], *tail_w, mem_k_p[l], mem_v_p[l], *tail_w2, tm_p, sp)
            shift_p.append(shift)
            wkv_p.append(s_t)

            streams, shift = _front1(hs, state_shift[o], g0, prm, tm_s, 1)
            gated, s_t = _wkv_step(streams[:6], streams[6], state_wkv[o], r_k, lnx_g, lnx_b)
            hs = _tail(hs, [gated], [w_o], *tail_w, mk_s, mv_s, *tail_w2, tm_s, 1, l)
            shift_s.append(shift)
            wkv_s.append(s_t)

    mem_shape = (bp, mem, MEM_HEADS, mem_dim // MEM_HEADS)
    return (hp.reshape(bp, sp, d), hs.reshape(bs, 1, d),
            jnp.stack(k_p), jnp.stack(v_p), jnp.stack(k_s), jnp.stack(v_s),
            jnp.stack(conv_p), jnp.stack(conv_s), jnp.stack(shift_p), jnp.stack(shift_s),
            jnp.stack(wkv_p), jnp.stack(wkv_s),
            jnp.stack([x.reshape(mem_shape) for x in mem_k_p]), jnp.stack([x.reshape(mem_shape) for x in mem_v_p]))
```

```python
import functools

import jax
import jax.numpy as jnp
from jax import lax
from jax.experimental import pallas as pl
from jax.experimental.pallas import tpu as pltpu

F32 = jnp.float32
BF16 = jnp.bfloat16

NORM_EPS = 1e-6
LN_EPS = 1e-5
GN_EPS = 64e-5
NEG_INF = -1e30
ATT_HD = 64
MEM_HEADS = 4
MEM_HD = 128
RWKV_HD = 64
MOBA_BLOCK = 256
MOBA_TOPK = 3
CONV_WIDTH = 31
ROPE_THETA = 10000.0

LANES = 128
SUBLANES = 8
BF16_ROWS = 16
MXU_DIM = 256
VMEM_LIMIT_BYTES = 56 * 1024 * 1024
SCAN_CHUNK = 64
SCAN_GROUP = MXU_DIM // RWKV_HD
HALO = 32
LOG2_E = 1.4426950408889634
SCAN_SEQS_PER_STEP = 4
DECODE_HEADS_PER_STEP = 4
GATE_BLOCKS_PER_STEP = 16


def _cparams(sem):
    return pltpu.CompilerParams(dimension_semantics=sem, vmem_limit_bytes=VMEM_LIMIT_BYTES)


def _const_spec(shape):
    nd = len(shape)
    return pl.BlockSpec(shape, lambda *_: (0,) * nd, pipeline_mode=pl.Buffered(1))


def _rms(x, g):
    return x * lax.rsqrt(jnp.mean(x * x, axis=-1, keepdims=True) + NORM_EPS) * g


def _sigmoid(x):
    return 1.0 / (1.0 + jnp.exp(-x))


def _mm(a, b):
    return jnp.dot(a.astype(BF16), b.astype(BF16), preferred_element_type=F32)


def _mm_nt(a, b):
    return lax.dot_general(a.astype(BF16), b.astype(BF16), (((1,), (1,)), ((), ())),
                           preferred_element_type=F32)


def _mm_tn(a, b):
    return lax.dot_general(a.astype(BF16), b.astype(BF16), (((0,), (0,)), ((), ())),
                           preferred_element_type=F32)


def _split(x):
    hi = x.astype(BF16)
    lo = (x - hi.astype(F32)).astype(BF16)
    return hi, lo


def _mm2(a, b):
    hi, lo = _split(a)
    return (jnp.dot(hi, b, preferred_element_type=F32)
            + jnp.dot(lo, b, preferred_element_type=F32))


def _proj_kernel(x_ref, w_ref, o_ref):
    o_ref[...] = _mm(x_ref[...], w_ref[...])


def _proj(x, w, tm):
    m, kdim = x.shape
    n = w.shape[1]
    return pl.pallas_call(
        _proj_kernel,
        out_shape=jax.ShapeDtypeStruct((m, n), F32),
        grid=(m // tm,),
        in_specs=[pl.BlockSpec((tm, kdim), lambda i: (i, 0)), _const_spec((kdim, n))],
        out_specs=pl.BlockSpec((tm, n), lambda i: (i, 0)),
        compiler_params=_cparams(("parallel",)),
        name="proj",
    )(x, w)


def _rope(x, cos, sin):
    width = x.shape[1]
    lane = lax.broadcasted_iota(jnp.int32, x.shape, 1)
    first_half = (lane % ATT_HD) < (ATT_HD // 2)
    partner = jnp.where(first_half,
                        pltpu.roll(x, width - ATT_HD // 2, axis=1),
                        pltpu.roll(x, ATT_HD // 2, axis=1))
    return x * cos + partner * sin


def _in_proj(x_ref, g_ref, w_ref, cos_ref, sin_ref, conv_dim, att_dim):
    hn = _rms(x_ref[...], g_ref[...]).astype(BF16)
    c0 = conv_dim
    val = jnp.dot(hn, w_ref[:, 0:c0], preferred_element_type=F32)
    gate = jnp.dot(hn, w_ref[:, c0:2 * c0], preferred_element_type=F32)
    u = val * _sigmoid(gate)
    reps = att_dim // LANES
    cos = jnp.concatenate([cos_ref[...]] * reps, axis=1)
    sin = jnp.concatenate([sin_ref[...]] * reps, axis=1)
    o = 2 * c0
    q = _rope(jnp.dot(hn, w_ref[:, o:o + att_dim], preferred_element_type=F32), cos, sin)
    k = _rope(jnp.dot(hn, w_ref[:, o + att_dim:o + 2 * att_dim], preferred_element_type=F32), cos, sin)
    v = jnp.dot(hn, w_ref[:, o + 2 * att_dim:o + 3 * att_dim], preferred_element_type=F32)
    return u, q, k, v


def _front0_step_kernel(x_ref, g_ref, w_ref, cos_ref, sin_ref, u_ref, q_ref, k_ref, v_ref, *, conv_dim, att_dim):
    u_ref[...], q_ref[...], k_ref[...], v_ref[...] = _in_proj(x_ref, g_ref, w_ref, cos_ref, sin_ref,
                                                               conv_dim, att_dim)


def _front0_step(h, g, w_in, cos_tab, sin_tab, conv_dim, att_dim):
    m, d = h.shape
    whole = lambda shape: pl.BlockSpec(shape, lambda i: (0,) * len(shape))
    return pl.pallas_call(
        functools.partial(_front0_step_kernel, conv_dim=conv_dim, att_dim=att_dim),
        out_shape=(jax.ShapeDtypeStruct((m, conv_dim), F32),) + (jax.ShapeDtypeStruct((m, att_dim), F32),) * 3,
        grid=(1,),
        in_specs=[whole((m, d)), _const_spec((1, d)), _const_spec(w_in.shape), whole((m, LANES)), whole((m, LANES))],
        out_specs=(whole((m, conv_dim)),) + (whole((m, att_dim)),) * 3,
        compiler_params=_cparams(("arbitrary",)),
        name="front0_step",
    )(h, g, w_in, cos_tab, sin_tab)


def _front0_seq_kernel(x_ref, g_ref, w_ref, cos_ref, sin_ref, cw_ref, cb_ref, lg_ref, lb_ref,
                       c_ref, ulast_ref, q_ref, k_ref, v_ref, v_scr, full_scr, *, conv_dim, att_dim, tiles_per_seq):
    i = pl.program_id(0)
    tm = x_ref.shape[0]

    @pl.when(i % tiles_per_seq == 0)
    def _():
        full_scr[0:HALO, :] = jnp.zeros((HALO, conv_dim), F32)

    u, q, k, v = _in_proj(x_ref, g_ref, w_ref, cos_ref, sin_ref, conv_dim, att_dim)
    full_scr[HALO:HALO + tm, :] = u
    ulast_ref[0] = u[tm - HALO:, :]
    q_ref[...] = q
    k_ref[0] = k.T
    v_scr[...] = v
    v_ref[0] = v_scr[...].T
    for r0 in range(0, tm, HALO):
        c_ref[r0:r0 + HALO, :] = _conv_block(full_scr, cw_ref, cb_ref, lg_ref, lb_ref, r0, HALO)
    full_scr[0:HALO, :] = full_scr[tm:tm + HALO, :]


def _front0_seq(h, g, w_in, cos_tab, sin_tab, conv_w, conv_b, ln_g, ln_b, att_dim, tm, seq_len):
    m, d = h.shape
    conv_dim = conv_w.shape[1]
    tps = seq_len // tm
    n_tab = cos_tab.shape[0] // tm
    row = lambda i: (i, 0)
    tab = lambda i: (i % n_tab, 0)
    kv_shape = jax.ShapeDtypeStruct((m // seq_len, att_dim, seq_len), F32)
    kv_spec = pl.BlockSpec((1, att_dim, tm), lambda i: (i // tps, 0, i % tps))
    wp = jnp.pad(conv_w, ((0, HALO - CONV_WIDTH), (0, 0)))
    return pl.pallas_call(
        functools.partial(_front0_seq_kernel, conv_dim=conv_dim, att_dim=att_dim, tiles_per_seq=tps),
        out_shape=(jax.ShapeDtypeStruct((m, conv_dim), F32), jax.ShapeDtypeStruct((m // seq_len, HALO, conv_dim), F32),
                   jax.ShapeDtypeStruct((m, att_dim), F32), kv_shape, kv_shape),
        grid=(m // tm,),
        in_specs=[pl.BlockSpec((tm, d), row), _const_spec((1, d)), _const_spec(w_in.shape),
                  pl.BlockSpec((tm, LANES), tab), pl.BlockSpec((tm, LANES), tab),
                  _const_spec((HALO, conv_dim)), _const_spec((1, conv_dim)), _const_spec((1, conv_dim)),
                  _const_spec((1, conv_dim))],
        out_specs=(pl.BlockSpec((tm, conv_dim), row), pl.BlockSpec((1, HALO, conv_dim), lambda i: (i // tps, 0, 0)),
                   pl.BlockSpec((tm, att_dim), row), kv_spec, kv_spec),
        scratch_shapes=[pltpu.VMEM((tm, att_dim), F32), pltpu.VMEM((HALO + tm, conv_dim), F32)],
        compiler_params=_cparams(("arbitrary",)),
        name="front0",
    )(h, g, w_in, cos_tab, sin_tab, wp, conv_b, ln_g, ln_b)


def _rope_tables(pos):
    half = ATT_HD // 2
    inv_freq = ROPE_THETA ** (-jnp.arange(half, dtype=F32) / half)
    ang = pos.astype(F32)[:, None] * inv_freq[None, :]
    cos, sin = jnp.cos(ang), jnp.sin(ang)
    reps = LANES // ATT_HD
    return (jnp.tile(jnp.concatenate([cos, cos], axis=1), (1, reps)),
            jnp.tile(jnp.concatenate([-sin, sin], axis=1), (1, reps)))


def _ln_silu(y, g, b):
    mu = jnp.mean(y, axis=-1, keepdims=True)
    yc = y - mu
    var = jnp.mean(yc * yc, axis=-1, keepdims=True)
    ln = yc * lax.rsqrt(var + LN_EPS) * g + b
    return ln * _sigmoid(ln)


def _conv_block(full_scr, w_ref, b_ref, lg_ref, lb_ref, r0, sub):
    first = HALO - (CONV_WIDTH - 1)
    acc = jnp.broadcast_to(b_ref[...], (sub, b_ref.shape[1]))
    for res in range(SUBLANES):
        rows = sub + (SUBLANES if res else 0)
        z = None
        for j in range(CONV_WIDTH):
            if (first + j) % SUBLANES != res:
                continue
            base = r0 + first + j - res
            term = w_ref[j:j + 1, :] * full_scr[base:base + rows, :]
            z = term if z is None else z + term
        acc = acc + z[res:res + sub, :]
    return _ln_silu(acc, lg_ref[...], lb_ref[...])


def _conv_step_kernel(st_ref, u_ref, w_ref, b_ref, lg_ref, lb_ref, c_ref):
    acc = b_ref[...] + w_ref[CONV_WIDTH - 1:CONV_WIDTH, :] * u_ref[...]
    for j in range(CONV_WIDTH - 1):
        acc = acc + w_ref[j:j + 1, :] * st_ref[j]
    c_ref[...] = _ln_silu(acc, lg_ref[...], lb_ref[...])


def _conv_step(state_t, u, w, b, lg, lb):
    bsz, c = u.shape
    wp = jnp.pad(w, ((0, HALO - CONV_WIDTH), (0, 0)))
    return pl.pallas_call(
        _conv_step_kernel,
        out_shape=jax.ShapeDtypeStruct((bsz, c), F32),
        grid=(1,),
        in_specs=[_const_spec(state_t.shape), _const_spec((bsz, c)), _const_spec((HALO, c)),
                  _const_spec((1, c)), _const_spec((1, c)), _const_spec((1, c))],
        out_specs=pl.BlockSpec((bsz, c), lambda i: (0, 0)),
        compiler_params=_cparams(("arbitrary",)),
        name="conv_step",
    )(state_t, u, wp, b, lg, lb)


def _top_blocks(gate, n_past):
    blk = lax.broadcasted_iota(jnp.int32, gate.shape, 0).astype(F32)
    g = jnp.where(blk < n_past.astype(F32), gate, NEG_INF)
    picks = []
    for r in range(MOBA_TOPK):
        m = jnp.max(g, axis=0, keepdims=True)
        idx = jnp.min(jnp.where(g == m, blk, 1e9), axis=0, keepdims=True)
        picks.append(jnp.where(r < n_past, idx, -1.0))
        g = jnp.where(blk == idx, -jnp.inf, g)
    return picks


def _moba_kernel(q_ref, k_ref, v_ref, o_ref, kmean_scr, kb_scr, vt_scr, s_scr, *, nb, scale):
    i = pl.program_id(2)
    blk = MOBA_BLOCK
    hpp = LANES // ATT_HD

    @pl.when(i == 0)
    def _():
        kmean_scr[...] = jnp.zeros_like(kmean_scr)
        vt_scr[...] = v_ref[0].astype(BF16)
        for j in range(nb):
            rows = slice(j * blk, (j + 1) * blk)
            kj = k_ref[0, :, rows].T
            kmean_scr[j:j + 1, :] = jnp.mean(kj, axis=0, keepdims=True)
            kb_scr[rows, :] = kj.astype(BF16)

    q = q_ref[0] * (scale * LOG2_E)
    lane = lax.broadcasted_iota(jnp.int32, q.shape, 1)
    q2 = jnp.concatenate([jnp.where((lane >= hh * ATT_HD) & (lane < (hh + 1) * ATT_HD), q, 0.0)
                          for hh in range(hpp)], axis=0)
    q_hi, q_lo = _split(q2)
    km_hi, km_lo = _split(kmean_scr[...])
    nbp = km_hi.shape[0]
    nt = (((1,), (1,)), ((), ()))

    def offset(j):
        return pl.multiple_of(jnp.minimum(j, nb - 1) * blk, blk)

    start = offset(i)
    stacked = lax.dot_general(jnp.concatenate([km_hi, km_lo, kb_scr[pl.ds(start, blk), :]], axis=0), q_hi, nt,
                              preferred_element_type=F32)
    gate = (stacked[0:nbp] + stacked[nbp:2 * nbp]
            + lax.dot_general(km_hi, q_lo, nt, preferred_element_type=F32))
    s = stacked[2 * nbp:]
    picks = _top_blocks(gate, i)

    ones = jnp.ones((SUBLANES, blk), BF16)

    def pv(off, p):
        vt_j = vt_scr[:, pl.ds(off, blk)]
        upd = [jnp.dot(vt_j[hh * ATT_HD:(hh + 1) * ATT_HD, :], p[:, hh * blk:(hh + 1) * blk],
                       preferred_element_type=F32) for hh in range(hpp)]
        return upd, jnp.dot(ones, p, preferred_element_type=F32)[0:1, :]

    def scores(off):
        return lax.dot_general(kb_scr[pl.ds(off, blk), :], q_hi, nt, preferred_element_type=F32)

    def past_scores(j):
        jf = j.astype(F32)
        chosen = (picks[0] == jf) | (picks[1] == jf) | (picks[2] == jf)
        return jnp.where(chosen, scores(offset(j)), NEG_INF)

    key = lax.broadcasted_iota(jnp.int32, s.shape, 0)
    qry = lax.broadcasted_iota(jnp.int32, s.shape, 1) % blk
    s = jnp.where(key <= qry, s, NEG_INF)
    s_scr[0] = past_scores(jnp.int32(0))
    m0 = jnp.max(s, axis=0, keepdims=True)
    acc0, l0 = pv(start, jnp.exp2(s - m0).astype(BF16))

    def fold(j, slot, carry):
        m, l, acc = carry
        sj = s_scr[slot]
        m_new = jnp.maximum(m, jnp.max(sj, axis=0, keepdims=True))
        alpha = jnp.exp2(m - m_new)
        upd, psum = pv(offset(j), jnp.exp2(sj - m_new).astype(BF16))
        l = alpha * l + psum
        acc = [alpha[:, hh * blk:(hh + 1) * blk] * acc[hh] + upd[hh] for hh in range(hpp)]
        return m_new, l, acc

    def body(jj, carry):
        j = 2 * jj
        s_scr[1] = past_scores(j + 1)
        carry = fold(j, 0, carry)
        s_scr[0] = past_scores(j + 2)
        return fold(j + 1, 1, carry)

    _, l, acc = lax.fori_loop(0, (i + 1) // 2, body, (m0, l0, acc0))
    out_t = jnp.concatenate([acc[hh] / l[:, hh * blk:(hh + 1) * blk] for hh in range(hpp)], axis=0)
    o_ref[0] = out_t.T


def _moba_prompt(q, kt, vt):
    bsz, s, a = q.shape
    nb = s // MOBA_BLOCK
    blk = pl.BlockSpec((1, MOBA_BLOCK, LANES), lambda b, hp, i: (b, i, hp))
    seq = pl.BlockSpec((1, LANES, s), lambda b, hp, i: (b, hp, 0))
    nb_pad = -(-nb // BF16_ROWS) * BF16_ROWS
    return pl.pallas_call(
        functools.partial(_moba_kernel, nb=nb, scale=ATT_HD ** -0.5),
        out_shape=jax.ShapeDtypeStruct((bsz, s, a), F32),
        grid=(bsz, a // LANES, nb),
        in_specs=[blk, seq, seq],
        out_specs=blk,
        scratch_shapes=[pltpu.VMEM((nb_pad, LANES), F32), pltpu.VMEM((s, LANES), BF16),
                        pltpu.VMEM((LANES, s), BF16),
                        pltpu.VMEM((2, MOBA_BLOCK, (LANES // ATT_HD) * MOBA_BLOCK), F32)],
        compiler_params=_cparams(("parallel", "parallel", "arbitrary")),
        name="moba_prompt",
    )(q, kt, vt)


def _moba_gate_kernel(pt_ref, q_ref, *refs, nb, pages_per_step, pages_per_block):
    del pt_ref
    page_refs = refs[:pages_per_step]
    sel_ref, qb_scr, g_scr = refs[pages_per_step:]
    j = pl.program_id(1)
    blocks_per_step = pages_per_step // pages_per_block

    @pl.when(j == 0)
    def _():
        qb_scr[...] = jnp.broadcast_to(q_ref[0], qb_scr.shape)

    qb = qb_scr[...]
    for bl in range(blocks_per_step):
        ksum = page_refs[bl * pages_per_block][0]
        for pg in range(1, pages_per_block):
            ksum = ksum + page_refs[bl * pages_per_block + pg][0]
        g_scr[j * blocks_per_step + bl] = jnp.sum(ksum * qb, axis=1)

    @pl.when(j == nb // blocks_per_step - 1)
    def _():
        gate = jnp.sum(g_scr[...], axis=2, keepdims=True) * (1.0 / MOBA_BLOCK)
        blk = lax.broadcasted_iota(jnp.int32, gate.shape, 0).astype(F32)
        lane = lax.broadcasted_iota(jnp.int32, sel_ref.shape[1:], 1)
        sel = jnp.zeros(sel_ref.shape[1:], F32)
        for r in range(MOBA_TOPK):
            m = jnp.max(gate, axis=0, keepdims=True)
            idx = jnp.min(jnp.where(gate == m, blk, 1e9), axis=0, keepdims=True)
            sel = jnp.where(lane == r, idx[0], sel)
            gate = jnp.where(blk == idx, -jnp.inf, gate)
        sel_ref[0] = sel.astype(jnp.int32)


def _moba_gate(page_table, q_col, cache_kt, nb, pages_per_block, pages_per_step):
    bsz, heads, hd, _ = q_col.shape
    page = cache_kt.shape[3]
    steps = nb * pages_per_block // pages_per_step

    def page_spec(w):
        return pl.BlockSpec((1, heads, hd, page), lambda b, j, pt: (pt[b, j * pages_per_step + w], 0, 0, 0))

    grid_spec = pltpu.PrefetchScalarGridSpec(
        num_scalar_prefetch=1,
        grid=(bsz, steps),
        in_specs=[pl.BlockSpec((1, heads, hd, 1), lambda b, j, pt: (b, 0, 0, 0))]
        + [page_spec(w) for w in range(pages_per_step)],
        out_specs=pl.BlockSpec((1, heads, LANES), lambda b, j, pt: (b, 0, 0)),
        scratch_shapes=[pltpu.VMEM((heads, hd, page), F32), pltpu.VMEM((nb, heads, page), F32)],
    )
    return pl.pallas_call(
        functools.partial(_moba_gate_kernel, nb=nb, pages_per_step=pages_per_step, pages_per_block=pages_per_block),
        out_shape=jax.ShapeDtypeStruct((bsz, heads, LANES), jnp.int32),
        grid_spec=grid_spec,
        compiler_params=_cparams(("parallel", "arbitrary")),
        name="moba_gate",
    )(page_table, q_col, *([cache_kt] * pages_per_step))


def _moba_decode_kernel(pt_ref, sel_ref, q_ref, kn_ref, vn_ref, *refs, n_pages, heads_per_step, scale):
    del pt_ref, sel_ref
    o_ref = refs[2 * n_pages * heads_per_step]
    for u in range(heads_per_step):
        k_refs = refs[u * n_pages:(u + 1) * n_pages]
        v_refs = refs[(heads_per_step + u) * n_pages:(heads_per_step + u + 1) * n_pages]
        q = q_ref[0, u]
        s_own = jnp.sum(q * kn_ref[0, u], axis=0, keepdims=True) * scale
        s = [jnp.sum(kr[0, 0] * q, axis=0, keepdims=True) * scale for kr in k_refs]
        m = s_own
        for sp in s:
            m = jnp.maximum(m, jnp.max(sp, axis=1, keepdims=True))
        p_own = jnp.exp(s_own - m)
        l = p_own
        acc = p_own * vn_ref[0, u]
        for sp, vr in zip(s, v_refs):
            p = jnp.exp(sp - m)
            l = l + jnp.sum(p, axis=1, keepdims=True)
            acc = acc + jnp.sum(vr[0, 0] * p, axis=1, keepdims=True)
        o_ref[0, u] = acc / l


def _moba_decode(page_table, sel, q_col, kn_col, vn_col, cache_kt, cache_vt, pages_per_block):
    bsz, heads, hd, _ = q_col.shape
    page = cache_kt.shape[3]
    n_pages = MOBA_TOPK * pages_per_block
    hps = _largest_divisor(heads, DECODE_HEADS_PER_STEP)
    vec = pl.BlockSpec((1, hps, hd, 1), lambda b, hg, pt, sl: (b, hg, 0, 0))

    def page_spec(u, r, w):
        return pl.BlockSpec(
            (1, 1, hd, page),
            lambda b, hg, pt, sl: (pt[b, pages_per_block * sl[b, hg * hps + u, r] + w], hg * hps + u, 0, 0))

    pages = [page_spec(u, r, w) for u in range(hps) for r in range(MOBA_TOPK) for w in range(pages_per_block)]
    grid_spec = pltpu.PrefetchScalarGridSpec(
        num_scalar_prefetch=2,
        grid=(bsz, heads // hps),
        in_specs=[vec, vec, vec] + pages + pages,
        out_specs=vec,
    )
    return pl.pallas_call(
        functools.partial(_moba_decode_kernel, n_pages=n_pages, heads_per_step=hps, scale=ATT_HD ** -0.5),
        out_shape=jax.ShapeDtypeStruct((bsz, heads, hd, 1), F32),
        grid_spec=grid_spec,
        compiler_params=_cparams(("parallel", "parallel")),
        name="moba_decode",
    )(page_table, sel, q_col, kn_col, vn_col, *([cache_kt] * len(pages)), *([cache_vt] * len(pages)))


def _tail_kernel(*refs, n_mix, nb, s, ff_chunk):
    h_ref = refs[0]
    mix_refs = refs[1:1 + n_mix]
    wmix_refs = refs[1 + n_mix:1 + 2 * n_mix]
    g_ref, wcq_ref, mk_ref, mv_ref, wco_ref, wup_ref, wdn_ref, o_ref, ca_scr = refs[1 + 2 * n_mix:]

    mix = _mm(mix_refs[0][...], wmix_refs[0][...])
    for mr, wr in zip(mix_refs[1:], wmix_refs[1:]):
        mix = mix + _mm(mr[...], wr[...])
    h = h_ref[...] + _rms(mix, g_ref[1:2, :])

    q = _mm(_rms(h, g_ref[2:3, :]), wcq_ref[...])
    rows = max(s, SUBLANES)
    mem = mk_ref.shape[1] // MEM_HEADS
    for bi in range(nb):
        qb = q[bi * s:(bi + 1) * s, :]
        if s < rows:
            qb = jnp.broadcast_to(qb[0:1, :], (rows, qb.shape[1]))
        for hh in range(MEM_HEADS):
            sl = slice(hh * MEM_HD, (hh + 1) * MEM_HD)
            tokens = pl.ds(hh, mem, stride=MEM_HEADS)
            sc = _mm_nt(qb[:, sl], mk_ref[bi, tokens, :]) * (MEM_HD ** -0.5)
            p = jnp.exp(sc - jnp.max(sc, axis=1, keepdims=True))
            oh = _mm(p, mv_ref[bi, tokens, :]) / jnp.sum(p, axis=1, keepdims=True)
            ca_scr[bi * s:(bi + 1) * s, sl] = oh[0:s, :]
    h = h + _rms(_mm(ca_scr[...], wco_ref[...]), g_ref[3:4, :])

    hn = _rms(h, g_ref[4:5, :]).astype(BF16)
    d_ff = wup_ref.shape[1]
    acc = jnp.zeros(h.shape, F32)
    for c0 in range(0, d_ff, ff_chunk):
        up = jnp.dot(hn, wup_ref[:, c0:c0 + ff_chunk], preferred_element_type=F32)
        up = jnp.square(jnp.maximum(up, 0.0))
        acc = acc + jnp.dot(up.astype(BF16), wdn_ref[c0:c0 + ff_chunk, :], preferred_element_type=F32)
    o_ref[...] = h + _rms(acc, g_ref[5:6, :])


def _tail(h, mixes, w_mixes, gains, w_cq, mk, mv, w_co, w_up, w_down, tm, seq_len, layer=0):
    m, d = h.shape
    s = min(seq_len, tm)
    nb = tm // s
    n_mix = len(mixes)
    mem_rows = mk.shape[-2]
    mem_dim = MEM_HEADS * MEM_HD
    row = lambda i: (i, 0)
    tiles_per_seq = max(seq_len // tm, 1)
    if mk.ndim == 4:
        mem_spec = pl.BlockSpec((None, nb, mem_rows, MEM_HD), lambda i: (layer, i // tiles_per_seq, 0, 0))
    else:
        mem_spec = pl.BlockSpec((nb, mem_rows, MEM_HD), lambda i: (i // tiles_per_seq, 0, 0))
    in_specs = [pl.BlockSpec((tm, d), row)]
    in_specs += [pl.BlockSpec((tm, x.shape[1]), row) for x in mixes]
    in_specs += [_const_spec(w.shape) for w in w_mixes]
    in_specs += [_const_spec(gains.shape), _const_spec(w_cq.shape),
                 mem_spec, mem_spec,
                 _const_spec(w_co.shape), _const_spec(w_up.shape), _const_spec(w_down.shape)]
    return pl.pallas_call(
        functools.partial(_tail_kernel, n_mix=n_mix, nb=nb, s=s, ff_chunk=min(1024, w_up.shape[1])),
        out_shape=jax.ShapeDtypeStruct((m, d), F32),
        grid=(m // tm,),
        in_specs=in_specs,
        out_specs=pl.BlockSpec((tm, d), row),
        scratch_shapes=[pltpu.VMEM((tm, mem_dim), F32)],
        compiler_params=_cparams(("parallel",)),
        name="tail",
    )(h, *mixes, *w_mixes, gains, w_cq, mk, mv, w_co, w_up, w_down)


def _front1_kernel(h_ref, hprev_ref, xlast_ref, g_ref, mu_ref, wr_ref, wk_ref, wv_ref, w0_ref, w1_ref, w2_ref,
                   a0_ref, a1_ref, a2_ref, g1_ref, g2_ref, kk_ref, ka_ref, seg_ref,
                   r_out, ld_out, k_out, v_out, kk_out, a_out, gt_out, sh_out, *, seq, tiles_per_seq):
    i = pl.program_id(0)
    g = g_ref[...]
    hn = _rms(h_ref[...], g)
    if seq:
        prev_row = _rms(hprev_ref[SUBLANES - 1:SUBLANES, :], g)
        prev_row = jnp.where(i % tiles_per_seq == 0, xlast_ref[0], prev_row)
        rowid = lax.broadcasted_iota(jnp.int32, hn.shape, 0)
        xprev = jnp.where(rowid == 0, prev_row, pltpu.roll(hn, 1, axis=0))
        sh_out[0] = hn[hn.shape[0] - 1:, :]
    else:
        xprev = xlast_ref[...]
        sh_out[...] = hn
    xx = xprev - hn
    mix = lambda n: hn + xx * mu_ref[n:n + 1, :]
    r = _mm(mix(0), wr_ref[...])
    k = _mm(mix(2), wk_ref[...])
    v = _mm(mix(3), wv_ref[...])
    wl = w0_ref[...] + _mm(jnp.tanh(_mm(mix(1), w1_ref[...])), w2_ref[...])
    z = -wl
    softplus = jnp.maximum(z, 0.0) + jnp.log(1.0 + jnp.exp(-jnp.abs(z)))
    ld_out[...] = -jnp.exp(-softplus - 0.5)
    a = _sigmoid(a0_ref[...] + _mm(_mm(mix(4), a1_ref[...]), a2_ref[...]))
    gt_out[...] = _mm(_sigmoid(_mm(mix(5), g1_ref[...])), g2_ref[...])
    kk = k * kk_ref[...]
    ss = _mm2(kk * kk, seg_ref[...])
    kk_out[...] = kk / jnp.maximum(jnp.sqrt(ss), 1e-12)
    k_out[...] = k * (1.0 + (a - 1.0) * ka_ref[...])
    r_out[...] = r
    v_out[...] = v
    a_out[...] = a


def _front1(h, x_last, g, p, tm, s):
    m, d = h.shape
    bsz = m // s
    seq = s > 1
    row = lambda i: (i, 0)
    big = jax.ShapeDtypeStruct((m, d), F32)
    if seq:
        tps = s // tm
        per = tm // SUBLANES
        hprev_spec = pl.BlockSpec((SUBLANES, d), lambda i: (jnp.maximum(i * per - 1, 0), 0))
        xlast = x_last.reshape(bsz, 1, d)
        xlast_spec = pl.BlockSpec((1, 1, d), lambda i: (i // tps, 0, 0))
        sh_shape = jax.ShapeDtypeStruct((bsz, 1, d), F32)
        sh_spec = pl.BlockSpec((1, 1, d), lambda i: (i // tps, 0, 0))
        sem = ("arbitrary",)
    else:
        tps = 1
        hprev_spec = pl.BlockSpec((SUBLANES, d), lambda i: (0, 0))
        xlast = x_last
        xlast_spec = pl.BlockSpec((tm, d), row)
        sh_shape = big
        sh_spec = pl.BlockSpec((tm, d), row)
        sem = ("parallel",)
    consts = [g, p["mu"], p["w_r"], p["w_k"], p["w_v"], p["w0"], p["w1"], p["w2"], p["a0"], p["a1"], p["a2"],
              p["g1"], p["g2"], p["k_k"], p["k_a"], p["seg"]]
    outs = pl.pallas_call(
        functools.partial(_front1_kernel, seq=seq, tiles_per_seq=tps),
        out_shape=(big,) * 7 + (sh_shape,),
        grid=(m // tm,),
        in_specs=[pl.BlockSpec((tm, d), row), hprev_spec, xlast_spec] + [_const_spec(c.shape) for c in consts],
        out_specs=(pl.BlockSpec((tm, d), row),) * 7 + (sh_spec,),
        compiler_params=_cparams(sem),
        name="front1",
    )(h, h, xlast, *consts)
    return outs[:7], outs[7].reshape(bsz, d)


def _scan_kernel(r_ref, ld_ref, k_ref, v_ref, kk_ref, a_ref, gt_ref, s0_ref, rk_ref, lg_ref, lb_ref,
                 o_ref, st_ref, s_scr, *, n_chunks, n_groups, n_seq):
    c = pl.program_id(1)
    grp = SCAN_GROUP
    w = grp * RWKV_HD
    sbd = (lax.broadcasted_iota(jnp.int32, (w, w), 0) // RWKV_HD
           == lax.broadcasted_iota(jnp.int32, (w, w), 1) // RWKV_HD)
    chains = [(bb, q) for bb in range(n_seq) for q in range(n_groups)]

    @pl.when(c == 0)
    def _():
        for g, (bb, q) in enumerate(chains):
            s0 = s0_ref[bb, q]
            s_scr[g] = jnp.where(sbd, jnp.concatenate([s0] * grp, axis=1), 0.0)

    seqs = lambda ref: jnp.concatenate([ref[bb] for bb in range(n_seq)], axis=1)
    rep = lambda ref: jnp.concatenate([ref[...]] * n_seq, axis=1)

    def write(g, val):
        bb, q = chains[g]
        o_ref[bb, :, q * w:(q + 1) * w] = val

    _scan_chunk(seqs(r_ref), seqs(ld_ref), seqs(k_ref), seqs(v_ref), seqs(kk_ref), seqs(a_ref), seqs(gt_ref),
                rep(rk_ref), rep(lg_ref), rep(lb_ref), write, s_scr, len(chains))

    @pl.when(c == n_chunks - 1)
    def _():
        for g, (bb, q) in enumerate(chains):
            full = s_scr[g]
            acc = full[:, 0:RWKV_HD]
            for hh in range(1, grp):
                acc = acc + full[:, hh * RWKV_HD:(hh + 1) * RWKV_HD]
            st_ref[bb, q] = acc


def _scan_chunk(r, ld, k, v, kk, a_gate, gt, r_k, lnx_g, lnx_b, write, s_scr, n_groups):
    ch = SCAN_CHUNK
    grp = SCAN_GROUP
    w = grp * RWKV_HD
    rows = grp * ch
    gs = range(n_groups)
    rhead = lax.broadcasted_iota(jnp.int32, (rows, w), 0) // ch
    chead = lax.broadcasted_iota(jnp.int32, (rows, w), 1) // RWKV_HD
    bd = rhead == chead
    tile_r = lambda x: jnp.concatenate([x] * grp, axis=0)
    diag = lambda x: jnp.where(bd, tile_r(x), 0.0)
    part = lambda x, q: x[:, q * w:(q + 1) * w]

    b = kk * a_gate
    trow = lax.broadcasted_iota(jnp.int32, ld.shape, 0)
    cum = ld
    sh = 1
    while sh < ch:
        cum = cum + jnp.where(trow >= sh, pltpu.roll(cum, sh, axis=0), 0.0)
        sh *= 2
    clast = cum[ch - 1:ch, :]
    p_inv = jnp.exp(-cum)
    p_to = jnp.exp(clast - cum)
    rt = r * jnp.exp(cum)
    at = -kk * jnp.exp(cum - ld)
    bt = b * p_inv
    kt = k * p_inv
    bh = b * p_to
    kh = k * p_to
    p_end = jnp.exp(clast)

    trow2 = lax.broadcasted_iota(jnp.int32, (ch, w), 0)
    tcol2 = lax.broadcasted_iota(jnp.int32, (ch, w), 1) % ch
    strict = tcol2 < trow2
    incl = tcol2 <= trow2
    eye = jnp.where(tcol2 == trow2, 1.0, 0.0)
    strict2 = jnp.concatenate([strict, strict], axis=1)
    incl2 = jnp.concatenate([incl, incl], axis=1)

    lhs = [jnp.concatenate([part(at, q), part(rt, q)], axis=0).astype(BF16) for q in gs]
    rhs = [jnp.concatenate([diag(part(bt, q)), diag(part(kt, q))], axis=0) for q in gs]
    mall = [_mm_nt(lhs[q], rhs[q]) for q in gs]
    la = [jnp.where(strict2, mall[q][0:ch], 0.0) for q in gs]
    lr = [jnp.where(incl2, mall[q][ch:], 0.0) for q in gs]
    lp = [la[q][:, 0:rows] for q in gs]
    mkk = [jnp.concatenate([la[q][:, rows:], lr[q][:, rows:]], axis=0) for q in gs]
    mrb = [lr[q][:, 0:rows] for q in gs]

    tinv = [eye + lp[q] for q in gs]
    lp = [_mm(lp[q], diag(lp[q])) for q in gs]
    n = 4
    while n < ch:
        both = [_mm(jnp.concatenate([lp[q], tinv[q]], axis=0), diag(lp[q])) for q in gs]
        lp = [both[q][0:ch] for q in gs]
        tinv = [tinv[q] + both[q][ch:] for q in gs]
        n *= 2
    tinv = [tinv[q] + _mm(tinv[q], diag(lp[q])) for q in gs]

    vd = [diag(part(v, q)).astype(BF16) for q in gs]
    state = [s_scr[q] for q in gs]
    s0t = [_mm_nt(lhs[q], state[q]) for q in gs]
    mv = [_mm(mkk[q], vd[q]) for q in gs]
    wmat = [s0t[q][0:ch] + mv[q][0:ch] for q in gs]
    u = [_mm(tinv[q], diag(wmat[q])) for q in gs]
    ys = [s0t[q][ch:] + _mm(mrb[q], diag(u[q])) + mv[q][ch:] for q in gs]
    upd = [_mm_tn(jnp.concatenate([u[q], part(v, q)], axis=0),
                  jnp.concatenate([part(bh, q), part(kh, q)], axis=0)) for q in gs]
    sbd = (lax.broadcasted_iota(jnp.int32, (w, w), 0) // RWKV_HD
           == lax.broadcasted_iota(jnp.int32, (w, w), 1) // RWKV_HD)
    for q in gs:
        s_scr[q] = state[q] * part(p_end, q) + jnp.where(sbd, upd[q], 0.0)

    seg = jnp.where(sbd, 1.0, 0.0).astype(BF16)
    rkr = r * k * r_k
    for q in gs:
        y = ys[q]
        mean = _mm2(y, seg) * (1.0 / RWKV_HD)
        yc = y - mean
        var = _mm2(yc * yc, seg) * (1.0 / RWKV_HD)
        yn = yc * lax.rsqrt(var + GN_EPS) * part(lnx_g, q) + part(lnx_b, q)
        bonus = _mm2(part(rkr, q), seg) * part(v, q)
        write(q, (yn + bonus) * part(gt, q))


def _scan(streams, gate, s0, r_k, lnx_g, lnx_b):
    bsz, t, d = gate.shape
    w = SCAN_GROUP * RWKV_HD
    nq = d // w
    nc = t // SCAN_CHUNK
    n_seq = _largest_divisor(bsz, SCAN_SEQS_PER_STEP)
    tok = pl.BlockSpec((n_seq, SCAN_CHUNK, d), lambda b, c: (b, c, 0))
    st = pl.BlockSpec((n_seq, nq, w, RWKV_HD), lambda b, c: (b, 0, 0, 0))
    s0g = s0.reshape(bsz, nq, w, RWKV_HD)
    out, s_t = pl.pallas_call(
        functools.partial(_scan_kernel, n_chunks=nc, n_groups=nq, n_seq=n_seq),
        out_shape=(jax.ShapeDtypeStruct((bsz, t, d), F32), jax.ShapeDtypeStruct(s0g.shape, F32)),
        grid=(bsz // n_seq, nc),
        in_specs=[tok] * 7 + [st, _const_spec((1, d)), _const_spec((1, d)), _const_spec((1, d))],
        out_specs=(tok, st),
        scratch_shapes=[pltpu.VMEM((n_seq * nq, w, w), F32)],
        compiler_params=_cparams(("parallel", "arbitrary")),
        name="rwkv_scan",
    )(*streams, gate, s0g, r_k, lnx_g, lnx_b)
    return out, s_t.reshape(s0.shape)


def _wkv_step_kernel(r_ref, ld_ref, k_ref, kk_ref, a_ref, v_ref, gt_ref, s_ref, rk_ref, lg_ref, lb_ref,
                     o_ref, st_ref):
    r, k, kk = r_ref[0], k_ref[0], kk_ref[0]
    v = v_ref[0]
    state = s_ref[0]
    sa = -jnp.sum(state * kk, axis=2, keepdims=True)
    state = state * jnp.exp(ld_ref[0]) + sa * (kk * a_ref[0]) + v * k
    st_ref[0] = state
    y = jnp.sum(state * r, axis=2, keepdims=True)
    mean = jnp.mean(y, axis=1, keepdims=True)
    yc = y - mean
    var = jnp.mean(yc * yc, axis=1, keepdims=True)
    yn = yc * lax.rsqrt(var + GN_EPS) * lg_ref[...] + lb_ref[...]
    bonus = jnp.sum(r * k * rk_ref[...], axis=2, keepdims=True) * v
    o_ref[0] = (yn + bonus) * gt_ref[0]


def _wkv_step(streams, gate, s0, r_k, lnx_g, lnx_b):
    r, ld, k, v, kk, a = streams
    bsz, d = r.shape
    n = RWKV_HD
    h = d // n
    row = lambda x: x.reshape(-1, h, 1, n)
    col = lambda x: x.reshape(-1, h, n, 1)
    row_spec = pl.BlockSpec((1, h, 1, n), lambda b: (b, 0, 0, 0))
    col_spec = pl.BlockSpec((1, h, n, 1), lambda b: (b, 0, 0, 0))
    st_spec = pl.BlockSpec((1, h, n, n), lambda b: (b, 0, 0, 0))
    out, s_t = pl.pallas_call(
        _wkv_step_kernel,
        out_shape=(jax.ShapeDtypeStruct((bsz, h, n, 1), F32), jax.ShapeDtypeStruct(s0.shape, F32)),
        grid=(bsz,),
        in_specs=[row_spec] * 5 + [col_spec] * 2 + [st_spec, _const_spec((h, 1, n)), _const_spec((h, n, 1)),
                                                   _const_spec((h, n, 1))],
        out_specs=(col_spec, st_spec),
        compiler_params=_cparams(("parallel",)),
        name="wkv_step",
    )(row(r), row(ld), row(k), row(kk), row(a), col(v), col(gate), s0,
      r_k.reshape(h, 1, n), lnx_g.reshape(h, n, 1), lnx_b.reshape(h, n, 1))
    return out.reshape(bsz, d), s_t


def _row_tile(m, cap):
    t = min(m, cap)
    assert m % t == 0
    return t


def _largest_divisor(n, cap):
    return max(t for t in range(1, cap + 1) if n % t == 0)


def _pad_cols(w, to):
    return jnp.pad(w, ((0, 0), (0, to - w.shape[1])))


def _pad_rows(w, to):
    return jnp.pad(w, ((0, to - w.shape[0]), (0, 0)))


def kernel(x_prompt, x_sample, mem_prompt, cache_k, cache_v, page_table, cache_mem_k, cache_mem_v, state_conv, state_shift, state_wkv, norm_gain, w_in, conv_w, conv_b, conv_ln_g, conv_ln_b, w_out, rwkv_mu, rwkv_w_r, rwkv_w_k, rwkv_w_v, rwkv_w_o, rwkv_w0, rwkv_w1, rwkv_w2, rwkv_a0, rwkv_a1, rwkv_a2, rwkv_g1, rwkv_g2, rwkv_k_k, rwkv_k_a, rwkv_r_k, rwkv_lnx_g, rwkv_lnx_b, w_cq, w_ck, w_cv, w_co, w_up, w_down):
    bp, sp, d = x_prompt.shape
    bs, ss, _ = x_sample.shape
    assert ss == 1
    depth = norm_gain.shape[0]
    conv_dim = conv_w.shape[2]
    att_dim = (w_in.shape[2] - 2 * conv_dim) // 3
    mem = mem_prompt.shape[1]
    mem_dim = w_ck.shape[2]
    n_pages, page = page_table.shape[1], cache_k.shape[2]
    past_len = n_pages * page
    pages_per_block = MOBA_BLOCK // page
    att_heads = att_dim // ATT_HD
    nb_past = past_len // MOBA_BLOCK
    assert past_len % MOBA_BLOCK == 0 and nb_past >= MOBA_TOPK and sp % MOBA_BLOCK == 0
    heads = d // RWKV_HD

    tm_p = _row_tile(sp, 512)
    tm_s = bs
    bf = lambda x: x.astype(BF16)

    w_mem = bf(jnp.concatenate([w for l in range(depth) for w in (w_ck[l], w_cv[l])], axis=1))
    memkv = _proj(mem_prompt.reshape(bp * mem, d), w_mem, _row_tile(bp * mem, 512))
    memkv = memkv.reshape(bp, mem, depth, 2, mem_dim)
    interleave = lambda x: x.reshape(x.shape[0], mem * MEM_HEADS, MEM_HD)
    mem_k_p = [interleave(memkv[:, :, l, 0]) for l in range(depth)]
    mem_v_p = [interleave(memkv[:, :, l, 1]) for l in range(depth)]

    cos_p, sin_p = _rope_tables(jnp.arange(sp, dtype=jnp.int32))
    cos_s, sin_s = _rope_tables(jnp.full((tm_s,), past_len, dtype=jnp.int32))

    hp = x_prompt.reshape(bp * sp, d)
    hs = x_sample.reshape(bs, d)
    k_p, v_p, k_s, v_s, conv_p, conv_s, shift_p, shift_s, wkv_p, wkv_s = ([] for _ in range(10))

    for l in range(depth):
        gains = norm_gain[l]
        g0 = gains[0:1]
        tail_w = (gains, bf(w_cq[l]))
        tail_w2 = (bf(w_co[l]), bf(w_up[l]), bf(w_down[l]))
        mk_s = cache_mem_k.reshape(depth, bs, mem * MEM_HEADS, MEM_HD)
        mv_s = cache_mem_v.reshape(depth, bs, mem * MEM_HEADS, MEM_HD)
        if l % 2 == 0:
            e = l // 2
            w_in_b = bf(w_in[e])
            w_o = bf(w_out[e])
            cb, lg, lb = conv_b[e][None], conv_ln_g[e][None], conv_ln_b[e][None]

            c, u_last, q, kt, vt = _front0_seq(hp, g0, w_in_b, cos_p, sin_p, conv_w[e], cb, lg, lb, att_dim, tm_p, sp)
            att = _moba_prompt(q.reshape(bp, sp, att_dim), kt, vt)
            hp = _tail(hp, [c, att.reshape(bp * sp, att_dim)],
                       [w_o[:conv_dim], w_o[conv_dim:]], *tail_w, mem_k_p[l], mem_v_p[l], *tail_w2, tm_p, sp)
            token_major = lambda x: jnp.transpose(x.reshape(bp, att_heads, ATT_HD, sp), (0, 3, 1, 2))
            k_p.append(token_major(kt))
            v_p.append(token_major(vt))
            conv_p.append(u_last[:, HALO - (CONV_WIDTH - 1):])

            u, q, k, v = _front0_step(hs, g0, w_in_b, cos_s, sin_s, conv_dim, att_dim)
            c = _conv_step(jnp.swapaxes(state_conv[e], 0, 1), u, conv_w[e], cb, lg, lb)
            ckt = jnp.transpose(cache_k[e], (0, 2, 3, 1))
            cvt = jnp.transpose(cache_v[e], (0, 2, 3, 1))
            col = lambda x: x.reshape(bs, att_heads, ATT_HD, 1)
            pages_per_step = pages_per_block * _largest_divisor(nb_past, GATE_BLOCKS_PER_STEP)
            sel = _moba_gate(page_table, col(q), ckt, nb_past, pages_per_block, pages_per_step)
            att = _moba_decode(page_table, sel[:, :, :MOBA_TOPK], col(q), col(k), col(v), ckt, cvt,
                               pages_per_block).reshape(bs, att_dim)
            hs = _tail(hs, [c, att], [w_o[:conv_dim], w_o[conv_dim:]], *tail_w, mk_s, mv_s, *tail_w2, tm_s, 1, l)
            k_s.append(k.reshape(bs, 1, att_dim // ATT_HD, ATT_HD))
            v_s.append(v.reshape(bs, 1, att_dim // ATT_HD, ATT_HD))
            conv_s.append(jnp.concatenate([state_conv[e][:, 1:], u[:, None]], axis=1))
        else:
            o = l // 2
            lora = LANES
            gl = MXU_DIM
            feat = jnp.arange(d) // RWKV_HD
            prm = dict(
                mu=rwkv_mu[o], w_r=bf(rwkv_w_r[o]), w_k=bf(rwkv_w_k[o]), w_v=bf(rwkv_w_v[o]),
                w0=rwkv_w0[o][None], w1=bf(_pad_cols(rwkv_w1[o], lora)), w2=bf(_pad_rows(rwkv_w2[o], lora)),
                a0=rwkv_a0[o][None], a1=bf(_pad_cols(rwkv_a1[o], lora)), a2=bf(_pad_rows(rwkv_a2[o], lora)),
                g1=bf(_pad_cols(rwkv_g1[o], gl)), g2=bf(_pad_rows(rwkv_g2[o], gl)),
                k_k=rwkv_k_k[o][None], k_a=rwkv_k_a[o][None],
                seg=(feat[:, None] == feat[None, :]).astype(BF16))
            r_k = rwkv_r_k[o].reshape(1, d)
            lnx_g, lnx_b = rwkv_lnx_g[o][None], rwkv_lnx_b[o][None]
            w_o = bf(rwkv_w_o[o])

            streams, shift = _front1(hp, jnp.zeros((bp, d), F32), g0, prm, tm_p, sp)
            to3 = lambda x: x.reshape(bp, sp, d)
            gated, s_t = _scan([to3(x) for x in streams[:6]], to3(streams[6]),
                               jnp.zeros((bp, heads, RWKV_HD, RWKV_HD), F32), r_k, lnx_g, lnx_b)
            hp = _tail(hp, [gated.reshape(bp * sp, d)], [w_o], *tail_w, mem_k_p[l], mem_v_p[l], *tail_w2, tm_p, sp)
            shift_p.append(shift)
            wkv_p.append(s_t)

            streams, shift = _front1(hs, state_shift[o], g0, prm, tm_s, 1)
            gated, s_t = _wkv_step(streams[:6], streams[6], state_wkv[o], r_k, lnx_g, lnx_b)
            hs = _tail(hs, [gated], [w_o], *tail_w, mk_s, mv_s, *tail_w2, tm_s, 1, l)
            shift_s.append(shift)
            wkv_s.append(s_t)

    mem_shape = (bp, mem, MEM_HEADS, mem_dim // MEM_HEADS)
    return (hp.reshape(bp, sp, d), hs.reshape(bs, 1, d),
            jnp.stack(k_p), jnp.stack(v_p), jnp.stack(k_s), jnp.stack(v_s),
            jnp.stack(conv_p), jnp.stack(conv_s), jnp.stack(shift_p), jnp.stack(shift_s),
            jnp.stack(wkv_p), jnp.stack(wkv_s),
            jnp.stack([x.reshape(mem_shape) for x in mem_k_p]), jnp.stack([x.reshape(mem_shape) for x in mem_v_p]))
```

```python
import functools

import jax
import jax.numpy as jnp
from jax import lax
from jax.experimental import pallas as pl
from jax.experimental.pallas import tpu as pltpu

F32 = jnp.float32
BF16 = jnp.bfloat16

NORM_EPS = 1e-6
LN_EPS = 1e-5
GN_EPS = 64e-5
NEG_INF = -1e30
ATT_HD = 64
MEM_HEADS = 4
MEM_HD = 128
RWKV_HD = 64
MOBA_BLOCK = 256
MOBA_TOPK = 3
CONV_WIDTH = 31
ROPE_THETA = 10000.0

LANES = 128
SUBLANES = 8
BF16_ROWS = 16
MXU_DIM = 256
VMEM_LIMIT_BYTES = 56 * 1024 * 1024
SCAN_CHUNK = RWKV_HD
SCAN_GROUP = MXU_DIM // RWKV_HD
HALO = 32
LOG2_E = 1.4426950408889634
SCAN_SEQS_PER_STEP = 4
MOBA_PAIRS_PER_STEP = 2
DECODE_HEADS_PER_STEP = 4
GATE_BLOCKS_PER_STEP = 16


def _cparams(sem):
    return pltpu.CompilerParams(dimension_semantics=sem, vmem_limit_bytes=VMEM_LIMIT_BYTES)


def _const_spec(shape):
    nd = len(shape)
    return pl.BlockSpec(shape, lambda *_: (0,) * nd, pipeline_mode=pl.Buffered(1))


def _rms(x, g):
    return x * lax.rsqrt(jnp.mean(x * x, axis=-1, keepdims=True) + NORM_EPS) * g


def _sigmoid(x):
    return 1.0 / (1.0 + jnp.exp(-x))


def _mm(a, b):
    return jnp.dot(a.astype(BF16), b.astype(BF16), preferred_element_type=F32)


def _mm_nt(a, b):
    return lax.dot_general(a.astype(BF16), b.astype(BF16), (((1,), (1,)), ((), ())),
                           preferred_element_type=F32)


def _mm_tn(a, b):
    return lax.dot_general(a.astype(BF16), b.astype(BF16), (((0,), (0,)), ((), ())),
                           preferred_element_type=F32)


def _split(x):
    hi = x.astype(BF16)
    lo = (x - hi.astype(F32)).astype(BF16)
    return hi, lo


def _mm2(a, b):
    hi, lo = _split(a)
    return (jnp.dot(hi, b, preferred_element_type=F32)
            + jnp.dot(lo, b, preferred_element_type=F32))


def _proj_kernel(x_ref, w_ref, o_ref):
    o_ref[...] = _mm(x_ref[...], w_ref[...])


def _proj(x, w, tm):
    m, kdim = x.shape
    n = w.shape[1]
    return pl.pallas_call(
        _proj_kernel,
        out_shape=jax.ShapeDtypeStruct((m, n), F32),
        grid=(m // tm,),
        in_specs=[pl.BlockSpec((tm, kdim), lambda i: (i, 0)), _const_spec((kdim, n))],
        out_specs=pl.BlockSpec((tm, n), lambda i: (i, 0)),
        compiler_params=_cparams(("parallel",)),
        name="proj",
    )(x, w)


def _rope(x, cos, sin):
    width = x.shape[1]
    lane = lax.broadcasted_iota(jnp.int32, x.shape, 1)
    first_half = (lane % ATT_HD) < (ATT_HD // 2)
    partner = jnp.where(first_half,
                        pltpu.roll(x, width - ATT_HD // 2, axis=1),
                        pltpu.roll(x, ATT_HD // 2, axis=1))
    return x * cos + partner * sin


def _in_proj(x_ref, g_ref, w_ref, cos_ref, sin_ref, conv_dim, att_dim):
    hn = _rms(x_ref[...], g_ref[...]).astype(BF16)
    c0 = conv_dim
    val = jnp.dot(hn, w_ref[:, 0:c0], preferred_element_type=F32)
    gate = jnp.dot(hn, w_ref[:, c0:2 * c0], preferred_element_type=F32)
    u = val * _sigmoid(gate)
    reps = att_dim // LANES
    cos = jnp.concatenate([cos_ref[...]] * reps, axis=1)
    sin = jnp.concatenate([sin_ref[...]] * reps, axis=1)
    o = 2 * c0
    q = _rope(jnp.dot(hn, w_ref[:, o:o + att_dim], preferred_element_type=F32), cos, sin)
    k = _rope(jnp.dot(hn, w_ref[:, o + att_dim:o + 2 * att_dim], preferred_element_type=F32), cos, sin)
    v = jnp.dot(hn, w_ref[:, o + 2 * att_dim:o + 3 * att_dim], preferred_element_type=F32)
    return u, q, k, v


def _front0_step_kernel(x_ref, g_ref, w_ref, cos_ref, sin_ref, u_ref, q_ref, k_ref, v_ref, *, conv_dim, att_dim):
    u_ref[...], q_ref[...], k_ref[...], v_ref[...] = _in_proj(x_ref, g_ref, w_ref, cos_ref, sin_ref,
                                                               conv_dim, att_dim)


def _front0_step(h, g, w_in, cos_tab, sin_tab, conv_dim, att_dim):
    m, d = h.shape
    whole = lambda shape: pl.BlockSpec(shape, lambda i: (0,) * len(shape))
    return pl.pallas_call(
        functools.partial(_front0_step_kernel, conv_dim=conv_dim, att_dim=att_dim),
        out_shape=(jax.ShapeDtypeStruct((m, conv_dim), F32),) + (jax.ShapeDtypeStruct((m, att_dim), F32),) * 3,
        grid=(1,),
        in_specs=[whole((m, d)), _const_spec((1, d)), _const_spec(w_in.shape), whole((m, LANES)), whole((m, LANES))],
        out_specs=(whole((m, conv_dim)),) + (whole((m, att_dim)),) * 3,
        compiler_params=_cparams(("arbitrary",)),
        name="front0_step",
    )(h, g, w_in, cos_tab, sin_tab)


def _front0_seq_kernel(x_ref, g_ref, w_ref, cos_ref, sin_ref, cw_ref, cb_ref, lg_ref, lb_ref,
                       c_ref, ulast_ref, q_ref, k_ref, v_ref, v_scr, full_scr, *, conv_dim, att_dim, tiles_per_seq):
    i = pl.program_id(0)
    tm = x_ref.shape[0]

    @pl.when(i % tiles_per_seq == 0)
    def _():
        full_scr[0:HALO, :] = jnp.zeros((HALO, conv_dim), F32)

    u, q, k, v = _in_proj(x_ref, g_ref, w_ref, cos_ref, sin_ref, conv_dim, att_dim)
    full_scr[HALO:HALO + tm, :] = u
    ulast_ref[0] = u[tm - HALO:, :]
    q_ref[...] = q
    k_ref[0] = k.T
    v_scr[...] = v
    v_ref[0] = v_scr[...].T
    for r0 in range(0, tm, HALO):
        c_ref[r0:r0 + HALO, :] = _conv_block(full_scr, cw_ref, cb_ref, lg_ref, lb_ref, r0, HALO)
    full_scr[0:HALO, :] = full_scr[tm:tm + HALO, :]


def _front0_seq(h, g, w_in, cos_tab, sin_tab, conv_w, conv_b, ln_g, ln_b, att_dim, tm, seq_len):
    m, d = h.shape
    conv_dim = conv_w.shape[1]
    tps = seq_len // tm
    n_tab = cos_tab.shape[0] // tm
    row = lambda i: (i, 0)
    tab = lambda i: (i % n_tab, 0)
    kv_shape = jax.ShapeDtypeStruct((m // seq_len, att_dim, seq_len), F32)
    kv_spec = pl.BlockSpec((1, att_dim, tm), lambda i: (i // tps, 0, i % tps))
    wp = jnp.pad(conv_w, ((0, HALO - CONV_WIDTH), (0, 0)))
    return pl.pallas_call(
        functools.partial(_front0_seq_kernel, conv_dim=conv_dim, att_dim=att_dim, tiles_per_seq=tps),
        out_shape=(jax.ShapeDtypeStruct((m, conv_dim), F32), jax.ShapeDtypeStruct((m // seq_len, HALO, conv_dim), F32),
                   jax.ShapeDtypeStruct((m, att_dim), F32), kv_shape, kv_shape),
        grid=(m // tm,),
        in_specs=[pl.BlockSpec((tm, d), row), _const_spec((1, d)), _const_spec(w_in.shape),
                  pl.BlockSpec((tm, LANES), tab), pl.BlockSpec((tm, LANES), tab),
                  _const_spec((HALO, conv_dim)), _const_spec((1, conv_dim)), _const_spec((1, conv_dim)),
                  _const_spec((1, conv_dim))],
        out_specs=(pl.BlockSpec((tm, conv_dim), row), pl.BlockSpec((1, HALO, conv_dim), lambda i: (i // tps, 0, 0)),
                   pl.BlockSpec((tm, att_dim), row), kv_spec, kv_spec),
        scratch_shapes=[pltpu.VMEM((tm, att_dim), F32), pltpu.VMEM((HALO + tm, conv_dim), F32)],
        compiler_params=_cparams(("arbitrary",)),
        name="front0",
    )(h, g, w_in, cos_tab, sin_tab, wp, conv_b, ln_g, ln_b)


def _rope_tables(pos):
    half = ATT_HD // 2
    inv_freq = ROPE_THETA ** (-jnp.arange(half, dtype=F32) / half)
    ang = pos.astype(F32)[:, None] * inv_freq[None, :]
    cos, sin = jnp.cos(ang), jnp.sin(ang)
    reps = LANES // ATT_HD
    return (jnp.tile(jnp.concatenate([cos, cos], axis=1), (1, reps)),
            jnp.tile(jnp.concatenate([-sin, sin], axis=1), (1, reps)))


def _ln_silu(y, g, b):
    mu = jnp.mean(y, axis=-1, keepdims=True)
    yc = y - mu
    var = jnp.mean(yc * yc, axis=-1, keepdims=True)
    ln = yc * lax.rsqrt(var + LN_EPS) * g + b
    return ln * _sigmoid(ln)


def _conv_block(full_scr, w_ref, b_ref, lg_ref, lb_ref, r0, sub):
    first = HALO - (CONV_WIDTH - 1)
    acc = jnp.broadcast_to(b_ref[...], (sub, b_ref.shape[1]))
    for res in range(SUBLANES):
        rows = sub + (SUBLANES if res else 0)
        z = None
        for j in range(CONV_WIDTH):
            if (first + j) % SUBLANES != res:
                continue
            base = r0 + first + j - res
            term = w_ref[j:j + 1, :] * full_scr[base:base + rows, :]
            z = term if z is None else z + term
        acc = acc + z[res:res + sub, :]
    return _ln_silu(acc, lg_ref[...], lb_ref[...])


def _conv_step_kernel(st_ref, u_ref, w_ref, b_ref, lg_ref, lb_ref, c_ref):
    acc = b_ref[...] + w_ref[CONV_WIDTH - 1:CONV_WIDTH, :] * u_ref[...]
    for j in range(CONV_WIDTH - 1):
        acc = acc + w_ref[j:j + 1, :] * st_ref[j]
    c_ref[...] = _ln_silu(acc, lg_ref[...], lb_ref[...])


def _conv_step(state_t, u, w, b, lg, lb):
    bsz, c = u.shape
    wp = jnp.pad(w, ((0, HALO - CONV_WIDTH), (0, 0)))
    return pl.pallas_call(
        _conv_step_kernel,
        out_shape=jax.ShapeDtypeStruct((bsz, c), F32),
        grid=(1,),
        in_specs=[_const_spec(state_t.shape), _const_spec((bsz, c)), _const_spec((HALO, c)),
                  _const_spec((1, c)), _const_spec((1, c)), _const_spec((1, c))],
        out_specs=pl.BlockSpec((bsz, c), lambda i: (0, 0)),
        compiler_params=_cparams(("arbitrary",)),
        name="conv_step",
    )(state_t, u, wp, b, lg, lb)


def _top_blocks(gate, n_past):
    blk = lax.broadcasted_iota(jnp.int32, gate.shape, 0).astype(F32)
    g = jnp.where(blk < n_past.astype(F32), gate, NEG_INF)
    picks = []
    for r in range(MOBA_TOPK):
        m = jnp.max(g, axis=0, keepdims=True)
        idx = jnp.min(jnp.where(g == m, blk, 1e9), axis=0, keepdims=True)
        picks.append(jnp.where(r < n_past, idx, -1.0))
        g = jnp.where(blk == idx, -jnp.inf, g)
    return picks


def _moba_kernel(q_ref, k_ref, v_ref, o_ref, kmean_scr, kb_scr, vt_scr, s_scr, *, nb, scale, n_pairs):
    i = pl.program_id(2)
    blk = MOBA_BLOCK

    @pl.when(i == 0)
    def _():
        kmean_scr[...] = jnp.zeros_like(kmean_scr)
        for pr in range(n_pairs):
            feats = slice(pr * LANES, (pr + 1) * LANES)
            vt_scr[pr] = v_ref[0, feats, :].astype(BF16)
            for j in range(nb):
                rows = slice(j * blk, (j + 1) * blk)
                kj = k_ref[0, feats, rows].T
                kmean_scr[pr, j:j + 1, :] = jnp.mean(kj, axis=0, keepdims=True)
                kb_scr[pr, rows, :] = kj.astype(BF16)

    pairs = [_moba_pair(q_ref[0, :, pr * LANES:(pr + 1) * LANES], kmean_scr.at[pr], kb_scr.at[pr], vt_scr.at[pr],
                        s_scr.at[pr], i, nb, scale) for pr in range(n_pairs)]
    for pr in pairs:
        pr.prefetch(jnp.int32(0), 0)
    carry = [pr.start() for pr in pairs]

    def body(jj, carry):
        j = 2 * jj
        for pr in pairs:
            pr.prefetch(j + 1, 1)
        carry = [pr.fold(j, 0, c) for pr, c in zip(pairs, carry)]
        for pr in pairs:
            pr.prefetch(j + 2, 0)
        return [pr.fold(j + 1, 1, c) for pr, c in zip(pairs, carry)]

    carry = lax.fori_loop(0, (i + 1) // 2, body, carry)
    for n, (pr, c) in enumerate(zip(pairs, carry)):
        o_ref[0, :, n * LANES:(n + 1) * LANES] = pr.finish(c)


class _MobaPair:
    def __init__(self, prefetch, start, fold, finish):
        self.prefetch, self.start, self.fold, self.finish = prefetch, start, fold, finish


def _moba_pair(q_raw, kmean_scr, kb_scr, vt_scr, s_scr, i, nb, scale):
    blk = MOBA_BLOCK
    hpp = LANES // ATT_HD
    q = q_raw * (scale * LOG2_E)
    lane = lax.broadcasted_iota(jnp.int32, q.shape, 1)
    q2 = jnp.concatenate([jnp.where((lane >= hh * ATT_HD) & (lane < (hh + 1) * ATT_HD), q, 0.0)
                          for hh in range(hpp)], axis=0)
    q_hi, q_lo = _split(q2)
    km_hi, km_lo = _split(kmean_scr[...])
    nbp = km_hi.shape[0]
    nt = (((1,), (1,)), ((), ()))

    def offset(j):
        return pl.multiple_of(jnp.minimum(j, nb - 1) * blk, blk)

    start = offset(i)
    stacked = lax.dot_general(jnp.concatenate([km_hi, km_lo, kb_scr[pl.ds(start, blk), :]], axis=0), q_hi, nt,
                              preferred_element_type=F32)
    gate = (stacked[0:nbp] + stacked[nbp:2 * nbp]
            + lax.dot_general(km_hi, q_lo, nt, preferred_element_type=F32))
    s = stacked[2 * nbp:]
    picks = _top_blocks(gate, i)

    ones = jnp.ones((SUBLANES, blk), BF16)

    def pv(off, p):
        vt_j = vt_scr[:, pl.ds(off, blk)]
        upd = [jnp.dot(vt_j[hh * ATT_HD:(hh + 1) * ATT_HD, :], p[:, hh * blk:(hh + 1) * blk],
                       preferred_element_type=F32) for hh in range(hpp)]
        return upd, jnp.dot(ones, p, preferred_element_type=F32)[0:1, :]

    def scores(off):
        return lax.dot_general(kb_scr[pl.ds(off, blk), :], q_hi, nt, preferred_element_type=F32)

    def past_scores(j):
        jf = j.astype(F32)
        chosen = (picks[0] == jf) | (picks[1] == jf) | (picks[2] == jf)
        return jnp.where(chosen, scores(offset(j)), NEG_INF)

    def prefetch(j, slot):
        s_scr[slot] = past_scores(j)

    def start_state():
        key = lax.broadcasted_iota(jnp.int32, s.shape, 0)
        qry = lax.broadcasted_iota(jnp.int32, s.shape, 1) % blk
        s_own = jnp.where(key <= qry, s, NEG_INF)
        m0 = jnp.max(s_own, axis=0, keepdims=True)
        acc0, l0 = pv(start, jnp.exp2(s_own - m0).astype(BF16))
        return m0, l0, acc0

    def fold(j, slot, carry):
        m, l, acc = carry
        sj = s_scr[slot]
        m_new = jnp.maximum(m, jnp.max(sj, axis=0, keepdims=True))
        alpha = jnp.exp2(m - m_new)
        upd, psum = pv(offset(j), jnp.exp2(sj - m_new).astype(BF16))
        l = alpha * l + psum
        acc = [alpha[:, hh * blk:(hh + 1) * blk] * acc[hh] + upd[hh] for hh in range(hpp)]
        return m_new, l, acc

    def finish(carry):
        _, l, acc = carry
        out_t = jnp.concatenate([acc[hh] / l[:, hh * blk:(hh + 1) * blk] for hh in range(hpp)], axis=0)
        return out_t.T

    return _MobaPair(prefetch, start_state, fold, finish)


def _moba_prompt(q, kt, vt):
    bsz, s, a = q.shape
    nb = s // MOBA_BLOCK
    n_pairs = _largest_divisor(a // LANES, MOBA_PAIRS_PER_STEP)
    wide = n_pairs * LANES
    blk = pl.BlockSpec((1, MOBA_BLOCK, wide), lambda b, hp, i: (b, i, hp))
    seq = pl.BlockSpec((1, wide, s), lambda b, hp, i: (b, hp, 0))
    nb_pad = -(-nb // BF16_ROWS) * BF16_ROWS
    return pl.pallas_call(
        functools.partial(_moba_kernel, nb=nb, scale=ATT_HD ** -0.5, n_pairs=n_pairs),
        out_shape=jax.ShapeDtypeStruct((bsz, s, a), F32),
        grid=(bsz, a // wide, nb),
        in_specs=[blk, seq, seq],
        out_specs=blk,
        scratch_shapes=[pltpu.VMEM((n_pairs, nb_pad, LANES), F32), pltpu.VMEM((n_pairs, s, LANES), BF16),
                        pltpu.VMEM((n_pairs, LANES, s), BF16),
                        pltpu.VMEM((n_pairs, 2, MOBA_BLOCK, (LANES // ATT_HD) * MOBA_BLOCK), F32)],
        compiler_params=_cparams(("parallel", "parallel", "arbitrary")),
        name="moba_prompt",
    )(q, kt, vt)


def _moba_gate_kernel(pt_ref, q_ref, *refs, nb, pages_per_step, pages_per_block):
    del pt_ref
    page_refs = refs[:pages_per_step]
    sel_ref, qb_scr, g_scr = refs[pages_per_step:]
    j = pl.program_id(1)
    blocks_per_step = pages_per_step // pages_per_block

    @pl.when(j == 0)
    def _():
        qb_scr[...] = jnp.broadcast_to(q_ref[0], qb_scr.shape)

    qb = qb_scr[...]
    for bl in range(blocks_per_step):
        ksum = page_refs[bl * pages_per_block][0]
        for pg in range(1, pages_per_block):
            ksum = ksum + page_refs[bl * pages_per_block + pg][0]
        g_scr[j * blocks_per_step + bl] = jnp.sum(ksum * qb, axis=1)

    @pl.when(j == nb // blocks_per_step - 1)
    def _():
        gate = jnp.sum(g_scr[...], axis=2, keepdims=True) * (1.0 / MOBA_BLOCK)
        blk = lax.broadcasted_iota(jnp.int32, gate.shape, 0).astype(F32)
        lane = lax.broadcasted_iota(jnp.int32, sel_ref.shape[1:], 1)
        sel = jnp.zeros(sel_ref.shape[1:], F32)
        for r in range(MOBA_TOPK):
            m = jnp.max(gate, axis=0, keepdims=True)
            idx = jnp.min(jnp.where(gate == m, blk, 1e9), axis=0, keepdims=True)
            sel = jnp.where(lane == r, idx[0], sel)
            gate = jnp.where(blk == idx, -jnp.inf, gate)
        sel_ref[0] = sel.astype(jnp.int32)


def _moba_gate(page_table, q_col, cache_kt, nb, pages_per_block, pages_per_step):
    bsz, heads, hd, _ = q_col.shape
    page = cache_kt.shape[3]
    steps = nb * pages_per_block // pages_per_step

    def page_spec(w):
        return pl.BlockSpec((1, heads, hd, page), lambda b, j, pt: (pt[b, j * pages_per_step + w], 0, 0, 0))

    grid_spec = pltpu.PrefetchScalarGridSpec(
        num_scalar_prefetch=1,
        grid=(bsz, steps),
        in_specs=[pl.BlockSpec((1, heads, hd, 1), lambda b, j, pt: (b, 0, 0, 0))]
        + [page_spec(w) for w in range(pages_per_step)],
        out_specs=pl.BlockSpec((1, heads, LANES), lambda b, j, pt: (b, 0, 0)),
        scratch_shapes=[pltpu.VMEM((heads, hd, page), F32), pltpu.VMEM((nb, heads, page), F32)],
    )
    return pl.pallas_call(
        functools.partial(_moba_gate_kernel, nb=nb, pages_per_step=pages_per_step, pages_per_block=pages_per_block),
        out_shape=jax.ShapeDtypeStruct((bsz, heads, LANES), jnp.int32),
        grid_spec=grid_spec,
        compiler_params=_cparams(("parallel", "arbitrary")),
        name="moba_gate",
    )(page_table, q_col, *([cache_kt] * pages_per_step))


def _moba_decode_kernel(pt_ref, sel_ref, q_ref, kn_ref, vn_ref, *refs, n_pages, heads_per_step, scale):
    del pt_ref, sel_ref
    o_ref = refs[2 * n_pages * heads_per_step]
    for u in range(heads_per_step):
        k_refs = refs[u * n_pages:(u + 1) * n_pages]
        v_refs = refs[(heads_per_step + u) * n_pages:(heads_per_step + u + 1) * n_pages]
        q = q_ref[0, u]
        s_own = jnp.sum(q * kn_ref[0, u], axis=0, keepdims=True) * scale
        s = [jnp.sum(kr[0, 0] * q, axis=0, keepdims=True) * scale for kr in k_refs]
        m = s_own
        for sp in s:
            m = jnp.maximum(m, jnp.max(sp, axis=1, keepdims=True))
        p_own = jnp.exp(s_own - m)
        l = p_own
        acc = p_own * vn_ref[0, u]
        for sp, vr in zip(s, v_refs):
            p = jnp.exp(sp - m)
            l = l + jnp.sum(p, axis=1, keepdims=True)
            acc = acc + jnp.sum(vr[0, 0] * p, axis=1, keepdims=True)
        o_ref[0, u] = acc / l


def _moba_decode(page_table, sel, q_col, kn_col, vn_col, cache_kt, cache_vt, pages_per_block):
    bsz, heads, hd, _ = q_col.shape
    page = cache_kt.shape[3]
    n_pages = MOBA_TOPK * pages_per_block
    hps = _largest_divisor(heads, DECODE_HEADS_PER_STEP)
    vec = pl.BlockSpec((1, hps, hd, 1), lambda b, hg, pt, sl: (b, hg, 0, 0))

    def page_spec(u, r, w):
        return pl.BlockSpec(
            (1, 1, hd, page),
            lambda b, hg, pt, sl: (pt[b, pages_per_block * sl[b, hg * hps + u, r] + w], hg * hps + u, 0, 0))

    pages = [page_spec(u, r, w) for u in range(hps) for r in range(MOBA_TOPK) for w in range(pages_per_block)]
    grid_spec = pltpu.PrefetchScalarGridSpec(
        num_scalar_prefetch=2,
        grid=(bsz, heads // hps),
        in_specs=[vec, vec, vec] + pages + pages,
        out_specs=vec,
    )
    return pl.pallas_call(
        functools.partial(_moba_decode_kernel, n_pages=n_pages, heads_per_step=hps, scale=ATT_HD ** -0.5),
        out_shape=jax.ShapeDtypeStruct((bsz, heads, hd, 1), F32),
        grid_spec=grid_spec,
        compiler_params=_cparams(("parallel", "parallel")),
        name="moba_decode",
    )(page_table, sel, q_col, kn_col, vn_col, *([cache_kt] * len(pages)), *([cache_vt] * len(pages)))


def _tail_kernel(*refs, n_mix, nb, s, ff_chunk):
    h_ref = refs[0]
    mix_refs = refs[1:1 + n_mix]
    wmix_refs = refs[1 + n_mix:1 + 2 * n_mix]
    g_ref, wcq_ref, mk_ref, mv_ref, wco_ref, wup_ref, wdn_ref, o_ref, ca_scr = refs[1 + 2 * n_mix:]

    mix = _mm(mix_refs[0][...], wmix_refs[0][...])
    for mr, wr in zip(mix_refs[1:], wmix_refs[1:]):
        mix = mix + _mm(mr[...], wr[...])
    h = h_ref[...] + _rms(mix, g_ref[1:2, :])

    q = _mm(_rms(h, g_ref[2:3, :]), wcq_ref[...])
    rows = max(s, SUBLANES)
    mem = mk_ref.shape[1] // MEM_HEADS
    for bi in range(nb):
        qb = q[bi * s:(bi + 1) * s, :]
        if s < rows:
            qb = jnp.broadcast_to(qb[0:1, :], (rows, qb.shape[1]))
        for hh in range(MEM_HEADS):
            sl = slice(hh * MEM_HD, (hh + 1) * MEM_HD)
            tokens = pl.ds(hh, mem, stride=MEM_HEADS)
            sc = _mm_nt(qb[:, sl], mk_ref[bi, tokens, :]) * (MEM_HD ** -0.5)
            p = jnp.exp(sc - jnp.max(sc, axis=1, keepdims=True))
            oh = _mm(p, mv_ref[bi, tokens, :]) / jnp.sum(p, axis=1, keepdims=True)
            ca_scr[bi * s:(bi + 1) * s, sl] = oh[0:s, :]
    h = h + _rms(_mm(ca_scr[...], wco_ref[...]), g_ref[3:4, :])

    hn = _rms(h, g_ref[4:5, :]).astype(BF16)
    d_ff = wup_ref.shape[1]
    acc = jnp.zeros(h.shape, F32)
    for c0 in range(0, d_ff, ff_chunk):
        up = jnp.dot(hn, wup_ref[:, c0:c0 + ff_chunk], preferred_element_type=F32)
        up = jnp.square(jnp.maximum(up, 0.0))
        acc = acc + jnp.dot(up.astype(BF16), wdn_ref[c0:c0 + ff_chunk, :], preferred_element_type=F32)
    o_ref[...] = h + _rms(acc, g_ref[5:6, :])


def _tail(h, mixes, w_mixes, gains, w_cq, mk, mv, w_co, w_up, w_down, tm, seq_len, layer=0):
    m, d = h.shape
    s = min(seq_len, tm)
    nb = tm // s
    n_mix = len(mixes)
    mem_rows = mk.shape[-2]
    mem_dim = MEM_HEADS * MEM_HD
    row = lambda i: (i, 0)
    tiles_per_seq = max(seq_len // tm, 1)
    if mk.ndim == 4:
        mem_spec = pl.BlockSpec((None, nb, mem_rows, MEM_HD), lambda i: (layer, i // tiles_per_seq, 0, 0))
    else:
        mem_spec = pl.BlockSpec((nb, mem_rows, MEM_HD), lambda i: (i // tiles_per_seq, 0, 0))
    in_specs = [pl.BlockSpec((tm, d), row)]
    in_specs += [pl.BlockSpec((tm, x.shape[1]), row) for x in mixes]
    in_specs += [_const_spec(w.shape) for w in w_mixes]
    in_specs += [_const_spec(gains.shape), _const_spec(w_cq.shape),
                 mem_spec, mem_spec,
                 _const_spec(w_co.shape), _const_spec(w_up.shape), _const_spec(w_down.shape)]
    return pl.pallas_call(
        functools.partial(_tail_kernel, n_mix=n_mix, nb=nb, s=s, ff_chunk=min(1024, w_up.shape[1])),
        out_shape=jax.ShapeDtypeStruct((m, d), F32),
        grid=(m // tm,),
        in_specs=in_specs,
        out_specs=pl.BlockSpec((tm, d), row),
        scratch_shapes=[pltpu.VMEM((tm, mem_dim), F32)],
        compiler_params=_cparams(("parallel",)),
        name="tail",
    )(h, *mixes, *w_mixes, gains, w_cq, mk, mv, w_co, w_up, w_down)


def _front1_kernel(h_ref, hprev_ref, xlast_ref, g_ref, mu_ref, wr_ref, wk_ref, wv_ref, w0_ref, w1_ref, w2_ref,
                   a0_ref, a1_ref, a2_ref, g1_ref, g2_ref, kk_ref, ka_ref, seg_ref,
                   r_out, ld_out, k_out, v_out, kk_out, a_out, gt_out, sh_out, *, seq, tiles_per_seq):
    i = pl.program_id(0)
    g = g_ref[...]
    hn = _rms(h_ref[...], g)
    if seq:
        prev_row = _rms(hprev_ref[SUBLANES - 1:SUBLANES, :], g)
        prev_row = jnp.where(i % tiles_per_seq == 0, xlast_ref[0], prev_row)
        rowid = lax.broadcasted_iota(jnp.int32, hn.shape, 0)
        xprev = jnp.where(rowid == 0, prev_row, pltpu.roll(hn, 1, axis=0))
        sh_out[0] = hn[hn.shape[0] - 1:, :]
    else:
        xprev = xlast_ref[...]
        sh_out[...] = hn
    xx = xprev - hn
    mix = lambda n: hn + xx * mu_ref[n:n + 1, :]
    r = _mm(mix(0), wr_ref[...])
    k = _mm(mix(2), wk_ref[...])
    v = _mm(mix(3), wv_ref[...])
    wl = w0_ref[...] + _mm(jnp.tanh(_mm(mix(1), w1_ref[...])), w2_ref[...])
    z = -wl
    softplus = jnp.maximum(z, 0.0) + jnp.log(1.0 + jnp.exp(-jnp.abs(z)))
    ld_out[...] = -jnp.exp(-softplus - 0.5)
    a = _sigmoid(a0_ref[...] + _mm(_mm(mix(4), a1_ref[...]), a2_ref[...]))
    gt_out[...] = _mm(_sigmoid(_mm(mix(5), g1_ref[...])), g2_ref[...])
    kk = k * kk_ref[...]
    sq = kk * kk
    seg_w = seg_ref.shape[0]
    ss = jnp.concatenate([_mm2(sq[:, c:c + seg_w], seg_ref[...]) for c in range(0, sq.shape[1], seg_w)], axis=1)
    kk_out[...] = kk / jnp.maximum(jnp.sqrt(ss), 1e-12)
    k_out[...] = k * (1.0 + (a - 1.0) * ka_ref[...])
    r_out[...] = r
    v_out[...] = v
    a_out[...] = a


def _front1(h, x_last, g, p, tm, s):
    m, d = h.shape
    bsz = m // s
    seq = s > 1
    row = lambda i: (i, 0)
    big = jax.ShapeDtypeStruct((m, d), F32)
    if seq:
        tps = s // tm
        per = tm // SUBLANES
        hprev_spec = pl.BlockSpec((SUBLANES, d), lambda i: (jnp.maximum(i * per - 1, 0), 0))
        xlast = x_last.reshape(bsz, 1, d)
        xlast_spec = pl.BlockSpec((1, 1, d), lambda i: (i // tps, 0, 0))
        sh_shape = jax.ShapeDtypeStruct((bsz, 1, d), F32)
        sh_spec = pl.BlockSpec((1, 1, d), lambda i: (i // tps, 0, 0))
        sem = ("arbitrary",)
    else:
        tps = 1
        hprev_spec = pl.BlockSpec((SUBLANES, d), lambda i: (0, 0))
        xlast = x_last
        xlast_spec = pl.BlockSpec((tm, d), row)
        sh_shape = big
        sh_spec = pl.BlockSpec((tm, d), row)
        sem = ("parallel",)
    consts = [g, p["mu"], p["w_r"], p["w_k"], p["w_v"], p["w0"], p["w1"], p["w2"], p["a0"], p["a1"], p["a2"],
              p["g1"], p["g2"], p["k_k"], p["k_a"], p["seg"]]
    outs = pl.pallas_call(
        functools.partial(_front1_kernel, seq=seq, tiles_per_seq=tps),
        out_shape=(big,) * 7 + (sh_shape,),
        grid=(m // tm,),
        in_specs=[pl.BlockSpec((tm, d), row), hprev_spec, xlast_spec] + [_const_spec(c.shape) for c in consts],
        out_specs=(pl.BlockSpec((tm, d), row),) * 7 + (sh_spec,),
        compiler_params=_cparams(sem),
        name="front1",
    )(h, h, xlast, *consts)
    return outs[:7], outs[7].reshape(bsz, d)


def _scan_kernel(r_ref, ld_ref, k_ref, v_ref, kk_ref, a_ref, gt_ref, s0_ref, rk_ref, lg_ref, lb_ref,
                 o_ref, st_ref, s_scr, *, n_chunks, n_groups, n_seq):
    c = pl.program_id(1)
    grp = SCAN_GROUP
    w = grp * RWKV_HD
    sbd = (lax.broadcasted_iota(jnp.int32, (w, w), 0) // RWKV_HD
           == lax.broadcasted_iota(jnp.int32, (w, w), 1) // RWKV_HD)
    chains = [(bb, q) for bb in range(n_seq) for q in range(n_groups)]

    @pl.when(c == 0)
    def _():
        for g, (bb, q) in enumerate(chains):
            s0 = s0_ref[bb, q]
            s_scr[g] = jnp.where(sbd, jnp.concatenate([s0] * grp, axis=1), 0.0)

    seqs = lambda ref: jnp.concatenate([ref[bb] for bb in range(n_seq)], axis=1)
    rep = lambda ref: jnp.concatenate([ref[...]] * n_seq, axis=1)

    def write(g, val):
        bb, q = chains[g]
        o_ref[bb, :, q * w:(q + 1) * w] = val

    _scan_chunk(seqs(r_ref), seqs(ld_ref), seqs(k_ref), seqs(v_ref), seqs(kk_ref), seqs(a_ref), seqs(gt_ref),
                rep(rk_ref), rep(lg_ref), rep(lb_ref), write, s_scr, len(chains))

    @pl.when(c == n_chunks - 1)
    def _():
        for g, (bb, q) in enumerate(chains):
            full = s_scr[g]
            acc = full[:, 0:RWKV_HD]
            for hh in range(1, grp):
                acc = acc + full[:, hh * RWKV_HD:(hh + 1) * RWKV_HD]
            st_ref[bb, q] = acc


def _scan_chunk(r, ld, k, v, kk, a_gate, gt, r_k, lnx_g, lnx_b, write, s_scr, n_groups):
    ch = SCAN_CHUNK
    grp = SCAN_GROUP
    w = grp * RWKV_HD
    rows = grp * ch
    gs = range(n_groups)
    rhead = lax.broadcasted_iota(jnp.int32, (rows, w), 0) // ch
    chead = lax.broadcasted_iota(jnp.int32, (rows, w), 1) // RWKV_HD
    bd = rhead == chead
    tile_r = lambda x: jnp.concatenate([x] * grp, axis=0)
    diag = lambda x: jnp.where(bd, tile_r(x), 0.0)
    part = lambda x, q: x[:, q * w:(q + 1) * w]

    b = kk * a_gate
    trow = lax.broadcasted_iota(jnp.int32, ld.shape, 0)
    cum = ld
    sh = 1
    while sh < ch:
        cum = cum + jnp.where(trow >= sh, pltpu.roll(cum, sh, axis=0), 0.0)
        sh *= 2
    clast = cum[ch - 1:ch, :]
    p_inv = jnp.exp(-cum)
    p_to = jnp.exp(clast - cum)
    rt = r * jnp.exp(cum)
    at = -kk * jnp.exp(cum - ld)
    bt = b * p_inv
    kt = k * p_inv
    bh = b * p_to
    kh = k * p_to
    p_end = jnp.exp(clast)

    trow2 = lax.broadcasted_iota(jnp.int32, (ch, w), 0)
    tcol2 = lax.broadcasted_iota(jnp.int32, (ch, w), 1) % ch
    strict = tcol2 < trow2
    incl = tcol2 <= trow2
    eye = jnp.where(tcol2 == trow2, 1.0, 0.0)
    strict2 = jnp.concatenate([strict, strict], axis=1)
    incl2 = jnp.concatenate([incl, incl], axis=1)

    lhs = [jnp.concatenate([part(at, q), part(rt, q)], axis=0).astype(BF16) for q in gs]
    rhs = [jnp.concatenate([diag(part(bt, q)), diag(part(kt, q))], axis=0) for q in gs]
    mall = [_mm_nt(lhs[q], rhs[q]) for q in gs]
    la = [jnp.where(strict2, mall[q][0:ch], 0.0) for q in gs]
    lr = [jnp.where(incl2, mall[q][ch:], 0.0) for q in gs]
    lp = [la[q][:, 0:rows] for q in gs]
    mkk = [jnp.concatenate([la[q][:, rows:], lr[q][:, rows:]], axis=0) for q in gs]
    mrb = [lr[q][:, 0:rows] for q in gs]

    tinv = [eye + lp[q] for q in gs]
    lp = [_mm(lp[q], diag(lp[q])) for q in gs]
    n = 4
    while n < ch:
        both = [_mm(jnp.concatenate([lp[q], tinv[q]], axis=0), diag(lp[q])) for q in gs]
        lp = [both[q][0:ch] for q in gs]
        tinv = [tinv[q] + both[q][ch:] for q in gs]
        n *= 2
    tinv = [tinv[q] + _mm(tinv[q], diag(lp[q])) for q in gs]

    vd = [diag(part(v, q)).astype(BF16) for q in gs]
    state = [s_scr[q] for q in gs]
    s0t = [_mm_nt(lhs[q], state[q]) for q in gs]
    mv = [_mm(mkk[q], vd[q]) for q in gs]
    wmat = [s0t[q][0:ch] + mv[q][0:ch] for q in gs]
    u = [_mm(tinv[q], diag(wmat[q])) for q in gs]
    ys = [s0t[q][ch:] + _mm(mrb[q], diag(u[q])) + mv[q][ch:] for q in gs]
    upd = [_mm_tn(jnp.concatenate([u[q], part(v, q)], axis=0),
                  jnp.concatenate([part(bh, q), part(kh, q)], axis=0)) for q in gs]
    sbd = (lax.broadcasted_iota(jnp.int32, (w, w), 0) // RWKV_HD
           == lax.broadcasted_iota(jnp.int32, (w, w), 1) // RWKV_HD)
    for q in gs:
        s_scr[q] = state[q] * part(p_end, q) + jnp.where(sbd, upd[q], 0.0)

    seg = jnp.where(sbd, 1.0, 0.0).astype(BF16)
    rkr = r * k * r_k
    for q in gs:
        y = ys[q]
        mean = _mm2(y, seg) * (1.0 / RWKV_HD)
        yc = y - mean
        var = _mm2(yc * yc, seg) * (1.0 / RWKV_HD)
        yn = yc * lax.rsqrt(var + GN_EPS) * part(lnx_g, q) + part(lnx_b, q)
        bonus = _mm2(part(rkr, q), seg) * part(v, q)
        write(q, (yn + bonus) * part(gt, q))


def _scan(streams, gate, s0, r_k, lnx_g, lnx_b):
    bsz, t, d = gate.shape
    w = SCAN_GROUP * RWKV_HD
    nq = d // w
    nc = t // SCAN_CHUNK
    n_seq = _largest_divisor(bsz, SCAN_SEQS_PER_STEP)
    tok = pl.BlockSpec((n_seq, SCAN_CHUNK, d), lambda b, c: (b, c, 0))
    st = pl.BlockSpec((n_seq, nq, w, RWKV_HD), lambda b, c: (b, 0, 0, 0))
    s0g = s0.reshape(bsz, nq, w, RWKV_HD)
    out, s_t = pl.pallas_call(
        functools.partial(_scan_kernel, n_chunks=nc, n_groups=nq, n_seq=n_seq),
        out_shape=(jax.ShapeDtypeStruct((bsz, t, d), F32), jax.ShapeDtypeStruct(s0g.shape, F32)),
        grid=(bsz // n_seq, nc),
        in_specs=[tok] * 7 + [st, _const_spec((1, d)), _const_spec((1, d)), _const_spec((1, d))],
        out_specs=(tok, st),
        scratch_shapes=[pltpu.VMEM((n_seq * nq, w, w), F32)],
        compiler_params=_cparams(("parallel", "arbitrary")),
        name="rwkv_scan",
    )(*streams, gate, s0g, r_k, lnx_g, lnx_b)
    return out, s_t.reshape(s0.shape)


def _wkv_step_kernel(r_ref, ld_ref, k_ref, kk_ref, a_ref, v_ref, gt_ref, s_ref, rk_ref, lg_ref, lb_ref,
                     o_ref, st_ref):
    r, k, kk = r_ref[0], k_ref[0], kk_ref[0]
    v = v_ref[0]
    state = s_ref[0]
    sa = -jnp.sum(state * kk, axis=2, keepdims=True)
    state = state * jnp.exp(ld_ref[0]) + sa * (kk * a_ref[0]) + v * k
    st_ref[0] = state
    y = jnp.sum(state * r, axis=2, keepdims=True)
    mean = jnp.mean(y, axis=1, keepdims=True)
    yc = y - mean
    var = jnp.mean(yc * yc, axis=1, keepdims=True)
    yn = yc * lax.rsqrt(var + GN_EPS) * lg_ref[...] + lb_ref[...]
    bonus = jnp.sum(r * k * rk_ref[...], axis=2, keepdims=True) * v
    o_ref[0] = (yn + bonus) * gt_ref[0]


def _wkv_step(streams, gate, s0, r_k, lnx_g, lnx_b):
    r, ld, k, v, kk, a = streams
    bsz, d = r.shape
    n = RWKV_HD
    h = d // n
    row = lambda x: x.reshape(-1, h, 1, n)
    col = lambda x: x.reshape(-1, h, n, 1)
    row_spec = pl.BlockSpec((1, h, 1, n), lambda b: (b, 0, 0, 0))
    col_spec = pl.BlockSpec((1, h, n, 1), lambda b: (b, 0, 0, 0))
    st_spec = pl.BlockSpec((1, h, n, n), lambda b: (b, 0, 0, 0))
    out, s_t = pl.pallas_call(
        _wkv_step_kernel,
        out_shape=(jax.ShapeDtypeStruct((bsz, h, n, 1), F32), jax.ShapeDtypeStruct(s0.shape, F32)),
        grid=(bsz,),
        in_specs=[row_spec] * 5 + [col_spec] * 2 + [st_spec, _const_spec((h, 1, n)), _const_spec((h, n, 1)),
                                                   _const_spec((h, n, 1))],
        out_specs=(col_spec, st_spec),
        compiler_params=_cparams(("parallel",)),
        name="wkv_step",
    )(row(r), row(ld), row(k), row(kk), row(a), col(v), col(gate), s0,
      r_k.reshape(h, 1, n), lnx_g.reshape(h, n, 1), lnx_b.reshape(h, n, 1))
    return out.reshape(bsz, d), s_t


def _row_tile(m, cap):
    t = min(m, cap)
    assert m % t == 0
    return t


def _largest_divisor(n, cap):
    return max(t for t in range(1, cap + 1) if n % t == 0)


def _pad_cols(w, to):
    return jnp.pad(w, ((0, 0), (0, to - w.shape[1])))


def _pad_rows(w, to):
    return jnp.pad(w, ((0, to - w.shape[0]), (0, 0)))


def kernel(x_prompt, x_sample, mem_prompt, cache_k, cache_v, page_table, cache_mem_k, cache_mem_v, state_conv, state_shift, state_wkv, norm_gain, w_in, conv_w, conv_b, conv_ln_g, conv_ln_b, w_out, rwkv_mu, rwkv_w_r, rwkv_w_k, rwkv_w_v, rwkv_w_o, rwkv_w0, rwkv_w1, rwkv_w2, rwkv_a0, rwkv_a1, rwkv_a2, rwkv_g1, rwkv_g2, rwkv_k_k, rwkv_k_a, rwkv_r_k, rwkv_lnx_g, rwkv_lnx_b, w_cq, w_ck, w_cv, w_co, w_up, w_down):
    bp, sp, d = x_prompt.shape
    bs, ss, _ = x_sample.shape
    assert ss == 1
    depth = norm_gain.shape[0]
    conv_dim = conv_w.shape[2]
    att_dim = (w_in.shape[2] - 2 * conv_dim) // 3
    mem = mem_prompt.shape[1]
    mem_dim = w_ck.shape[2]
    n_pages, page = page_table.shape[1], cache_k.shape[2]
    past_len = n_pages * page
    pages_per_block = MOBA_BLOCK // page
    att_heads = att_dim // ATT_HD
    nb_past = past_len // MOBA_BLOCK
    assert past_len % MOBA_BLOCK == 0 and nb_past >= MOBA_TOPK and sp % MOBA_BLOCK == 0
    heads = d // RWKV_HD

    tm_p = _row_tile(sp, 512)
    tm_s = bs
    bf = lambda x: x.astype(BF16)

    w_mem = bf(jnp.concatenate([w for l in range(depth) for w in (w_ck[l], w_cv[l])], axis=1))
    memkv = _proj(mem_prompt.reshape(bp * mem, d), w_mem, _row_tile(bp * mem, 512))
    memkv = memkv.reshape(bp, mem, depth, 2, mem_dim)
    interleave = lambda x: x.reshape(x.shape[0], mem * MEM_HEADS, MEM_HD)
    mem_k_p = [interleave(memkv[:, :, l, 0]) for l in range(depth)]
    mem_v_p = [interleave(memkv[:, :, l, 1]) for l in range(depth)]

    cos_p, sin_p = _rope_tables(jnp.arange(sp, dtype=jnp.int32))
    cos_s, sin_s = _rope_tables(jnp.full((tm_s,), past_len, dtype=jnp.int32))

    hp = x_prompt.reshape(bp * sp, d)
    hs = x_sample.reshape(bs, d)
    k_p, v_p, k_s, v_s, conv_p, conv_s, shift_p, shift_s, wkv_p, wkv_s = ([] for _ in range(10))

    for l in range(depth):
        gains = norm_gain[l]
        g0 = gains[0:1]
        tail_w = (gains, bf(w_cq[l]))
        tail_w2 = (bf(w_co[l]), bf(w_up[l]), bf(w_down[l]))
        mk_s = cache_mem_k.reshape(depth, bs, mem * MEM_HEADS, MEM_HD)
        mv_s = cache_mem_v.reshape(depth, bs, mem * MEM_HEADS, MEM_HD)
        if l % 2 == 0:
            e = l // 2
            w_in_b = bf(w_in[e])
            w_o = bf(w_out[e])
            cb, lg, lb = conv_b[e][None], conv_ln_g[e][None], conv_ln_b[e][None]

            c, u_last, q, kt, vt = _front0_seq(hp, g0, w_in_b, cos_p, sin_p, conv_w[e], cb, lg, lb, att_dim, tm_p, sp)
            att = _moba_prompt(q.reshape(bp, sp, att_dim), kt, vt)
            hp = _tail(hp, [c, att.reshape(bp * sp, att_dim)],
                       [w_o[:conv_dim], w_o[conv_dim:]], *tail_w, mem_k_p[l], mem_v_p[l], *tail_w2, tm_p, sp)
            token_major = lambda x: jnp.transpose(x.reshape(bp, att_heads, ATT_HD, sp), (0, 3, 1, 2))
            k_p.append(token_major(kt))
            v_p.append(token_major(vt))
            conv_p.append(u_last[:, HALO - (CONV_WIDTH - 1):])

            u, q, k, v = _front0_step(hs, g0, w_in_b, cos_s, sin_s, conv_dim, att_dim)
            c = _conv_step(jnp.swapaxes(state_conv[e], 0, 1), u, conv_w[e], cb, lg, lb)
            ckt = jnp.transpose(cache_k[e], (0, 2, 3, 1))
            cvt = jnp.transpose(cache_v[e], (0, 2, 3, 1))
            pages_per_step = pages_per_block * _largest_divisor(nb_past, GATE_BLOCKS_PER_STEP)
            col = lambda x: x.reshape(bs, att_heads, ATT_HD, 1)
            sel = _moba_gate(page_table, col(q), ckt, nb_past, pages_per_block, pages_per_step)
            att = _moba_decode(page_table, sel[:, :, :MOBA_TOPK], col(q), col(k), col(v), ckt, cvt,
                               pages_per_block).reshape(bs, att_dim)
            hs = _tail(hs, [c, att], [w_o[:conv_dim], w_o[conv_dim:]], *tail_w, mk_s, mv_s, *tail_w2, tm_s, 1, l)
            k_s.append(k.reshape(bs, 1, att_dim // ATT_HD, ATT_HD))
            v_s.append(v.reshape(bs, 1, att_dim // ATT_HD, ATT_HD))
            conv_s.append(jnp.concatenate([state_conv[e][:, 1:], u[:, None]], axis=1))
        else:
            o = l // 2
            lora = LANES
            gl = MXU_DIM
            feat = jnp.arange(min(d, MXU_DIM)) // RWKV_HD
            prm = dict(
                mu=rwkv_mu[o], w_r=bf(rwkv_w_r[o]), w_k=bf(rwkv_w_k[o]), w_v=bf(rwkv_w_v[o]),
                w0=rwkv_w0[o][None], w1=bf(_pad_cols(rwkv_w1[o], lora)), w2=bf(_pad_rows(rwkv_w2[o], lora)),
                a0=rwkv_a0[o][None], a1=bf(_pad_cols(rwkv_a1[o], lora)), a2=bf(_pad_rows(rwkv_a2[o], lora)),
                g1=bf(_pad_cols(rwkv_g1[o], gl)), g2=bf(_pad_rows(rwkv_g2[o], gl)),
                k_k=rwkv_k_k[o][None], k_a=rwkv_k_a[o][None],
                seg=(feat[:, None] == feat[None, :]).astype(BF16))
            r_k = rwkv_r_k[o].reshape(1, d)
            lnx_g, lnx_b = rwkv_lnx_g[o][None], rwkv_lnx_b[o][None]
            w_o = bf(rwkv_w_o[o])

            streams, shift = _front1(hp, jnp.zeros((bp, d), F32), g0, prm, tm_p, sp)
            to3 = lambda x: x.reshape(bp, sp, d)
            gated, s_t = _scan([to3(x) for x in streams[:6]], to3(streams[6]),
                               jnp.zeros((bp, heads, RWKV_HD, RWKV_HD), F32), r_k, lnx_g, lnx_b)
            hp = _tail(hp, [gated.reshape(bp * sp, d)], [w_o], *tail_w, mem_k_p[l], mem_v_p[l], *tail_w2, tm_p, sp)
            shift_p.append(shift)
            wkv_p.append(s_t)

            streams, shift = _front1(hs, state_shift[o], g0, prm, tm_s, 1)
            gated, s_t = _wkv_step(streams[:6], streams[6], state_wkv[o], r_k, lnx_g, lnx_b)
            hs = _tail(hs, [gated], [w_o], *tail_w, mk_s, mv_s, *tail_w2, tm_s, 1, l)
            shift_s.append(shift)
            wkv_s.append(s_t)

    mem_shape = (bp, mem, MEM_HEADS, mem_dim // MEM_HEADS)
    return (hp.reshape(bp, sp, d), hs.reshape(bs, 1, d),
            jnp.stack(k_p), jnp.stack(v_p), jnp.stack(k_s), jnp.stack(v_s),
            jnp.stack(conv_p), jnp.stack(conv_s), jnp.stack(shift_p), jnp.stack(shift_s),
            jnp.stack(wkv_p), jnp.stack(wkv_s),
            jnp.stack([x.reshape(mem_shape) for x in mem_k_p]), jnp.stack([x.reshape(mem_shape) for x in mem_v_p]))
```

```python
import functools

import jax
import jax.numpy as jnp
from jax import lax
from jax.experimental import pallas as pl
from jax.experimental.pallas import tpu as pltpu

F32 = jnp.float32
BF16 = jnp.bfloat16

NORM_EPS = 1e-6
LN_EPS = 1e-5
GN_EPS = 64e-5
NEG_INF = -1e30
ATT_HD = 64
MEM_HEADS = 4
MEM_HD = 128
RWKV_HD = 64
MOBA_BLOCK = 256
MOBA_TOPK = 3
CONV_WIDTH = 31
ROPE_THETA = 10000.0

LANES = 128
SUBLANES = 8
BF16_ROWS = 16
MXU_DIM = 256
VMEM_LIMIT_BYTES = 56 * 1024 * 1024
SCAN_CHUNK = RWKV_HD
SCAN_GROUP = MXU_DIM // RWKV_HD
HALO = 32
CONV_ROWS = 256
FF_CHUNK = 1024
LOG2_E = 1.4426950408889634
SCAN_SEQS_PER_STEP = 4
MOBA_PAIRS_PER_STEP = 2
DECODE_HEADS_PER_STEP = 4
GATE_BLOCKS_PER_STEP = 16


def _cparams(sem):
    return pltpu.CompilerParams(dimension_semantics=sem, vmem_limit_bytes=VMEM_LIMIT_BYTES)


def _const_spec(shape):
    nd = len(shape)
    return pl.BlockSpec(shape, lambda *_: (0,) * nd, pipeline_mode=pl.Buffered(1))


def _rms(x, g):
    return x * lax.rsqrt(jnp.mean(x * x, axis=-1, keepdims=True) + NORM_EPS) * g


def _sigmoid(x):
    return 1.0 / (1.0 + jnp.exp(-x))


def _mm(a, b):
    return jnp.dot(a.astype(BF16), b.astype(BF16), preferred_element_type=F32)


def _mm_nt(a, b):
    return lax.dot_general(a.astype(BF16), b.astype(BF16), (((1,), (1,)), ((), ())),
                           preferred_element_type=F32)


def _mm_tn(a, b):
    return lax.dot_general(a.astype(BF16), b.astype(BF16), (((0,), (0,)), ((), ())),
                           preferred_element_type=F32)


def _split(x):
    hi = x.astype(BF16)
    lo = (x - hi.astype(F32)).astype(BF16)
    return hi, lo


def _mm2(a, b):
    hi, lo = _split(a)
    return (jnp.dot(hi, b, preferred_element_type=F32)
            + jnp.dot(lo, b, preferred_element_type=F32))


def _proj_kernel(x_ref, w_ref, o_ref):
    o_ref[...] = _mm(x_ref[...], w_ref[...])


def _proj(x, w, tm):
    m, kdim = x.shape
    n = w.shape[1]
    return pl.pallas_call(
        _proj_kernel,
        out_shape=jax.ShapeDtypeStruct((m, n), F32),
        grid=(m // tm,),
        in_specs=[pl.BlockSpec((tm, kdim), lambda i: (i, 0)), _const_spec((kdim, n))],
        out_specs=pl.BlockSpec((tm, n), lambda i: (i, 0)),
        compiler_params=_cparams(("parallel",)),
        name="proj",
    )(x, w)


def _rope(x, cos, sin):
    width = x.shape[1]
    lane = lax.broadcasted_iota(jnp.int32, x.shape, 1)
    first_half = (lane % ATT_HD) < (ATT_HD // 2)
    partner = jnp.where(first_half,
                        pltpu.roll(x, width - ATT_HD // 2, axis=1),
                        pltpu.roll(x, ATT_HD // 2, axis=1))
    return x * cos + partner * sin


def _in_proj(x_ref, g_ref, w_ref, cos_ref, sin_ref, conv_dim, att_dim):
    hn = _rms(x_ref[...], g_ref[...]).astype(BF16)
    c0 = conv_dim
    val = jnp.dot(hn, w_ref[:, 0:c0], preferred_element_type=F32)
    gate = jnp.dot(hn, w_ref[:, c0:2 * c0], preferred_element_type=F32)
    u = val * _sigmoid(gate)
    reps = att_dim // LANES
    cos = jnp.concatenate([cos_ref[...]] * reps, axis=1)
    sin = jnp.concatenate([sin_ref[...]] * reps, axis=1)
    o = 2 * c0
    q = _rope(jnp.dot(hn, w_ref[:, o:o + att_dim], preferred_element_type=F32), cos, sin)
    k = _rope(jnp.dot(hn, w_ref[:, o + att_dim:o + 2 * att_dim], preferred_element_type=F32), cos, sin)
    v = jnp.dot(hn, w_ref[:, o + 2 * att_dim:o + 3 * att_dim], preferred_element_type=F32)
    return u, q, k, v


def _front0_step_kernel(x_ref, g_ref, w_ref, cos_ref, sin_ref, u_ref, q_ref, k_ref, v_ref, *, conv_dim, att_dim):
    u_ref[...], q_ref[...], k_ref[...], v_ref[...] = _in_proj(x_ref, g_ref, w_ref, cos_ref, sin_ref,
                                                               conv_dim, att_dim)


def _front0_step(h, g, w_in, cos_tab, sin_tab, conv_dim, att_dim):
    m, d = h.shape
    whole = lambda shape: pl.BlockSpec(shape, lambda i: (0,) * len(shape))
    return pl.pallas_call(
        functools.partial(_front0_step_kernel, conv_dim=conv_dim, att_dim=att_dim),
        out_shape=(jax.ShapeDtypeStruct((m, conv_dim), F32),) + (jax.ShapeDtypeStruct((m, att_dim), F32),) * 3,
        grid=(1,),
        in_specs=[whole((m, d)), _const_spec((1, d)), _const_spec(w_in.shape), whole((m, LANES)), whole((m, LANES))],
        out_specs=(whole((m, conv_dim)),) + (whole((m, att_dim)),) * 3,
        compiler_params=_cparams(("arbitrary",)),
        name="front0_step",
    )(h, g, w_in, cos_tab, sin_tab)


def _front0_seq_kernel(x_ref, g_ref, w_ref, cos_ref, sin_ref, cw_ref, cb_ref, lg_ref, lb_ref,
                       c_ref, ulast_ref, q_ref, k_ref, v_ref, v_scr, full_scr, *, conv_dim, att_dim, tiles_per_seq):
    i = pl.program_id(0)
    tm = x_ref.shape[0]

    @pl.when(i % tiles_per_seq == 0)
    def _():
        full_scr[0:HALO, :] = jnp.zeros((HALO, conv_dim), F32)

    u, q, k, v = _in_proj(x_ref, g_ref, w_ref, cos_ref, sin_ref, conv_dim, att_dim)
    full_scr[HALO:HALO + tm, :] = u
    ulast_ref[0] = u[tm - HALO:, :]
    q_ref[...] = q
    k_ref[0] = k.T
    v_scr[...] = v
    v_ref[0] = v_scr[...].T
    for r0 in range(0, tm, CONV_ROWS):
        c_ref[r0:r0 + CONV_ROWS, :] = _conv_block(full_scr, cw_ref, cb_ref, lg_ref, lb_ref, r0, CONV_ROWS)
    full_scr[0:HALO, :] = full_scr[tm:tm + HALO, :]


def _front0_seq(h, g, w_in, cos_tab, sin_tab, conv_w, conv_b, ln_g, ln_b, att_dim, tm, seq_len):
    m, d = h.shape
    conv_dim = conv_w.shape[1]
    tps = seq_len // tm
    n_tab = cos_tab.shape[0] // tm
    row = lambda i: (i, 0)
    tab = lambda i: (i % n_tab, 0)
    kv_shape = jax.ShapeDtypeStruct((m // seq_len, att_dim, seq_len), F32)
    kv_spec = pl.BlockSpec((1, att_dim, tm), lambda i: (i // tps, 0, i % tps))
    wp = jnp.pad(conv_w, ((0, HALO - CONV_WIDTH), (0, 0)))
    return pl.pallas_call(
        functools.partial(_front0_seq_kernel, conv_dim=conv_dim, att_dim=att_dim, tiles_per_seq=tps),
        out_shape=(jax.ShapeDtypeStruct((m, conv_dim), F32), jax.ShapeDtypeStruct((m // seq_len, HALO, conv_dim), F32),
                   jax.ShapeDtypeStruct((m, att_dim), F32), kv_shape, kv_shape),
        grid=(m // tm,),
        in_specs=[pl.BlockSpec((tm, d), row), _const_spec((1, d)), _const_spec(w_in.shape),
                  pl.BlockSpec((tm, LANES), tab), pl.BlockSpec((tm, LANES), tab),
                  _const_spec((HALO, conv_dim)), _const_spec((1, conv_dim)), _const_spec((1, conv_dim)),
                  _const_spec((1, conv_dim))],
        out_specs=(pl.BlockSpec((tm, conv_dim), row), pl.BlockSpec((1, HALO, conv_dim), lambda i: (i // tps, 0, 0)),
                   pl.BlockSpec((tm, att_dim), row), kv_spec, kv_spec),
        scratch_shapes=[pltpu.VMEM((tm, att_dim), F32), pltpu.VMEM((HALO + tm, conv_dim), F32)],
        compiler_params=_cparams(("arbitrary",)),
        name="front0",
    )(h, g, w_in, cos_tab, sin_tab, wp, conv_b, ln_g, ln_b)


def _rope_tables(pos):
    half = ATT_HD // 2
    inv_freq = ROPE_THETA ** (-jnp.arange(half, dtype=F32) / half)
    ang = pos.astype(F32)[:, None] * inv_freq[None, :]
    cos, sin = jnp.cos(ang), jnp.sin(ang)
    reps = LANES // ATT_HD
    return (jnp.tile(jnp.concatenate([cos, cos], axis=1), (1, reps)),
            jnp.tile(jnp.concatenate([-sin, sin], axis=1), (1, reps)))


def _ln_silu(y, g, b):
    mu = jnp.mean(y, axis=-1, keepdims=True)
    yc = y - mu
    var = jnp.mean(yc * yc, axis=-1, keepdims=True)
    ln = yc * lax.rsqrt(var + LN_EPS) * g + b
    return ln * _sigmoid(ln)


def _conv_block(full_scr, w_ref, b_ref, lg_ref, lb_ref, r0, sub):
    first = HALO - (CONV_WIDTH - 1)
    acc = jnp.broadcast_to(b_ref[...], (sub, b_ref.shape[1]))
    for res in range(SUBLANES):
        rows = sub + (SUBLANES if res else 0)
        z = None
        for j in range(CONV_WIDTH):
            if (first + j) % SUBLANES != res:
                continue
            base = r0 + first + j - res
            term = w_ref[j:j + 1, :] * full_scr[base:base + rows, :]
            z = term if z is None else z + term
        acc = acc + z[res:res + sub, :]
    return _ln_silu(acc, lg_ref[...], lb_ref[...])


def _conv_step_kernel(st_ref, u_ref, w_ref, b_ref, lg_ref, lb_ref, c_ref):
    acc = b_ref[...] + w_ref[CONV_WIDTH - 1:CONV_WIDTH, :] * u_ref[...]
    for j in range(CONV_WIDTH - 1):
        acc = acc + w_ref[j:j + 1, :] * st_ref[j]
    c_ref[...] = _ln_silu(acc, lg_ref[...], lb_ref[...])


def _conv_step(state_t, u, w, b, lg, lb):
    bsz, c = u.shape
    wp = jnp.pad(w, ((0, HALO - CONV_WIDTH), (0, 0)))
    return pl.pallas_call(
        _conv_step_kernel,
        out_shape=jax.ShapeDtypeStruct((bsz, c), F32),
        grid=(1,),
        in_specs=[_const_spec(state_t.shape), _const_spec((bsz, c)), _const_spec((HALO, c)),
                  _const_spec((1, c)), _const_spec((1, c)), _const_spec((1, c))],
        out_specs=pl.BlockSpec((bsz, c), lambda i: (0, 0)),
        compiler_params=_cparams(("arbitrary",)),
        name="conv_step",
    )(state_t, u, wp, b, lg, lb)


def _top_blocks(gate, n_past):
    blk = lax.broadcasted_iota(jnp.int32, gate.shape, 0).astype(F32)
    g = jnp.where(blk < n_past.astype(F32), gate, NEG_INF)
    picks = []
    for r in range(MOBA_TOPK):
        m = jnp.max(g, axis=0, keepdims=True)
        idx = jnp.min(jnp.where(g == m, blk, 1e9), axis=0, keepdims=True)
        picks.append(jnp.where(r < n_past, idx, -1.0))
        g = jnp.where(blk == idx, -jnp.inf, g)
    return picks


def _moba_kernel(q_ref, k_ref, v_ref, o_ref, kmean_scr, kb_scr, vt_scr, s_scr, *, nb, scale, n_pairs):
    i = pl.program_id(2)
    blk = MOBA_BLOCK

    @pl.when(i == 0)
    def _():
        kmean_scr[...] = jnp.zeros_like(kmean_scr)
        for pr in range(n_pairs):
            feats = slice(pr * LANES, (pr + 1) * LANES)
            vt_scr[pr] = v_ref[0, feats, :].astype(BF16)
            for j in range(nb):
                rows = slice(j * blk, (j + 1) * blk)
                kj = k_ref[0, feats, rows].T
                kmean_scr[pr, j:j + 1, :] = jnp.mean(kj, axis=0, keepdims=True)
                kb_scr[pr, rows, :] = kj.astype(BF16)

    pairs = [_moba_pair(q_ref[0, :, pr * LANES:(pr + 1) * LANES], kmean_scr.at[pr], kb_scr.at[pr], vt_scr.at[pr],
                        s_scr.at[pr], i, nb, scale) for pr in range(n_pairs)]
    for pr in pairs:
        pr.prefetch(jnp.int32(0), 0)
    carry = [pr.start() for pr in pairs]

    def body(jj, carry):
        j = 2 * jj
        for pr in pairs:
            pr.prefetch(j + 1, 1)
        carry = [pr.fold(j, 0, c) for pr, c in zip(pairs, carry)]
        for pr in pairs:
            pr.prefetch(j + 2, 0)
        return [pr.fold(j + 1, 1, c) for pr, c in zip(pairs, carry)]

    carry = lax.fori_loop(0, (i + 1) // 2, body, carry)
    for n, (pr, c) in enumerate(zip(pairs, carry)):
        o_ref[0, :, n * LANES:(n + 1) * LANES] = pr.finish(c)


class _MobaPair:
    def __init__(self, prefetch, start, fold, finish):
        self.prefetch, self.start, self.fold, self.finish = prefetch, start, fold, finish


def _moba_pair(q_raw, kmean_scr, kb_scr, vt_scr, s_scr, i, nb, scale):
    blk = MOBA_BLOCK
    hpp = LANES // ATT_HD
    q = q_raw * (scale * LOG2_E)
    lane = lax.broadcasted_iota(jnp.int32, q.shape, 1)
    q2 = jnp.concatenate([jnp.where((lane >= hh * ATT_HD) & (lane < (hh + 1) * ATT_HD), q, 0.0)
                          for hh in range(hpp)], axis=0)
    q_hi, q_lo = _split(q2)
    km_hi, km_lo = _split(kmean_scr[...])
    nbp = km_hi.shape[0]
    nt = (((1,), (1,)), ((), ()))

    def offset(j):
        return pl.multiple_of(jnp.minimum(j, nb - 1) * blk, blk)

    start = offset(i)
    stacked = lax.dot_general(jnp.concatenate([km_hi, km_lo, kb_scr[pl.ds(start, blk), :]], axis=0), q_hi, nt,
                              preferred_element_type=F32)
    gate = (stacked[0:nbp] + stacked[nbp:2 * nbp]
            + lax.dot_general(km_hi, q_lo, nt, preferred_element_type=F32))
    s = stacked[2 * nbp:]
    picks = _top_blocks(gate, i)

    ones = jnp.ones((SUBLANES, blk), BF16)

    def pv(off, p):
        vt_j = vt_scr[:, pl.ds(off, blk)]
        upd = [jnp.dot(vt_j[hh * ATT_HD:(hh + 1) * ATT_HD, :], p[:, hh * blk:(hh + 1) * blk],
                       preferred_element_type=F32) for hh in range(hpp)]
        return upd, jnp.dot(ones, p, preferred_element_type=F32)[0:1, :]

    def scores(off):
        return lax.dot_general(kb_scr[pl.ds(off, blk), :], q_hi, nt, preferred_element_type=F32)

    def past_scores(j):
        jf = j.astype(F32)
        chosen = (picks[0] == jf) | (picks[1] == jf) | (picks[2] == jf)
        return jnp.where(chosen, scores(offset(j)), NEG_INF)

    def prefetch(j, slot):
        s_scr[slot] = past_scores(j)

    def start_state():
        key = lax.broadcasted_iota(jnp.int32, s.shape, 0)
        qry = lax.broadcasted_iota(jnp.int32, s.shape, 1) % blk
        s_own = jnp.where(key <= qry, s, NEG_INF)
        m0 = jnp.max(s_own, axis=0, keepdims=True)
        acc0, l0 = pv(start, jnp.exp2(s_own - m0).astype(BF16))
        return m0, l0, acc0

    def fold(j, slot, carry):
        m, l, acc = carry
        sj = s_scr[slot]
        m_new = jnp.maximum(m, jnp.max(sj, axis=0, keepdims=True))
        alpha = jnp.exp2(m - m_new)
        upd, psum = pv(offset(j), jnp.exp2(sj - m_new).astype(BF16))
        l = alpha * l + psum
        acc = [alpha[:, hh * blk:(hh + 1) * blk] * acc[hh] + upd[hh] for hh in range(hpp)]
        return m_new, l, acc

    def finish(carry):
        _, l, acc = carry
        out_t = jnp.concatenate([acc[hh] / l[:, hh * blk:(hh + 1) * blk] for hh in range(hpp)], axis=0)
        return out_t.T

    return _MobaPair(prefetch, start_state, fold, finish)


def _moba_prompt(q, kt, vt):
    bsz, s, a = q.shape
    nb = s // MOBA_BLOCK
    n_pairs = _largest_divisor(a // LANES, MOBA_PAIRS_PER_STEP)
    wide = n_pairs * LANES
    blk = pl.BlockSpec((1, MOBA_BLOCK, wide), lambda b, hp, i: (b, i, hp))
    seq = pl.BlockSpec((1, wide, s), lambda b, hp, i: (b, hp, 0))
    nb_pad = -(-nb // BF16_ROWS) * BF16_ROWS
    return pl.pallas_call(
        functools.partial(_moba_kernel, nb=nb, scale=ATT_HD ** -0.5, n_pairs=n_pairs),
        out_shape=jax.ShapeDtypeStruct((bsz, s, a), F32),
        grid=(bsz, a // wide, nb),
        in_specs=[blk, seq, seq],
        out_specs=blk,
        scratch_shapes=[pltpu.VMEM((n_pairs, nb_pad, LANES), F32), pltpu.VMEM((n_pairs, s, LANES), BF16),
                        pltpu.VMEM((n_pairs, LANES, s), BF16),
                        pltpu.VMEM((n_pairs, 2, MOBA_BLOCK, (LANES // ATT_HD) * MOBA_BLOCK), F32)],
        compiler_params=_cparams(("parallel", "parallel", "arbitrary")),
        name="moba_prompt",
    )(q, kt, vt)


def _moba_gate_kernel(pt_ref, q_ref, *refs, nb, pages_per_step, pages_per_block):
    del pt_ref
    page_refs = refs[:pages_per_step]
    sel_ref, qb_scr, g_scr = refs[pages_per_step:]
    j = pl.program_id(1)
    blocks_per_step = pages_per_step // pages_per_block

    @pl.when(j == 0)
    def _():
        qb_scr[...] = jnp.broadcast_to(q_ref[0], qb_scr.shape)

    qb = qb_scr[...]
    for bl in range(blocks_per_step):
        ksum = page_refs[bl * pages_per_block][0]
        for pg in range(1, pages_per_block):
            ksum = ksum + page_refs[bl * pages_per_block + pg][0]
        g_scr[j * blocks_per_step + bl] = jnp.sum(ksum * qb, axis=1)

    @pl.when(j == nb // blocks_per_step - 1)
    def _():
        gate = jnp.sum(g_scr[...], axis=2, keepdims=True) * (1.0 / MOBA_BLOCK)
        blk = lax.broadcasted_iota(jnp.int32, gate.shape, 0).astype(F32)
        lane = lax.broadcasted_iota(jnp.int32, sel_ref.shape[1:], 1)
        sel = jnp.zeros(sel_ref.shape[1:], F32)
        for r in range(MOBA_TOPK):
            m = jnp.max(gate, axis=0, keepdims=True)
            idx = jnp.min(jnp.where(gate == m, blk, 1e9), axis=0, keepdims=True)
            sel = jnp.where(lane == r, idx[0], sel)
            gate = jnp.where(blk == idx, -jnp.inf, gate)
        sel_ref[0] = sel.astype(jnp.int32)


def _moba_gate(page_table, q_col, cache_kt, nb, pages_per_block, pages_per_step):
    bsz, heads, hd, _ = q_col.shape
    page = cache_kt.shape[3]
    steps = nb * pages_per_block // pages_per_step

    def page_spec(w):
        return pl.BlockSpec((1, heads, hd, page), lambda b, j, pt: (pt[b, j * pages_per_step + w], 0, 0, 0))

    grid_spec = pltpu.PrefetchScalarGridSpec(
        num_scalar_prefetch=1,
        grid=(bsz, steps),
        in_specs=[pl.BlockSpec((1, heads, hd, 1), lambda b, j, pt: (b, 0, 0, 0))]
        + [page_spec(w) for w in range(pages_per_step)],
        out_specs=pl.BlockSpec((1, heads, LANES), lambda b, j, pt: (b, 0, 0)),
        scratch_shapes=[pltpu.VMEM((heads, hd, page), F32), pltpu.VMEM((nb, heads, page), F32)],
    )
    return pl.pallas_call(
        functools.partial(_moba_gate_kernel, nb=nb, pages_per_step=pages_per_step, pages_per_block=pages_per_block),
        out_shape=jax.ShapeDtypeStruct((bsz, heads, LANES), jnp.int32),
        grid_spec=grid_spec,
        compiler_params=_cparams(("parallel", "arbitrary")),
        name="moba_gate",
    )(page_table, q_col, *([cache_kt] * pages_per_step))


def _moba_decode_kernel(pt_ref, sel_ref, q_ref, kn_ref, vn_ref, *refs, n_pages, heads_per_step, scale):
    del pt_ref, sel_ref
    o_ref = refs[2 * n_pages * heads_per_step]
    for u in range(heads_per_step):
        k_refs = refs[u * n_pages:(u + 1) * n_pages]
        v_refs = refs[(heads_per_step + u) * n_pages:(heads_per_step + u + 1) * n_pages]
        q = q_ref[0, u]
        s_own = jnp.sum(q * kn_ref[0, u], axis=0, keepdims=True) * scale
        s = [jnp.sum(kr[0, 0] * q, axis=0, keepdims=True) * scale for kr in k_refs]
        m = s_own
        for sp in s:
            m = jnp.maximum(m, jnp.max(sp, axis=1, keepdims=True))
        p_own = jnp.exp(s_own - m)
        l = p_own
        acc = p_own * vn_ref[0, u]
        for sp, vr in zip(s, v_refs):
            p = jnp.exp(sp - m)
            l = l + jnp.sum(p, axis=1, keepdims=True)
            acc = acc + jnp.sum(vr[0, 0] * p, axis=1, keepdims=True)
        o_ref[0, u] = acc / l


def _moba_decode(page_table, sel, q_col, kn_col, vn_col, cache_kt, cache_vt, pages_per_block):
    bsz, heads, hd, _ = q_col.shape
    page = cache_kt.shape[3]
    n_pages = MOBA_TOPK * pages_per_block
    hps = _largest_divisor(heads, DECODE_HEADS_PER_STEP)
    vec = pl.BlockSpec((1, hps, hd, 1), lambda b, hg, pt, sl: (b, hg, 0, 0))

    def page_spec(u, r, w):
        return pl.BlockSpec(
            (1, 1, hd, page),
            lambda b, hg, pt, sl: (pt[b, pages_per_block * sl[b, hg * hps + u, r] + w], hg * hps + u, 0, 0))

    pages = [page_spec(u, r, w) for u in range(hps) for r in range(MOBA_TOPK) for w in range(pages_per_block)]
    grid_spec = pltpu.PrefetchScalarGridSpec(
        num_scalar_prefetch=2,
        grid=(bsz, heads // hps),
        in_specs=[vec, vec, vec] + pages + pages,
        out_specs=vec,
    )
    return pl.pallas_call(
        functools.partial(_moba_decode_kernel, n_pages=n_pages, heads_per_step=hps, scale=ATT_HD ** -0.5),
        out_shape=jax.ShapeDtypeStruct((bsz, heads, hd, 1), F32),
        grid_spec=grid_spec,
        compiler_params=_cparams(("parallel", "parallel")),
        name="moba_decode",
    )(page_table, sel, q_col, kn_col, vn_col, *([cache_kt] * len(pages)), *([cache_vt] * len(pages)))


def _tail_kernel(*refs, n_mix, nb, s, ff_chunk):
    h_ref = refs[0]
    mix_refs = refs[1:1 + n_mix]
    wmix_refs = refs[1 + n_mix:1 + 2 * n_mix]
    g_ref, wcq_ref, mk_ref, mv_ref, wco_ref, wup_ref, wdn_ref, o_ref, ca_scr = refs[1 + 2 * n_mix:]

    mix = _mm(mix_refs[0][...], wmix_refs[0][...])
    for mr, wr in zip(mix_refs[1:], wmix_refs[1:]):
        mix = mix + _mm(mr[...], wr[...])
    h = h_ref[...] + _rms(mix, g_ref[1:2, :])

    q = _mm(_rms(h, g_ref[2:3, :]), wcq_ref[...])
    rows = max(s, SUBLANES)
    mem = mk_ref.shape[1] // MEM_HEADS
    for bi in range(nb):
        qb = q[bi * s:(bi + 1) * s, :]
        if s < rows:
            qb = jnp.broadcast_to(qb[0:1, :], (rows, qb.shape[1]))
        for hh in range(MEM_HEADS):
            sl = slice(hh * MEM_HD, (hh + 1) * MEM_HD)
            tokens = pl.ds(hh, mem, stride=MEM_HEADS)
            sc = _mm_nt(qb[:, sl], mk_ref[bi, tokens, :]) * (MEM_HD ** -0.5)
            p = jnp.exp(sc - jnp.max(sc, axis=1, keepdims=True))
            oh = _mm(p, mv_ref[bi, tokens, :]) / jnp.sum(p, axis=1, keepdims=True)
            ca_scr[bi * s:(bi + 1) * s, sl] = oh[0:s, :]
    h = h + _rms(_mm(ca_scr[...], wco_ref[...]), g_ref[3:4, :])

    hn = _rms(h, g_ref[4:5, :]).astype(BF16)
    d_ff = wup_ref.shape[1]
    acc = jnp.zeros(h.shape, F32)
    for c0 in range(0, d_ff, ff_chunk):
        up = jnp.dot(hn, wup_ref[:, c0:c0 + ff_chunk], preferred_element_type=F32)
        up = jnp.square(jnp.maximum(up, 0.0))
        acc = acc + jnp.dot(up.astype(BF16), wdn_ref[c0:c0 + ff_chunk, :], preferred_element_type=F32)
    o_ref[...] = h + _rms(acc, g_ref[5:6, :])


def _tail(h, mixes, w_mixes, gains, w_cq, mk, mv, w_co, w_up, w_down, tm, seq_len, layer=0):
    m, d = h.shape
    s = min(seq_len, tm)
    nb = tm // s
    n_mix = len(mixes)
    mem_rows = mk.shape[-2]
    mem_dim = MEM_HEADS * MEM_HD
    row = lambda i: (i, 0)
    tiles_per_seq = max(seq_len // tm, 1)
    if mk.ndim == 4:
        mem_spec = pl.BlockSpec((None, nb, mem_rows, MEM_HD), lambda i: (layer, i // tiles_per_seq, 0, 0))
    else:
        mem_spec = pl.BlockSpec((nb, mem_rows, MEM_HD), lambda i: (i // tiles_per_seq, 0, 0))
    in_specs = [pl.BlockSpec((tm, d), row)]
    in_specs += [pl.BlockSpec((tm, x.shape[1]), row) for x in mixes]
    in_specs += [_const_spec(w.shape) for w in w_mixes]
    in_specs += [_const_spec(gains.shape), _const_spec(w_cq.shape),
                 mem_spec, mem_spec,
                 _const_spec(w_co.shape), _const_spec(w_up.shape), _const_spec(w_down.shape)]
    return pl.pallas_call(
        functools.partial(_tail_kernel, n_mix=n_mix, nb=nb, s=s, ff_chunk=min(FF_CHUNK, w_up.shape[1])),
        out_shape=jax.ShapeDtypeStruct((m, d), F32),
        grid=(m // tm,),
        in_specs=in_specs,
        out_specs=pl.BlockSpec((tm, d), row),
        scratch_shapes=[pltpu.VMEM((tm, mem_dim), F32)],
        compiler_params=_cparams(("parallel",)),
        name="tail",
    )(h, *mixes, *w_mixes, gains, w_cq, mk, mv, w_co, w_up, w_down)


def _front1_kernel(h_ref, hprev_ref, xlast_ref, g_ref, mu_ref, wr_ref, wk_ref, wv_ref, w0_ref, w1_ref, w2_ref,
                   a0_ref, a1_ref, a2_ref, g1_ref, g2_ref, kk_ref, ka_ref, seg_ref,
                   r_out, ld_out, k_out, v_out, kk_out, a_out, gt_out, sh_out, *, seq, tiles_per_seq):
    i = pl.program_id(0)
    g = g_ref[...]
    hn = _rms(h_ref[...], g)
    if seq:
        prev_row = _rms(hprev_ref[SUBLANES - 1:SUBLANES, :], g)
        prev_row = jnp.where(i % tiles_per_seq == 0, xlast_ref[0], prev_row)
        rowid = lax.broadcasted_iota(jnp.int32, hn.shape, 0)
        xprev = jnp.where(rowid == 0, prev_row, pltpu.roll(hn, 1, axis=0))
        sh_out[0] = hn[hn.shape[0] - 1:, :]
    else:
        xprev = xlast_ref[...]
        sh_out[...] = hn
    xx = xprev - hn
    mix = lambda n: hn + xx * mu_ref[n:n + 1, :]
    r = _mm(mix(0), wr_ref[...])
    k = _mm(mix(2), wk_ref[...])
    v = _mm(mix(3), wv_ref[...])
    wl = w0_ref[...] + _mm(jnp.tanh(_mm(mix(1), w1_ref[...])), w2_ref[...])
    z = -wl
    softplus = jnp.maximum(z, 0.0) + jnp.log(1.0 + jnp.exp(-jnp.abs(z)))
    ld_out[...] = -jnp.exp(-softplus - 0.5)
    a = _sigmoid(a0_ref[...] + _mm(_mm(mix(4), a1_ref[...]), a2_ref[...]))
    gt_out[...] = _mm(_sigmoid(_mm(mix(5), g1_ref[...])), g2_ref[...])
    kk = k * kk_ref[...]
    sq = kk * kk
    seg_w = seg_ref.shape[0]
    ss = jnp.concatenate([_mm2(sq[:, c:c + seg_w], seg_ref[...]) for c in range(0, sq.shape[1], seg_w)], axis=1)
    kk_out[...] = kk / jnp.maximum(jnp.sqrt(ss), 1e-12)
    k_out[...] = k * (1.0 + (a - 1.0) * ka_ref[...])
    r_out[...] = r
    v_out[...] = v
    a_out[...] = a


def _front1(h, x_last, g, p, tm, s):
    m, d = h.shape
    bsz = m // s
    seq = s > 1
    row = lambda i: (i, 0)
    big = jax.ShapeDtypeStruct((m, d), F32)
    if seq:
        tps = s // tm
        per = tm // SUBLANES
        hprev_spec = pl.BlockSpec((SUBLANES, d), lambda i: (jnp.maximum(i * per - 1, 0), 0))
        xlast = x_last.reshape(bsz, 1, d)
        xlast_spec = pl.BlockSpec((1, 1, d), lambda i: (i // tps, 0, 0))
        sh_shape = jax.ShapeDtypeStruct((bsz, 1, d), F32)
        sh_spec = pl.BlockSpec((1, 1, d), lambda i: (i // tps, 0, 0))
        sem = ("arbitrary",)
    else:
        tps = 1
        hprev_spec = pl.BlockSpec((SUBLANES, d), lambda i: (0, 0))
        xlast = x_last
        xlast_spec = pl.BlockSpec((tm, d), row)
        sh_shape = big
        sh_spec = pl.BlockSpec((tm, d), row)
        sem = ("parallel",)
    consts = [g, p["mu"], p["w_r"], p["w_k"], p["w_v"], p["w0"], p["w1"], p["w2"], p["a0"], p["a1"], p["a2"],
              p["g1"], p["g2"], p["k_k"], p["k_a"], p["seg"]]
    outs = pl.pallas_call(
        functools.partial(_front1_kernel, seq=seq, tiles_per_seq=tps),
        out_shape=(big,) * 7 + (sh_shape,),
        grid=(m // tm,),
        in_specs=[pl.BlockSpec((tm, d), row), hprev_spec, xlast_spec] + [_const_spec(c.shape) for c in consts],
        out_specs=(pl.BlockSpec((tm, d), row),) * 7 + (sh_spec,),
        compiler_params=_cparams(sem),
        name="front1",
    )(h, h, xlast, *consts)
    return outs[:7], outs[7].reshape(bsz, d)


def _scan_kernel(r_ref, ld_ref, k_ref, v_ref, kk_ref, a_ref, gt_ref, s0_ref, rk_ref, lg_ref, lb_ref,
                 o_ref, st_ref, s_scr, *, n_chunks, n_groups, n_seq):
    c = pl.program_id(1)
    grp = SCAN_GROUP
    w = grp * RWKV_HD
    sbd = (lax.broadcasted_iota(jnp.int32, (w, w), 0) // RWKV_HD
           == lax.broadcasted_iota(jnp.int32, (w, w), 1) // RWKV_HD)
    chains = [(bb, q) for bb in range(n_seq) for q in range(n_groups)]

    @pl.when(c == 0)
    def _():
        for g, (bb, q) in enumerate(chains):
            s0 = s0_ref[bb, q]
            s_scr[g] = jnp.where(sbd, jnp.concatenate([s0] * grp, axis=1), 0.0)

    seqs = lambda ref: jnp.concatenate([ref[bb] for bb in range(n_seq)], axis=1)
    rep = lambda ref: jnp.concatenate([ref[...]] * n_seq, axis=1)

    def write(g, val):
        bb, q = chains[g]
        o_ref[bb, :, q * w:(q + 1) * w] = val

    _scan_chunk(seqs(r_ref), seqs(ld_ref), seqs(k_ref), seqs(v_ref), seqs(kk_ref), seqs(a_ref), seqs(gt_ref),
                rep(rk_ref), rep(lg_ref), rep(lb_ref), write, s_scr, len(chains))

    @pl.when(c == n_chunks - 1)
    def _():
        for g, (bb, q) in enumerate(chains):
            full = s_scr[g]
            acc = full[:, 0:RWKV_HD]
            for hh in range(1, grp):
                acc = acc + full[:, hh * RWKV_HD:(hh + 1) * RWKV_HD]
            st_ref[bb, q] = acc


def _scan_chunk(r, ld, k, v, kk, a_gate, gt, r_k, lnx_g, lnx_b, write, s_scr, n_groups):
    ch = SCAN_CHUNK
    grp = SCAN_GROUP
    w = grp * RWKV_HD
    rows = grp * ch
    gs = range(n_groups)
    rhead = lax.broadcasted_iota(jnp.int32, (rows, w), 0) // ch
    chead = lax.broadcasted_iota(jnp.int32, (rows, w), 1) // RWKV_HD
    bd = rhead == chead
    tile_r = lambda x: jnp.concatenate([x] * grp, axis=0)
    diag = lambda x: jnp.where(bd, tile_r(x), 0.0)
    part = lambda x, q: x[:, q * w:(q + 1) * w]

    b = kk * a_gate
    trow = lax.broadcasted_iota(jnp.int32, ld.shape, 0)
    cum = ld
    sh = 1
    while sh < ch:
        cum = cum + jnp.where(trow >= sh, pltpu.roll(cum, sh, axis=0), 0.0)
        sh *= 2
    clast = cum[ch - 1:ch, :]
    p_inv = jnp.exp(-cum)
    p_to = jnp.exp(clast - cum)
    rt = r * jnp.exp(cum)
    at = -kk * jnp.exp(cum - ld)
    bt = b * p_inv
    kt = k * p_inv
    bh = b * p_to
    kh = k * p_to
    p_end = jnp.exp(clast)

    trow2 = lax.broadcasted_iota(jnp.int32, (ch, w), 0)
    tcol2 = lax.broadcasted_iota(jnp.int32, (ch, w), 1) % ch
    strict = tcol2 < trow2
    incl = tcol2 <= trow2
    eye = jnp.where(tcol2 == trow2, 1.0, 0.0)
    strict2 = jnp.concatenate([strict, strict], axis=1)
    incl2 = jnp.concatenate([incl, incl], axis=1)

    lhs = [jnp.concatenate([part(at, q), part(rt, q)], axis=0).astype(BF16) for q in gs]
    rhs = [jnp.concatenate([diag(part(bt, q)), diag(part(kt, q))], axis=0) for q in gs]
    mall = [_mm_nt(lhs[q], rhs[q]) for q in gs]
    la = [jnp.where(strict2, mall[q][0:ch], 0.0) for q in gs]
    lr = [jnp.where(incl2, mall[q][ch:], 0.0) for q in gs]
    lp = [la[q][:, 0:rows] for q in gs]
    mkk = [jnp.concatenate([la[q][:, rows:], lr[q][:, rows:]], axis=0) for q in gs]
    mrb = [lr[q][:, 0:rows] for q in gs]

    tinv = [eye + lp[q] for q in gs]
    lp = [_mm(lp[q], diag(lp[q])) for q in gs]
    n = 4
    while n < ch:
        both = [_mm(jnp.concatenate([lp[q], tinv[q]], axis=0), diag(lp[q])) for q in gs]
        lp = [both[q][0:ch] for q in gs]
        tinv = [tinv[q] + both[q][ch:] for q in gs]
        n *= 2
    tinv = [tinv[q] + _mm(tinv[q], diag(lp[q])) for q in gs]

    vd = [diag(part(v, q)).astype(BF16) for q in gs]
    state = [s_scr[q] for q in gs]
    s0t = [_mm_nt(lhs[q], state[q]) for q in gs]
    mv = [_mm(mkk[q], vd[q]) for q in gs]
    wmat = [s0t[q][0:ch] + mv[q][0:ch] for q in gs]
    u = [_mm(tinv[q], diag(wmat[q])) for q in gs]
    ys = [s0t[q][ch:] + _mm(mrb[q], diag(u[q])) + mv[q][ch:] for q in gs]
    upd = [_mm_tn(jnp.concatenate([u[q], part(v, q)], axis=0),
                  jnp.concatenate([part(bh, q), part(kh, q)], axis=0)) for q in gs]
    sbd = (lax.broadcasted_iota(jnp.int32, (w, w), 0) // RWKV_HD
           == lax.broadcasted_iota(jnp.int32, (w, w), 1) // RWKV_HD)
    for q in gs:
        s_scr[q] = state[q] * part(p_end, q) + jnp.where(sbd, upd[q], 0.0)

    seg = jnp.where(sbd, 1.0, 0.0).astype(BF16)
    rkr = r * k * r_k
    for q in gs:
        y = ys[q]
        mean = _mm2(y, seg) * (1.0 / RWKV_HD)
        yc = y - mean
        var = _mm2(yc * yc, seg) * (1.0 / RWKV_HD)
        yn = yc * lax.rsqrt(var + GN_EPS) * part(lnx_g, q) + part(lnx_b, q)
        bonus = _mm2(part(rkr, q), seg) * part(v, q)
        write(q, (yn + bonus) * part(gt, q))


def _scan(streams, gate, s0, r_k, lnx_g, lnx_b):
    bsz, t, d = gate.shape
    w = SCAN_GROUP * RWKV_HD
    nq = d // w
    nc = t // SCAN_CHUNK
    n_seq = _largest_divisor(bsz, SCAN_SEQS_PER_STEP)
    tok = pl.BlockSpec((n_seq, SCAN_CHUNK, d), lambda b, c: (b, c, 0))
    st = pl.BlockSpec((n_seq, nq, w, RWKV_HD), lambda b, c: (b, 0, 0, 0))
    s0g = s0.reshape(bsz, nq, w, RWKV_HD)
    out, s_t = pl.pallas_call(
        functools.partial(_scan_kernel, n_chunks=nc, n_groups=nq, n_seq=n_seq),
        out_shape=(jax.ShapeDtypeStruct((bsz, t, d), F32), jax.ShapeDtypeStruct(s0g.shape, F32)),
        grid=(bsz // n_seq, nc),
        in_specs=[tok] * 7 + [st, _const_spec((1, d)), _const_spec((1, d)), _const_spec((1, d))],
        out_specs=(tok, st),
        scratch_shapes=[pltpu.VMEM((n_seq * nq, w, w), F32)],
        compiler_params=_cparams(("parallel", "arbitrary")),
        name="rwkv_scan",
    )(*streams, gate, s0g, r_k, lnx_g, lnx_b)
    return out, s_t.reshape(s0.shape)


def _wkv_step_kernel(r_ref, ld_ref, k_ref, kk_ref, a_ref, v_ref, gt_ref, s_ref, rk_ref, lg_ref, lb_ref,
                     o_ref, st_ref):
    r, k, kk = r_ref[0], k_ref[0], kk_ref[0]
    v = v_ref[0]
    state = s_ref[0]
    sa = -jnp.sum(state * kk, axis=2, keepdims=True)
    state = state * jnp.exp(ld_ref[0]) + sa * (kk * a_ref[0]) + v * k
    st_ref[0] = state
    y = jnp.sum(state * r, axis=2, keepdims=True)
    mean = jnp.mean(y, axis=1, keepdims=True)
    yc = y - mean
    var = jnp.mean(yc * yc, axis=1, keepdims=True)
    yn = yc * lax.rsqrt(var + GN_EPS) * lg_ref[...] + lb_ref[...]
    bonus = jnp.sum(r * k * rk_ref[...], axis=2, keepdims=True) * v
    o_ref[0] = (yn + bonus) * gt_ref[0]


def _wkv_step(streams, gate, s0, r_k, lnx_g, lnx_b):
    r, ld, k, v, kk, a = streams
    bsz, d = r.shape
    n = RWKV_HD
    h = d // n
    row = lambda x: x.reshape(-1, h, 1, n)
    col = lambda x: x.reshape(-1, h, n, 1)
    row_spec = pl.BlockSpec((1, h, 1, n), lambda b: (b, 0, 0, 0))
    col_spec = pl.BlockSpec((1, h, n, 1), lambda b: (b, 0, 0, 0))
    st_spec = pl.BlockSpec((1, h, n, n), lambda b: (b, 0, 0, 0))
    out, s_t = pl.pallas_call(
        _wkv_step_kernel,
        out_shape=(jax.ShapeDtypeStruct((bsz, h, n, 1), F32), jax.ShapeDtypeStruct(s0.shape, F32)),
        grid=(bsz,),
        in_specs=[row_spec] * 5 + [col_spec] * 2 + [st_spec, _const_spec((h, 1, n)), _const_spec((h, n, 1)),
                                                   _const_spec((h, n, 1))],
        out_specs=(col_spec, st_spec),
        compiler_params=_cparams(("parallel",)),
        name="wkv_step",
    )(row(r), row(ld), row(k), row(kk), row(a), col(v), col(gate), s0,
      r_k.reshape(h, 1, n), lnx_g.reshape(h, n, 1), lnx_b.reshape(h, n, 1))
    return out.reshape(bsz, d), s_t


def _row_tile(m, cap):
    t = min(m, cap)
    assert m % t == 0
    return t


def _largest_divisor(n, cap):
    return max(t for t in range(1, cap + 1) if n % t == 0)


def _pad_cols(w, to):
    return jnp.pad(w, ((0, 0), (0, to - w.shape[1])))


def _pad_rows(w, to):
    return jnp.pad(w, ((0, to - w.shape[0]), (0, 0)))


def kernel(x_prompt, x_sample, mem_prompt, cache_k, cache_v, page_table, cache_mem_k, cache_mem_v, state_conv, state_shift, state_wkv, norm_gain, w_in, conv_w, conv_b, conv_ln_g, conv_ln_b, w_out, rwkv_mu, rwkv_w_r, rwkv_w_k, rwkv_w_v, rwkv_w_o, rwkv_w0, rwkv_w1, rwkv_w2, rwkv_a0, rwkv_a1, rwkv_a2, rwkv_g1, rwkv_g2, rwkv_k_k, rwkv_k_a, rwkv_r_k, rwkv_lnx_g, rwkv_lnx_b, w_cq, w_ck, w_cv, w_co, w_up, w_down):
    bp, sp, d = x_prompt.shape
    bs, ss, _ = x_sample.shape
    assert ss == 1
    depth = norm_gain.shape[0]
    conv_dim = conv_w.shape[2]
    att_dim = (w_in.shape[2] - 2 * conv_dim) // 3
    mem = mem_prompt.shape[1]
    mem_dim = w_ck.shape[2]
    n_pages, page = page_table.shape[1], cache_k.shape[2]
    past_len = n_pages * page
    pages_per_block = MOBA_BLOCK // page
    att_heads = att_dim // ATT_HD
    nb_past = past_len // MOBA_BLOCK
    assert past_len % MOBA_BLOCK == 0 and nb_past >= MOBA_TOPK and sp % MOBA_BLOCK == 0
    heads = d // RWKV_HD

    tm_p = _row_tile(sp, 512)
    tm_s = bs
    bf = lambda x: x.astype(BF16)

    w_mem = bf(jnp.concatenate([w for l in range(depth) for w in (w_ck[l], w_cv[l])], axis=1))
    memkv = _proj(mem_prompt.reshape(bp * mem, d), w_mem, _row_tile(bp * mem, 512))
    memkv = memkv.reshape(bp, mem, depth, 2, mem_dim)
    interleave = lambda x: x.reshape(x.shape[0], mem * MEM_HEADS, MEM_HD)
    mem_k_p = [interleave(memkv[:, :, l, 0]) for l in range(depth)]
    mem_v_p = [interleave(memkv[:, :, l, 1]) for l in range(depth)]

    cos_p, sin_p = _rope_tables(jnp.arange(sp, dtype=jnp.int32))
    cos_s, sin_s = _rope_tables(jnp.full((tm_s,), past_len, dtype=jnp.int32))

    hp = x_prompt.reshape(bp * sp, d)
    hs = x_sample.reshape(bs, d)
    k_p, v_p, k_s, v_s, conv_p, conv_s, shift_p, shift_s, wkv_p, wkv_s = ([] for _ in range(10))

    for l in range(depth):
        gains = norm_gain[l]
        g0 = gains[0:1]
        tail_w = (gains, bf(w_cq[l]))
        tail_w2 = (bf(w_co[l]), bf(w_up[l]), bf(w_down[l]))
        mk_s = cache_mem_k.reshape(depth, bs, mem * MEM_HEADS, MEM_HD)
        mv_s = cache_mem_v.reshape(depth, bs, mem * MEM_HEADS, MEM_HD)
        if l % 2 == 0:
            e = l // 2
            w_in_b = bf(w_in[e])
            w_o = bf(w_out[e])
            cb, lg, lb = conv_b[e][None], conv_ln_g[e][None], conv_ln_b[e][None]

            c, u_last, q, kt, vt = _front0_seq(hp, g0, w_in_b, cos_p, sin_p, conv_w[e], cb, lg, lb, att_dim, tm_p, sp)
            att = _moba_prompt(q.reshape(bp, sp, att_dim), kt, vt)
            hp = _tail(hp, [c, att.reshape(bp * sp, att_dim)],
                       [w_o[:conv_dim], w_o[conv_dim:]], *tail_w, mem_k_p[l], mem_v_p[l], *tail_w2, tm_p, sp)
            token_major = lambda x: jnp.transpose(x.reshape(bp, att_heads, ATT_HD, sp), (0, 3, 1, 2))
            k_p.append(token_major(kt))
            v_p.append(token_major(vt))
            conv_p.append(u_last[:, HALO - (CONV_WIDTH - 1):])

            u, q, k, v = _front0_step(hs, g0, w_in_b, cos_s, sin_s, conv_dim, att_dim)
            c = _conv_step(jnp.swapaxes(state_conv[e], 0, 1), u, conv_w[e], cb, lg, lb)
            ckt = jnp.transpose(cache_k[e], (0, 2, 3, 1))
            cvt = jnp.transpose(cache_v[e], (0, 2, 3, 1))
            pages_per_step = pages_per_block * _largest_divisor(nb_past, GATE_BLOCKS_PER_STEP)
            col = lambda x: x.reshape(bs, att_heads, ATT_HD, 1)
            sel = _moba_gate(page_table, col(q), ckt, nb_past, pages_per_block, pages_per_step)
            att = _moba_decode(page_table, sel[:, :, :MOBA_TOPK], col(q), col(k), col(v), ckt, cvt,
                               pages_per_block).reshape(bs, att_dim)
            hs = _tail(hs, [c, att], [w_o[:conv_dim], w_o[conv_dim:]], *tail_w, mk_s, mv_s, *tail_w2, tm_s, 1, l)
            k_s.append(k.reshape(bs, 1, att_dim // ATT_HD, ATT_HD))
            v_s.append(v.reshape(bs, 1, att_dim // ATT_HD, ATT_HD))
            conv_s.append(jnp.concatenate([state_conv[e][:, 1:], u[:, None]], axis=1))
        else:
            o = l // 2
            lora = LANES
            gl = MXU_DIM
            feat = jnp.arange(min(d, MXU_DIM)) // RWKV_HD
            prm = dict(
                mu=rwkv_mu[o], w_r=bf(rwkv_w_r[o]), w_k=bf(rwkv_w_k[o]), w_v=bf(rwkv_w_v[o]),
                w0=rwkv_w0[o][None], w1=bf(_pad_cols(rwkv_w1[o], lora)), w2=bf(_pad_rows(rwkv_w2[o], lora)),
                a0=rwkv_a0[o][None], a1=bf(_pad_cols(rwkv_a1[o], lora)), a2=bf(_pad_rows(rwkv_a2[o], lora)),
                g1=bf(_pad_cols(rwkv_g1[o], gl)), g2=bf(_pad_rows(rwkv_g2[o], gl)),
                k_k=rwkv_k_k[o][None], k_a=rwkv_k_a[o][None],
                seg=(feat[:, None] == feat[None, :]).astype(BF16))
            r_k = rwkv_r_k[o].reshape(1, d)
            lnx_g, lnx_b = rwkv_lnx_g[o][None], rwkv_lnx_b[o][None]
            w_o = bf(rwkv_w_o[o])

            streams, shift = _front1(hp, jnp.zeros((bp, d), F32), g0, prm, tm_p, sp)
            to3 = lambda x: x.reshape(bp, sp, d)
            gated, s_t = _scan([to3(x) for x in streams[:6]], to3(streams[6]),
                               jnp.zeros((bp, heads, RWKV_HD, RWKV_HD), F32), r_k, lnx_g, lnx_b)
            hp = _tail(hp, [gated.reshape(bp * sp, d)], [w_o], *tail_w, mem_k_p[l], mem_v_p[l], *tail_w2, tm_p, sp)
            shift_p.append(shift)
            wkv_p.append(s_t)

            streams, shift = _front1(hs, state_shift[o], g0, prm, tm_s, 1)
            gated, s_t = _wkv_step(streams[:6], streams[6], state_wkv[o], r_k, lnx_g, lnx_b)
            hs = _tail(hs, [gated], [w_o], *tail_w, mk_s, mv_s, *tail_w2, tm_s, 1, l)
            shift_s.append(shift)
            wkv_s.append(s_t)

    mem_shape = (bp, mem, MEM_HEADS, mem_dim // MEM_HEADS)
    return (hp.reshape(bp, sp, d), hs.reshape(bs, 1, d),
            jnp.stack(k_p), jnp.stack(v_p), jnp.stack(k_s), jnp.stack(v_s),
            jnp.stack(conv_p), jnp.stack(conv_s), jnp.stack(shift_p), jnp.stack(shift_s),
            jnp.stack(wkv_p), jnp.stack(wkv_s),
            jnp.stack([x.reshape(mem_shape) for x in mem_k_p]), jnp.stack([x.reshape(mem_shape) for x in mem_v_p]))
```

```python
import functools

import jax
import jax.numpy as jnp
from jax import lax
from jax.experimental import pallas as pl
from jax.experimental.pallas import tpu as pltpu

F32 = jnp.float32
BF16 = jnp.bfloat16

NORM_EPS = 1e-6
LN_EPS = 1e-5
GN_EPS = 64e-5
NEG_INF = -1e30
ATT_HD = 64
MEM_HEADS = 4
MEM_HD = 128
RWKV_HD = 64
MOBA_BLOCK = 256
MOBA_TOPK = 3
CONV_WIDTH = 31
ROPE_THETA = 10000.0

LANES = 128
SUBLANES = 8
BF16_ROWS = 16
MXU_DIM = 256
VMEM_LIMIT_BYTES = 56 * 1024 * 1024
SCAN_CHUNK = RWKV_HD
SCAN_GROUP = MXU_DIM // RWKV_HD
HALO = 32
CONV_ROWS = 256
FF_CHUNK = 1024
LOG2_E = 1.4426950408889634
SCAN_SEQS_PER_STEP = 4
MOBA_PAIRS_PER_STEP = 2
DECODE_HEADS_PER_STEP = 8
GATE_BLOCKS_PER_STEP = 16


def _cparams(sem):
    return pltpu.CompilerParams(dimension_semantics=sem, vmem_limit_bytes=VMEM_LIMIT_BYTES)


def _const_spec(shape):
    nd = len(shape)
    return pl.BlockSpec(shape, lambda *_: (0,) * nd, pipeline_mode=pl.Buffered(1))


def _rms(x, g):
    return x * lax.rsqrt(jnp.mean(x * x, axis=-1, keepdims=True) + NORM_EPS) * g


def _sigmoid(x):
    return 1.0 / (1.0 + jnp.exp(-x))


def _mm(a, b):
    return jnp.dot(a.astype(BF16), b.astype(BF16), preferred_element_type=F32)


def _mm_nt(a, b):
    return lax.dot_general(a.astype(BF16), b.astype(BF16), (((1,), (1,)), ((), ())),
                           preferred_element_type=F32)


def _mm_tn(a, b):
    return lax.dot_general(a.astype(BF16), b.astype(BF16), (((0,), (0,)), ((), ())),
                           preferred_element_type=F32)


def _split(x):
    hi = x.astype(BF16)
    lo = (x - hi.astype(F32)).astype(BF16)
    return hi, lo


def _mm2(a, b):
    hi, lo = _split(a)
    return (jnp.dot(hi, b, preferred_element_type=F32)
            + jnp.dot(lo, b, preferred_element_type=F32))


def _proj_kernel(x_ref, w_ref, o_ref):
    o_ref[...] = _mm(x_ref[...], w_ref[...])


def _proj(x, w, tm):
    m, kdim = x.shape
    n = w.shape[1]
    return pl.pallas_call(
        _proj_kernel,
        out_shape=jax.ShapeDtypeStruct((m, n), F32),
        grid=(m // tm,),
        in_specs=[pl.BlockSpec((tm, kdim), lambda i: (i, 0)), _const_spec((kdim, n))],
        out_specs=pl.BlockSpec((tm, n), lambda i: (i, 0)),
        compiler_params=_cparams(("parallel",)),
        name="proj",
    )(x, w)


def _rope(x, cos, sin):
    width = x.shape[1]
    lane = lax.broadcasted_iota(jnp.int32, x.shape, 1)
    first_half = (lane % ATT_HD) < (ATT_HD // 2)
    partner = jnp.where(first_half,
                        pltpu.roll(x, width - ATT_HD // 2, axis=1),
                        pltpu.roll(x, ATT_HD // 2, axis=1))
    return x * cos + partner * sin


def _in_proj(x_ref, g_ref, w_ref, cos_ref, sin_ref, conv_dim, att_dim):
    hn = _rms(x_ref[...], g_ref[...]).astype(BF16)
    c0 = conv_dim
    val = jnp.dot(hn, w_ref[:, 0:c0], preferred_element_type=F32)
    gate = jnp.dot(hn, w_ref[:, c0:2 * c0], preferred_element_type=F32)
    u = val * _sigmoid(gate)
    reps = att_dim // LANES
    cos = jnp.concatenate([cos_ref[...]] * reps, axis=1)
    sin = jnp.concatenate([sin_ref[...]] * reps, axis=1)
    o = 2 * c0
    q = _rope(jnp.dot(hn, w_ref[:, o:o + att_dim], preferred_element_type=F32), cos, sin)
    k = _rope(jnp.dot(hn, w_ref[:, o + att_dim:o + 2 * att_dim], preferred_element_type=F32), cos, sin)
    v = jnp.dot(hn, w_ref[:, o + 2 * att_dim:o + 3 * att_dim], preferred_element_type=F32)
    return u, q, k, v


def _front0_step_kernel(x_ref, g_ref, w_ref, cos_ref, sin_ref, u_ref, q_ref, k_ref, v_ref, *, conv_dim, att_dim):
    u_ref[...], q_ref[...], k_ref[...], v_ref[...] = _in_proj(x_ref, g_ref, w_ref, cos_ref, sin_ref,
                                                               conv_dim, att_dim)


def _front0_step(h, g, w_in, cos_tab, sin_tab, conv_dim, att_dim):
    m, d = h.shape
    whole = lambda shape: pl.BlockSpec(shape, lambda i: (0,) * len(shape))
    return pl.pallas_call(
        functools.partial(_front0_step_kernel, conv_dim=conv_dim, att_dim=att_dim),
        out_shape=(jax.ShapeDtypeStruct((m, conv_dim), F32),) + (jax.ShapeDtypeStruct((m, att_dim), F32),) * 3,
        grid=(1,),
        in_specs=[whole((m, d)), _const_spec((1, d)), _const_spec(w_in.shape), whole((m, LANES)), whole((m, LANES))],
        out_specs=(whole((m, conv_dim)),) + (whole((m, att_dim)),) * 3,
        compiler_params=_cparams(("arbitrary",)),
        name="front0_step",
    )(h, g, w_in, cos_tab, sin_tab)


def _front0_seq_kernel(x_ref, g_ref, w_ref, cos_ref, sin_ref, cw_ref, cb_ref, lg_ref, lb_ref,
                       c_ref, ulast_ref, q_ref, k_ref, v_ref, v_scr, full_scr, *, conv_dim, att_dim, tiles_per_seq):
    i = pl.program_id(0)
    tm = x_ref.shape[0]

    @pl.when(i % tiles_per_seq == 0)
    def _():
        full_scr[0:HALO, :] = jnp.zeros((HALO, conv_dim), F32)

    u, q, k, v = _in_proj(x_ref, g_ref, w_ref, cos_ref, sin_ref, conv_dim, att_dim)
    full_scr[HALO:HALO + tm, :] = u
    ulast_ref[0] = u[tm - HALO:, :]
    q_ref[...] = q
    k_ref[0] = k.T
    v_scr[...] = v
    v_ref[0] = v_scr[...].T
    for r0 in range(0, tm, CONV_ROWS):
        c_ref[r0:r0 + CONV_ROWS, :] = _conv_block(full_scr, cw_ref, cb_ref, lg_ref, lb_ref, r0, CONV_ROWS)
    full_scr[0:HALO, :] = full_scr[tm:tm + HALO, :]


def _front0_seq(h, g, w_in, cos_tab, sin_tab, conv_w, conv_b, ln_g, ln_b, att_dim, tm, seq_len):
    m, d = h.shape
    conv_dim = conv_w.shape[1]
    tps = seq_len // tm
    n_tab = cos_tab.shape[0] // tm
    row = lambda i: (i, 0)
    tab = lambda i: (i % n_tab, 0)
    kv_shape = jax.ShapeDtypeStruct((m // seq_len, att_dim, seq_len), F32)
    kv_spec = pl.BlockSpec((1, att_dim, tm), lambda i: (i // tps, 0, i % tps))
    wp = jnp.pad(conv_w, ((0, HALO - CONV_WIDTH), (0, 0)))
    return pl.pallas_call(
        functools.partial(_front0_seq_kernel, conv_dim=conv_dim, att_dim=att_dim, tiles_per_seq=tps),
        out_shape=(jax.ShapeDtypeStruct((m, conv_dim), F32), jax.ShapeDtypeStruct((m // seq_len, HALO, conv_dim), F32),
                   jax.ShapeDtypeStruct((m, att_dim), F32), kv_shape, kv_shape),
        grid=(m // tm,),
        in_specs=[pl.BlockSpec((tm, d), row), _const_spec((1, d)), _const_spec(w_in.shape),
                  pl.BlockSpec((tm, LANES), tab), pl.BlockSpec((tm, LANES), tab),
                  _const_spec((HALO, conv_dim)), _const_spec((1, conv_dim)), _const_spec((1, conv_dim)),
                  _const_spec((1, conv_dim))],
        out_specs=(pl.BlockSpec((tm, conv_dim), row), pl.BlockSpec((1, HALO, conv_dim), lambda i: (i // tps, 0, 0)),
                   pl.BlockSpec((tm, att_dim), row), kv_spec, kv_spec),
        scratch_shapes=[pltpu.VMEM((tm, att_dim), F32), pltpu.VMEM((HALO + tm, conv_dim), F32)],
        compiler_params=_cparams(("arbitrary",)),
        name="front0",
    )(h, g, w_in, cos_tab, sin_tab, wp, conv_b, ln_g, ln_b)


def _rope_tables(pos):
    half = ATT_HD // 2
    inv_freq = ROPE_THETA ** (-jnp.arange(half, dtype=F32) / half)
    ang = pos.astype(F32)[:, None] * inv_freq[None, :]
    cos, sin = jnp.cos(ang), jnp.sin(ang)
    reps = LANES // ATT_HD
    return (jnp.tile(jnp.concatenate([cos, cos], axis=1), (1, reps)),
            jnp.tile(jnp.concatenate([-sin, sin], axis=1), (1, reps)))


def _ln_silu(y, g, b):
    mu = jnp.mean(y, axis=-1, keepdims=True)
    yc = y - mu
    var = jnp.mean(yc * yc, axis=-1, keepdims=True)
    ln = yc * lax.rsqrt(var + LN_EPS) * g + b
    return ln * _sigmoid(ln)


def _conv_block(full_scr, w_ref, b_ref, lg_ref, lb_ref, r0, sub):
    first = HALO - (CONV_WIDTH - 1)
    acc = jnp.broadcast_to(b_ref[...], (sub, b_ref.shape[1]))
    for res in range(SUBLANES):
        rows = sub + (SUBLANES if res else 0)
        z = None
        for j in range(CONV_WIDTH):
            if (first + j) % SUBLANES != res:
                continue
            base = r0 + first + j - res
            term = w_ref[j:j + 1, :] * full_scr[base:base + rows, :]
            z = term if z is None else z + term
        acc = acc + z[res:res + sub, :]
    return _ln_silu(acc, lg_ref[...], lb_ref[...])


def _conv_step_kernel(st_ref, u_ref, w_ref, b_ref, lg_ref, lb_ref, c_ref):
    acc = b_ref[...] + w_ref[CONV_WIDTH - 1:CONV_WIDTH, :] * u_ref[...]
    for j in range(CONV_WIDTH - 1):
        acc = acc + w_ref[j:j + 1, :] * st_ref[j]
    c_ref[...] = _ln_silu(acc, lg_ref[...], lb_ref[...])


def _conv_step(state_t, u, w, b, lg, lb):
    bsz, c = u.shape
    wp = jnp.pad(w, ((0, HALO - CONV_WIDTH), (0, 0)))
    return pl.pallas_call(
        _conv_step_kernel,
        out_shape=jax.ShapeDtypeStruct((bsz, c), F32),
        grid=(1,),
        in_specs=[_const_spec(state_t.shape), _const_spec((bsz, c)), _const_spec((HALO, c)),
                  _const_spec((1, c)), _const_spec((1, c)), _const_spec((1, c))],
        out_specs=pl.BlockSpec((bsz, c), lambda i: (0, 0)),
        compiler_params=_cparams(("arbitrary",)),
        name="conv_step",
    )(state_t, u, wp, b, lg, lb)


def _top_blocks(gate, n_past):
    blk = lax.broadcasted_iota(jnp.int32, gate.shape, 0).astype(F32)
    g = jnp.where(blk < n_past.astype(F32), gate, NEG_INF)
    picks = []
    for r in range(MOBA_TOPK):
        m = jnp.max(g, axis=0, keepdims=True)
        idx = jnp.min(jnp.where(g == m, blk, 1e9), axis=0, keepdims=True)
        picks.append(jnp.where(r < n_past, idx, -1.0))
        g = jnp.where(blk == idx, -jnp.inf, g)
    return picks


def _moba_kernel(q_ref, k_ref, v_ref, o_ref, kmean_scr, kb_scr, vt_scr, s_scr, *, nb, scale, n_pairs):
    i = pl.program_id(2)
    blk = MOBA_BLOCK

    @pl.when(i == 0)
    def _():
        kmean_scr[...] = jnp.zeros_like(kmean_scr)
        for pr in range(n_pairs):
            feats = slice(pr * LANES, (pr + 1) * LANES)
            vt_scr[pr] = v_ref[0, feats, :].astype(BF16)
            for j in range(nb):
                rows = slice(j * blk, (j + 1) * blk)
                kj = k_ref[0, feats, rows].T
                kmean_scr[pr, j:j + 1, :] = jnp.mean(kj, axis=0, keepdims=True)
                kb_scr[pr, rows, :] = kj.astype(BF16)

    pairs = [_moba_pair(q_ref[0, :, pr * LANES:(pr + 1) * LANES], kmean_scr.at[pr], kb_scr.at[pr], vt_scr.at[pr],
                        s_scr.at[pr], i, nb, scale) for pr in range(n_pairs)]
    for pr in pairs:
        pr.prefetch(jnp.int32(0), 0)
    carry = [pr.start() for pr in pairs]

    def body(jj, carry):
        j = 2 * jj
        for pr in pairs:
            pr.prefetch(j + 1, 1)
        carry = [pr.fold(j, 0, c) for pr, c in zip(pairs, carry)]
        for pr in pairs:
            pr.prefetch(j + 2, 0)
        return [pr.fold(j + 1, 1, c) for pr, c in zip(pairs, carry)]

    carry = lax.fori_loop(0, (i + 1) // 2, body, carry)
    for n, (pr, c) in enumerate(zip(pairs, carry)):
        o_ref[0, :, n * LANES:(n + 1) * LANES] = pr.finish(c)


class _MobaPair:
    def __init__(self, prefetch, start, fold, finish):
        self.prefetch, self.start, self.fold, self.finish = prefetch, start, fold, finish


def _moba_pair(q_raw, kmean_scr, kb_scr, vt_scr, s_scr, i, nb, scale):
    blk = MOBA_BLOCK
    hpp = LANES // ATT_HD
    q = q_raw * (scale * LOG2_E)
    lane = lax.broadcasted_iota(jnp.int32, q.shape, 1)
    q2 = jnp.concatenate([jnp.where((lane >= hh * ATT_HD) & (lane < (hh + 1) * ATT_HD), q, 0.0)
                          for hh in range(hpp)], axis=0)
    q_hi, q_lo = _split(q2)
    km_hi, km_lo = _split(kmean_scr[...])
    nbp = km_hi.shape[0]
    nt = (((1,), (1,)), ((), ()))

    def offset(j):
        return pl.multiple_of(jnp.minimum(j, nb - 1) * blk, blk)

    start = offset(i)
    stacked = lax.dot_general(jnp.concatenate([km_hi, km_lo, kb_scr[pl.ds(start, blk), :]], axis=0), q_hi, nt,
                              preferred_element_type=F32)
    gate = (stacked[0:nbp] + stacked[nbp:2 * nbp]
            + lax.dot_general(km_hi, q_lo, nt, preferred_element_type=F32))
    s = stacked[2 * nbp:]
    picks = _top_blocks(gate, i)

    ones = jnp.ones((SUBLANES, blk), BF16)

    def pv(off, p):
        vt_j = vt_scr[:, pl.ds(off, blk)]
        upd = [jnp.dot(vt_j[hh * ATT_HD:(hh + 1) * ATT_HD, :], p[:, hh * blk:(hh + 1) * blk],
                       preferred_element_type=F32) for hh in range(hpp)]
        return upd, jnp.dot(ones, p, preferred_element_type=F32)[0:1, :]

    def scores(off):
        return lax.dot_general(kb_scr[pl.ds(off, blk), :], q_hi, nt, preferred_element_type=F32)

    def past_scores(j):
        jf = j.astype(F32)
        chosen = (picks[0] == jf) | (picks[1] == jf) | (picks[2] == jf)
        return jnp.where(chosen, scores(offset(j)), NEG_INF)

    def prefetch(j, slot):
        s_scr[slot] = past_scores(j)

    def start_state():
        key = lax.broadcasted_iota(jnp.int32, s.shape, 0)
        qry = lax.broadcasted_iota(jnp.int32, s.shape, 1) % blk
        s_own = jnp.where(key <= qry, s, NEG_INF)
        m0 = jnp.max(s_own, axis=0, keepdims=True)
        acc0, l0 = pv(start, jnp.exp2(s_own - m0).astype(BF16))
        return m0, l0, acc0

    def fold(j, slot, carry):
        m, l, acc = carry
        sj = s_scr[slot]
        m_new = jnp.maximum(m, jnp.max(sj, axis=0, keepdims=True))
        alpha = jnp.exp2(m - m_new)
        upd, psum = pv(offset(j), jnp.exp2(sj - m_new).astype(BF16))
        l = alpha * l + psum
        acc = [alpha[:, hh * blk:(hh + 1) * blk] * acc[hh] + upd[hh] for hh in range(hpp)]
        return m_new, l, acc

    def finish(carry):
        _, l, acc = carry
        out_t = jnp.concatenate([acc[hh] / l[:, hh * blk:(hh + 1) * blk] for hh in range(hpp)], axis=0)
        return out_t.T

    return _MobaPair(prefetch, start_state, fold, finish)


def _moba_prompt(q, kt, vt):
    bsz, s, a = q.shape
    nb = s // MOBA_BLOCK
    n_pairs = _largest_divisor(a // LANES, MOBA_PAIRS_PER_STEP)
    wide = n_pairs * LANES
    blk = pl.BlockSpec((1, MOBA_BLOCK, wide), lambda b, hp, i: (b, i, hp))
    seq = pl.BlockSpec((1, wide, s), lambda b, hp, i: (b, hp, 0))
    nb_pad = -(-nb // BF16_ROWS) * BF16_ROWS
    return pl.pallas_call(
        functools.partial(_moba_kernel, nb=nb, scale=ATT_HD ** -0.5, n_pairs=n_pairs),
        out_shape=jax.ShapeDtypeStruct((bsz, s, a), F32),
        grid=(bsz, a // wide, nb),
        in_specs=[blk, seq, seq],
        out_specs=blk,
        scratch_shapes=[pltpu.VMEM((n_pairs, nb_pad, LANES), F32), pltpu.VMEM((n_pairs, s, LANES), BF16),
                        pltpu.VMEM((n_pairs, LANES, s), BF16),
                        pltpu.VMEM((n_pairs, 2, MOBA_BLOCK, (LANES // ATT_HD) * MOBA_BLOCK), F32)],
        compiler_params=_cparams(("parallel", "parallel", "arbitrary")),
        name="moba_prompt",
    )(q, kt, vt)


def _moba_gate_kernel(pt_ref, q_ref, *refs, nb, pages_per_step, pages_per_block):
    del pt_ref
    page_refs = refs[:pages_per_step]
    sel_ref, qb_scr, g_scr = refs[pages_per_step:]
    j = pl.program_id(1)
    blocks_per_step = pages_per_step // pages_per_block

    @pl.when(j == 0)
    def _():
        qb_scr[...] = jnp.broadcast_to(q_ref[0], qb_scr.shape)

    qb = qb_scr[...]
    for bl in range(blocks_per_step):
        ksum = page_refs[bl * pages_per_block][0]
        for pg in range(1, pages_per_block):
            ksum = ksum + page_refs[bl * pages_per_block + pg][0]
        g_scr[j * blocks_per_step + bl] = jnp.sum(ksum * qb, axis=1)

    @pl.when(j == nb // blocks_per_step - 1)
    def _():
        gate = jnp.sum(g_scr[...], axis=2, keepdims=True) * (1.0 / MOBA_BLOCK)
        blk = lax.broadcasted_iota(jnp.int32, gate.shape, 0).astype(F32)
        lane = lax.broadcasted_iota(jnp.int32, sel_ref.shape[1:], 1)
        sel = jnp.zeros(sel_ref.shape[1:], F32)
        for r in range(MOBA_TOPK):
            m = jnp.max(gate, axis=0, keepdims=True)
            idx = jnp.min(jnp.where(gate == m, blk, 1e9), axis=0, keepdims=True)
            sel = jnp.where(lane == r, idx[0], sel)
            gate = jnp.where(blk == idx, -jnp.inf, gate)
        sel_ref[0] = sel.astype(jnp.int32)


def _moba_gate(page_table, q_col, cache_kt, nb, pages_per_block, pages_per_step):
    bsz, heads, hd, _ = q_col.shape
    page = cache_kt.shape[3]
    steps = nb * pages_per_block // pages_per_step

    def page_spec(w):
        return pl.BlockSpec((1, heads, hd, page), lambda b, j, pt: (pt[b, j * pages_per_step + w], 0, 0, 0))

    grid_spec = pltpu.PrefetchScalarGridSpec(
        num_scalar_prefetch=1,
        grid=(bsz, steps),
        in_specs=[pl.BlockSpec((1, heads, hd, 1), lambda b, j, pt: (b, 0, 0, 0))]
        + [page_spec(w) for w in range(pages_per_step)],
        out_specs=pl.BlockSpec((1, heads, LANES), lambda b, j, pt: (b, 0, 0)),
        scratch_shapes=[pltpu.VMEM((heads, hd, page), F32), pltpu.VMEM((nb, heads, page), F32)],
    )
    return pl.pallas_call(
        functools.partial(_moba_gate_kernel, nb=nb, pages_per_step=pages_per_step, pages_per_block=pages_per_block),
        out_shape=jax.ShapeDtypeStruct((bsz, heads, LANES), jnp.int32),
        grid_spec=grid_spec,
        compiler_params=_cparams(("parallel", "arbitrary")),
        name="moba_gate",
    )(page_table, q_col, *([cache_kt] * pages_per_step))


def _moba_decode_kernel(pt_ref, sel_ref, q_ref, kn_ref, vn_ref, *refs, n_pages, heads_per_step, scale):
    del pt_ref, sel_ref
    o_ref = refs[2 * n_pages * heads_per_step]
    hs = range(heads_per_step)
    k_refs = [refs[u * n_pages:(u + 1) * n_pages] for u in hs]
    v_refs = [refs[(heads_per_step + u) * n_pages:(heads_per_step + u + 1) * n_pages] for u in hs]
    q = [q_ref[0, u] for u in hs]
    s_own = [jnp.sum(q[u] * kn_ref[0, u], axis=0, keepdims=True) * scale for u in hs]
    s = [[jnp.sum(kr[0, 0] * q[u], axis=0, keepdims=True) * scale for kr in k_refs[u]] for u in hs]
    m = [functools.reduce(jnp.maximum, [jnp.max(sp, axis=1, keepdims=True) for sp in s[u]], s_own[u]) for u in hs]
    p_own = [jnp.exp(s_own[u] - m[u]) for u in hs]
    p = [[jnp.exp(sp - m[u]) for sp in s[u]] for u in hs]
    l = [p_own[u] + sum(jnp.sum(pp, axis=1, keepdims=True) for pp in p[u]) for u in hs]
    acc = [p_own[u] * vn_ref[0, u] + sum(jnp.sum(vr[0, 0] * pp, axis=1, keepdims=True)
                                         for pp, vr in zip(p[u], v_refs[u])) for u in hs]
    for u in hs:
        o_ref[0, u] = acc[u] / l[u]


def _moba_decode(page_table, sel, q_col, kn_col, vn_col, cache_kt, cache_vt, pages_per_block):
    bsz, heads, hd, _ = q_col.shape
    page = cache_kt.shape[3]
    n_pages = MOBA_TOPK * pages_per_block
    hps = _largest_divisor(heads, DECODE_HEADS_PER_STEP)
    vec = pl.BlockSpec((1, hps, hd, 1), lambda b, hg, pt, sl: (b, hg, 0, 0))

    def page_spec(u, r, w):
        return pl.BlockSpec(
            (1, 1, hd, page),
            lambda b, hg, pt, sl: (pt[b, pages_per_block * sl[b, hg * hps + u, r] + w], hg * hps + u, 0, 0))

    pages = [page_spec(u, r, w) for u in range(hps) for r in range(MOBA_TOPK) for w in range(pages_per_block)]
    grid_spec = pltpu.PrefetchScalarGridSpec(
        num_scalar_prefetch=2,
        grid=(bsz, heads // hps),
        in_specs=[vec, vec, vec] + pages + pages,
        out_specs=vec,
    )
    return pl.pallas_call(
        functools.partial(_moba_decode_kernel, n_pages=n_pages, heads_per_step=hps, scale=ATT_HD ** -0.5),
        out_shape=jax.ShapeDtypeStruct((bsz, heads, hd, 1), F32),
        grid_spec=grid_spec,
        compiler_params=_cparams(("parallel", "parallel")),
        name="moba_decode",
    )(page_table, sel, q_col, kn_col, vn_col, *([cache_kt] * len(pages)), *([cache_vt] * len(pages)))


def _tail_kernel(*refs, n_mix, nb, s, ff_chunk):
    h_ref = refs[0]
    mix_refs = refs[1:1 + n_mix]
    wmix_refs = refs[1 + n_mix:1 + 2 * n_mix]
    g_ref, wcq_ref, mk_ref, mv_ref, wco_ref, wup_ref, wdn_ref, o_ref, ca_scr = refs[1 + 2 * n_mix:]

    mix = _mm(mix_refs[0][...], wmix_refs[0][...])
    for mr, wr in zip(mix_refs[1:], wmix_refs[1:]):
        mix = mix + _mm(mr[...], wr[...])
    h = h_ref[...] + _rms(mix, g_ref[1:2, :])

    q = _mm(_rms(h, g_ref[2:3, :]), wcq_ref[...])
    rows = max(s, SUBLANES)
    mem = mk_ref.shape[1] // MEM_HEADS
    for bi in range(nb):
        qb = q[bi * s:(bi + 1) * s, :]
        if s < rows:
            qb = jnp.broadcast_to(qb[0:1, :], (rows, qb.shape[1]))
        for hh in range(MEM_HEADS):
            sl = slice(hh * MEM_HD, (hh + 1) * MEM_HD)
            tokens = pl.ds(hh, mem, stride=MEM_HEADS)
            sc = _mm_nt(qb[:, sl], mk_ref[bi, tokens, :]) * (MEM_HD ** -0.5)
            p = jnp.exp(sc - jnp.max(sc, axis=1, keepdims=True))
            oh = _mm(p, mv_ref[bi, tokens, :]) / jnp.sum(p, axis=1, keepdims=True)
            ca_scr[bi * s:(bi + 1) * s, sl] = oh[0:s, :]
    h = h + _rms(_mm(ca_scr[...], wco_ref[...]), g_ref[3:4, :])

    hn = _rms(h, g_ref[4:5, :]).astype(BF16)
    d_ff = wup_ref.shape[1]
    acc = jnp.zeros(h.shape, F32)
    for c0 in range(0, d_ff, ff_chunk):
        up = jnp.dot(hn, wup_ref[:, c0:c0 + ff_chunk], preferred_element_type=F32)
        up = jnp.square(jnp.maximum(up, 0.0))
        acc = acc + jnp.dot(up.astype(BF16), wdn_ref[c0:c0 + ff_chunk, :], preferred_element_type=F32)
    o_ref[...] = h + _rms(acc, g_ref[5:6, :])


def _tail(h, mixes, w_mixes, gains, w_cq, mk, mv, w_co, w_up, w_down, tm, seq_len, layer=0):
    m, d = h.shape
    s = min(seq_len, tm)
    nb = tm // s
    n_mix = len(mixes)
    mem_rows = mk.shape[-2]
    mem_dim = MEM_HEADS * MEM_HD
    row = lambda i: (i, 0)
    tiles_per_seq = max(seq_len // tm, 1)
    if mk.ndim == 4:
        mem_spec = pl.BlockSpec((None, nb, mem_rows, MEM_HD), lambda i: (layer, i // tiles_per_seq, 0, 0))
    else:
        mem_spec = pl.BlockSpec((nb, mem_rows, MEM_HD), lambda i: (i // tiles_per_seq, 0, 0))
    in_specs = [pl.BlockSpec((tm, d), row)]
    in_specs += [pl.BlockSpec((tm, x.shape[1]), row) for x in mixes]
    in_specs += [_const_spec(w.shape) for w in w_mixes]
    in_specs += [_const_spec(gains.shape), _const_spec(w_cq.shape),
                 mem_spec, mem_spec,
                 _const_spec(w_co.shape), _const_spec(w_up.shape), _const_spec(w_down.shape)]
    return pl.pallas_call(
        functools.partial(_tail_kernel, n_mix=n_mix, nb=nb, s=s, ff_chunk=min(FF_CHUNK, w_up.shape[1])),
        out_shape=jax.ShapeDtypeStruct((m, d), F32),
        grid=(m // tm,),
        in_specs=in_specs,
        out_specs=pl.BlockSpec((tm, d), row),
        scratch_shapes=[pltpu.VMEM((tm, mem_dim), F32)],
        compiler_params=_cparams(("parallel",)),
        name="tail",
    )(h, *mixes, *w_mixes, gains, w_cq, mk, mv, w_co, w_up, w_down)


def _front1_kernel(h_ref, hprev_ref, xlast_ref, g_ref, mu_ref, wr_ref, wk_ref, wv_ref, w0_ref, w1_ref, w2_ref,
                   a0_ref, a1_ref, a2_ref, g1_ref, g2_ref, kk_ref, ka_ref, seg_ref,
                   r_out, ld_out, k_out, v_out, kk_out, a_out, gt_out, sh_out, *, seq, tiles_per_seq):
    i = pl.program_id(0)
    g = g_ref[...]
    hn = _rms(h_ref[...], g)
    if seq:
        prev_row = _rms(hprev_ref[SUBLANES - 1:SUBLANES, :], g)
        prev_row = jnp.where(i % tiles_per_seq == 0, xlast_ref[0], prev_row)
        rowid = lax.broadcasted_iota(jnp.int32, hn.shape, 0)
        xprev = jnp.where(rowid == 0, prev_row, pltpu.roll(hn, 1, axis=0))
        sh_out[0] = hn[hn.shape[0] - 1:, :]
    else:
        xprev = xlast_ref[...]
        sh_out[...] = hn
    xx = xprev - hn
    mix = lambda n: hn + xx * mu_ref[n:n + 1, :]
    r = _mm(mix(0), wr_ref[...])
    k = _mm(mix(2), wk_ref[...])
    v = _mm(mix(3), wv_ref[...])
    wl = w0_ref[...] + _mm(jnp.tanh(_mm(mix(1), w1_ref[...])), w2_ref[...])
    z = -wl
    softplus = jnp.maximum(z, 0.0) + jnp.log(1.0 + jnp.exp(-jnp.abs(z)))
    ld_out[...] = -jnp.exp(-softplus - 0.5)
    a = _sigmoid(a0_ref[...] + _mm(_mm(mix(4), a1_ref[...]), a2_ref[...]))
    gt_out[...] = _mm(_sigmoid(_mm(mix(5), g1_ref[...])), g2_ref[...])
    kk = k * kk_ref[...]
    sq = kk * kk
    seg_w = seg_ref.shape[0]
    ss = jnp.concatenate([_mm2(sq[:, c:c + seg_w], seg_ref[...]) for c in range(0, sq.shape[1], seg_w)], axis=1)
    kk_out[...] = kk / jnp.maximum(jnp.sqrt(ss), 1e-12)
    k_out[...] = k * (1.0 + (a - 1.0) * ka_ref[...])
    r_out[...] = r
    v_out[...] = v
    a_out[...] = a


def _front1(h, x_last, g, p, tm, s):
    m, d = h.shape
    bsz = m // s
    seq = s > 1
    row = lambda i: (i, 0)
    big = jax.ShapeDtypeStruct((m, d), F32)
    if seq:
        tps = s // tm
        per = tm // SUBLANES
        hprev_spec = pl.BlockSpec((SUBLANES, d), lambda i: (jnp.maximum(i * per - 1, 0), 0))
        xlast = x_last.reshape(bsz, 1, d)
        xlast_spec = pl.BlockSpec((1, 1, d), lambda i: (i // tps, 0, 0))
        sh_shape = jax.ShapeDtypeStruct((bsz, 1, d), F32)
        sh_spec = pl.BlockSpec((1, 1, d), lambda i: (i // tps, 0, 0))
        sem = ("arbitrary",)
    else:
        tps = 1
        hprev_spec = pl.BlockSpec((SUBLANES, d), lambda i: (0, 0))
        xlast = x_last
        xlast_spec = pl.BlockSpec((tm, d), row)
        sh_shape = big
        sh_spec = pl.BlockSpec((tm, d), row)
        sem = ("parallel",)
    consts = [g, p["mu"], p["w_r"], p["w_k"], p["w_v"], p["w0"], p["w1"], p["w2"], p["a0"], p["a1"], p["a2"],
              p["g1"], p["g2"], p["k_k"], p["k_a"], p["seg"]]
    outs = pl.pallas_call(
        functools.partial(_front1_kernel, seq=seq, tiles_per_seq=tps),
        out_shape=(big,) * 7 + (sh_shape,),
        grid=(m // tm,),
        in_specs=[pl.BlockSpec((tm, d), row), hprev_spec, xlast_spec] + [_const_spec(c.shape) for c in consts],
        out_specs=(pl.BlockSpec((tm, d), row),) * 7 + (sh_spec,),
        compiler_params=_cparams(sem),
        name="front1",
    )(h, h, xlast, *consts)
    return outs[:7], outs[7].reshape(bsz, d)


def _scan_kernel(r_ref, ld_ref, k_ref, v_ref, kk_ref, a_ref, gt_ref, s0_ref, rk_ref, lg_ref, lb_ref,
                 o_ref, st_ref, s_scr, *, n_chunks, n_groups, n_seq):
    c = pl.program_id(1)
    grp = SCAN_GROUP
    w = grp * RWKV_HD
    sbd = (lax.broadcasted_iota(jnp.int32, (w, w), 0) // RWKV_HD
           == lax.broadcasted_iota(jnp.int32, (w, w), 1) // RWKV_HD)
    chains = [(bb, q) for bb in range(n_seq) for q in range(n_groups)]

    @pl.when(c == 0)
    def _():
        for g, (bb, q) in enumerate(chains):
            s0 = s0_ref[bb, q]
            s_scr[g] = jnp.where(sbd, jnp.concatenate([s0] * grp, axis=1), 0.0)

    seqs = lambda ref: jnp.concatenate([ref[bb] for bb in range(n_seq)], axis=1)
    rep = lambda ref: jnp.concatenate([ref[...]] * n_seq, axis=1)

    def write(g, val):
        bb, q = chains[g]
        o_ref[bb, :, q * w:(q + 1) * w] = val

    _scan_chunk(seqs(r_ref), seqs(ld_ref), seqs(k_ref), seqs(v_ref), seqs(kk_ref), seqs(a_ref), seqs(gt_ref),
                rep(rk_ref), rep(lg_ref), rep(lb_ref), write, s_scr, len(chains))

    @pl.when(c == n_chunks - 1)
    def _():
        for g, (bb, q) in enumerate(chains):
            full = s_scr[g]
            acc = full[:, 0:RWKV_HD]
            for hh in range(1, grp):
                acc = acc + full[:, hh * RWKV_HD:(hh + 1) * RWKV_HD]
            st_ref[bb, q] = acc


def _scan_chunk(r, ld, k, v, kk, a_gate, gt, r_k, lnx_g, lnx_b, write, s_scr, n_groups):
    ch = SCAN_CHUNK
    grp = SCAN_GROUP
    w = grp * RWKV_HD
    rows = grp * ch
    gs = range(n_groups)
    rhead = lax.broadcasted_iota(jnp.int32, (rows, w), 0) // ch
    chead = lax.broadcasted_iota(jnp.int32, (rows, w), 1) // RWKV_HD
    bd = rhead == chead
    tile_r = lambda x: jnp.concatenate([x] * grp, axis=0)
    diag = lambda x: jnp.where(bd, tile_r(x), 0.0)
    part = lambda x, q: x[:, q * w:(q + 1) * w]

    b = kk * a_gate
    trow = lax.broadcasted_iota(jnp.int32, ld.shape, 0)
    cum = ld
    sh = 1
    while sh < ch:
        cum = cum + jnp.where(trow >= sh, pltpu.roll(cum, sh, axis=0), 0.0)
        sh *= 2
    clast = cum[ch - 1:ch, :]
    p_inv = jnp.exp(-cum)
    p_to = jnp.exp(clast - cum)
    rt = r * jnp.exp(cum)
    at = -kk * jnp.exp(cum - ld)
    bt = b * p_inv
    kt = k * p_inv
    bh = b * p_to
    kh = k * p_to
    p_end = jnp.exp(clast)

    trow2 = lax.broadcasted_iota(jnp.int32, (ch, w), 0)
    tcol2 = lax.broadcasted_iota(jnp.int32, (ch, w), 1) % ch
    strict = tcol2 < trow2
    incl = tcol2 <= trow2
    eye = jnp.where(tcol2 == trow2, 1.0, 0.0)
    strict2 = jnp.concatenate([strict, strict], axis=1)
    incl2 = jnp.concatenate([incl, incl], axis=1)

    lhs = [jnp.concatenate([part(at, q), part(rt, q)], axis=0).astype(BF16) for q in gs]
    rhs = [jnp.concatenate([diag(part(bt, q)), diag(part(kt, q))], axis=0) for q in gs]
    mall = [_mm_nt(lhs[q], rhs[q]) for q in gs]
    la = [jnp.where(strict2, mall[q][0:ch], 0.0) for q in gs]
    lr = [jnp.where(incl2, mall[q][ch:], 0.0) for q in gs]
    lp = [la[q][:, 0:rows] for q in gs]
    mkk = [jnp.concatenate([la[q][:, rows:], lr[q][:, rows:]], axis=0) for q in gs]
    mrb = [lr[q][:, 0:rows] for q in gs]

    tinv = [eye + lp[q] for q in gs]
    lp = [_mm(lp[q], diag(lp[q])) for q in gs]
    n = 4
    while n < ch:
        both = [_mm(jnp.concatenate([lp[q], tinv[q]], axis=0), diag(lp[q])) for q in gs]
        lp = [both[q][0:ch] for q in gs]
        tinv = [tinv[q] + both[q][ch:] for q in gs]
        n *= 2
    tinv = [tinv[q] + _mm(tinv[q], diag(lp[q])) for q in gs]

    vd = [diag(part(v, q)).astype(BF16) for q in gs]
    state = [s_scr[q] for q in gs]
    s0t = [_mm_nt(lhs[q], state[q]) for q in gs]
    mv = [_mm(mkk[q], vd[q]) for q in gs]
    wmat = [s0t[q][0:ch] + mv[q][0:ch] for q in gs]
    u = [_mm(tinv[q], diag(wmat[q])) for q in gs]
    ys = [s0t[q][ch:] + _mm(mrb[q], diag(u[q])) + mv[q][ch:] for q in gs]
    upd = [_mm_tn(jnp.concatenate([u[q], part(v, q)], axis=0),
                  jnp.concatenate([part(bh, q), part(kh, q)], axis=0)) for q in gs]
    sbd = (lax.broadcasted_iota(jnp.int32, (w, w), 0) // RWKV_HD
           == lax.broadcasted_iota(jnp.int32, (w, w), 1) // RWKV_HD)
    for q in gs:
        s_scr[q] = state[q] * part(p_end, q) + jnp.where(sbd, upd[q], 0.0)

    seg = jnp.where(sbd, 1.0, 0.0).astype(BF16)
    rkr = r * k * r_k
    for q in gs:
        y = ys[q]
        mean = _mm2(y, seg) * (1.0 / RWKV_HD)
        yc = y - mean
        var = _mm2(yc * yc, seg) * (1.0 / RWKV_HD)
        yn = yc * lax.rsqrt(var + GN_EPS) * part(lnx_g, q) + part(lnx_b, q)
        bonus = _mm2(part(rkr, q), seg) * part(v, q)
        write(q, (yn + bonus) * part(gt, q))


def _scan(streams, gate, s0, r_k, lnx_g, lnx_b):
    bsz, t, d = gate.shape
    w = SCAN_GROUP * RWKV_HD
    nq = d // w
    nc = t // SCAN_CHUNK
    n_seq = _largest_divisor(bsz, SCAN_SEQS_PER_STEP)
    tok = pl.BlockSpec((n_seq, SCAN_CHUNK, d), lambda b, c: (b, c, 0))
    st = pl.BlockSpec((n_seq, nq, w, RWKV_HD), lambda b, c: (b, 0, 0, 0))
    s0g = s0.reshape(bsz, nq, w, RWKV_HD)
    out, s_t = pl.pallas_call(
        functools.partial(_scan_kernel, n_chunks=nc, n_groups=nq, n_seq=n_seq),
        out_shape=(jax.ShapeDtypeStruct((bsz, t, d), F32), jax.ShapeDtypeStruct(s0g.shape, F32)),
        grid=(bsz // n_seq, nc),
        in_specs=[tok] * 7 + [st, _const_spec((1, d)), _const_spec((1, d)), _const_spec((1, d))],
        out_specs=(tok, st),
        scratch_shapes=[pltpu.VMEM((n_seq * nq, w, w), F32)],
        compiler_params=_cparams(("parallel", "arbitrary")),
        name="rwkv_scan",
    )(*streams, gate, s0g, r_k, lnx_g, lnx_b)
    return out, s_t.reshape(s0.shape)


def _wkv_step_kernel(r_ref, ld_ref, k_ref, kk_ref, a_ref, v_ref, gt_ref, s_ref, rk_ref, lg_ref, lb_ref,
                     o_ref, st_ref):
    r, k, kk = r_ref[0], k_ref[0], kk_ref[0]
    v = v_ref[0]
    state = s_ref[0]
    sa = -jnp.sum(state * kk, axis=2, keepdims=True)
    state = state * jnp.exp(ld_ref[0]) + sa * (kk * a_ref[0]) + v * k
    st_ref[0] = state
    y = jnp.sum(state * r, axis=2, keepdims=True)
    mean = jnp.mean(y, axis=1, keepdims=True)
    yc = y - mean
    var = jnp.mean(yc * yc, axis=1, keepdims=True)
    yn = yc * lax.rsqrt(var + GN_EPS) * lg_ref[...] + lb_ref[...]
    bonus = jnp.sum(r * k * rk_ref[...], axis=2, keepdims=True) * v
    o_ref[0] = (yn + bonus) * gt_ref[0]


def _wkv_step(streams, gate, s0, r_k, lnx_g, lnx_b):
    r, ld, k, v, kk, a = streams
    bsz, d = r.shape
    n = RWKV_HD
    h = d // n
    row = lambda x: x.reshape(-1, h, 1, n)
    col = lambda x: x.reshape(-1, h, n, 1)
    row_spec = pl.BlockSpec((1, h, 1, n), lambda b: (b, 0, 0, 0))
    col_spec = pl.BlockSpec((1, h, n, 1), lambda b: (b, 0, 0, 0))
    st_spec = pl.BlockSpec((1, h, n, n), lambda b: (b, 0, 0, 0))
    out, s_t = pl.pallas_call(
        _wkv_step_kernel,
        out_shape=(jax.ShapeDtypeStruct((bsz, h, n, 1), F32), jax.ShapeDtypeStruct(s0.shape, F32)),
        grid=(bsz,),
        in_specs=[row_spec] * 5 + [col_spec] * 2 + [st_spec, _const_spec((h, 1, n)), _const_spec((h, n, 1)),
                                                   _const_spec((h, n, 1))],
        out_specs=(col_spec, st_spec),
        compiler_params=_cparams(("parallel",)),
        name="wkv_step",
    )(row(r), row(ld), row(k), row(kk), row(a), col(v), col(gate), s0,
      r_k.reshape(h, 1, n), lnx_g.reshape(h, n, 1), lnx_b.reshape(h, n, 1))
    return out.reshape(bsz, d), s_t


def _row_tile(m, cap):
    t = min(m, cap)
    assert m % t == 0
    return t


def _largest_divisor(n, cap):
    return max(t for t in range(1, cap + 1) if n % t == 0)


def _pad_cols(w, to):
    return jnp.pad(w, ((0, 0), (0, to - w.shape[1])))


def _pad_rows(w, to):
    return jnp.pad(w, ((0, to - w.shape[0]), (0, 0)))


def kernel(x_prompt, x_sample, mem_prompt, cache_k, cache_v, page_table, cache_mem_k, cache_mem_v, state_conv, state_shift, state_wkv, norm_gain, w_in, conv_w, conv_b, conv_ln_g, conv_ln_b, w_out, rwkv_mu, rwkv_w_r, rwkv_w_k, rwkv_w_v, rwkv_w_o, rwkv_w0, rwkv_w1, rwkv_w2, rwkv_a0, rwkv_a1, rwkv_a2, rwkv_g1, rwkv_g2, rwkv_k_k, rwkv_k_a, rwkv_r_k, rwkv_lnx_g, rwkv_lnx_b, w_cq, w_ck, w_cv, w_co, w_up, w_down):
    bp, sp, d = x_prompt.shape
    bs, ss, _ = x_sample.shape
    assert ss == 1
    depth = norm_gain.shape[0]
    conv_dim = conv_w.shape[2]
    att_dim = (w_in.shape[2] - 2 * conv_dim) // 3
    mem = mem_prompt.shape[1]
    mem_dim = w_ck.shape[2]
    n_pages, page = page_table.shape[1], cache_k.shape[2]
    past_len = n_pages * page
    pages_per_block = MOBA_BLOCK // page
    att_heads = att_dim // ATT_HD
    nb_past = past_len // MOBA_BLOCK
    assert past_len % MOBA_BLOCK == 0 and nb_past >= MOBA_TOPK and sp % MOBA_BLOCK == 0
    heads = d // RWKV_HD

    tm_p = _row_tile(sp, 512)
    tm_s = bs
    bf = lambda x: x.astype(BF16)

    w_mem = bf(jnp.concatenate([w for l in range(depth) for w in (w_ck[l], w_cv[l])], axis=1))
    memkv = _proj(mem_prompt.reshape(bp * mem, d), w_mem, _row_tile(bp * mem, 512))
    memkv = memkv.reshape(bp, mem, depth, 2, mem_dim)
    interleave = lambda x: x.reshape(x.shape[0], mem * MEM_HEADS, MEM_HD)
    mem_k_p = [interleave(memkv[:, :, l, 0]) for l in range(depth)]
    mem_v_p = [interleave(memkv[:, :, l, 1]) for l in range(depth)]

    cos_p, sin_p = _rope_tables(jnp.arange(sp, dtype=jnp.int32))
    cos_s, sin_s = _rope_tables(jnp.full((tm_s,), past_len, dtype=jnp.int32))

    hp = x_prompt.reshape(bp * sp, d)
    hs = x_sample.reshape(bs, d)
    k_p, v_p, k_s, v_s, conv_p, conv_s, shift_p, shift_s, wkv_p, wkv_s = ([] for _ in range(10))

    for l in range(depth):
        gains = norm_gain[l]
        g0 = gains[0:1]
        tail_w = (gains, bf(w_cq[l]))
        tail_w2 = (bf(w_co[l]), bf(w_up[l]), bf(w_down[l]))
        mk_s = cache_mem_k.reshape(depth, bs, mem * MEM_HEADS, MEM_HD)
        mv_s = cache_mem_v.reshape(depth, bs, mem * MEM_HEADS, MEM_HD)
        if l % 2 == 0:
            e = l // 2
            w_in_b = bf(w_in[e])
            w_o = bf(w_out[e])
            cb, lg, lb = conv_b[e][None], conv_ln_g[e][None], conv_ln_b[e][None]

            c, u_last, q, kt, vt = _front0_seq(hp, g0, w_in_b, cos_p, sin_p, conv_w[e], cb, lg, lb, att_dim, tm_p, sp)
            att = _moba_prompt(q.reshape(bp, sp, att_dim), kt, vt)
            hp = _tail(hp, [c, att.reshape(bp * sp, att_dim)],
                       [w_o[:conv_dim], w_o[conv_dim:]], *tail_w, mem_k_p[l], mem_v_p[l], *tail_w2, tm_p, sp)
            token_major = lambda x: jnp.transpose(x.reshape(bp, att_heads, ATT_HD, sp), (0, 3, 1, 2))
            k_p.append(token_major(kt))
            v_p.append(token_major(vt))
            conv_p.append(u_last[:, HALO - (CONV_WIDTH - 1):])

            u, q, k, v = _front0_step(hs, g0, w_in_b, cos_s, sin_s, conv_dim, att_dim)
            c = _conv_step(jnp.swapaxes(state_conv[e], 0, 1), u, conv_w[e], cb, lg, lb)
            ckt = jnp.transpose(cache_k[e], (0, 2, 3, 1))
            cvt = jnp.transpose(cache_v[e], (0, 2, 3, 1))
            pages_per_step = pages_per_block * _largest_divisor(nb_past, GATE_BLOCKS_PER_STEP)
            col = lambda x: x.reshape(bs, att_heads, ATT_HD, 1)
            sel = _moba_gate(page_table, col(q), ckt, nb_past, pages_per_block, pages_per_step)
            att = _moba_decode(page_table, sel[:, :, :MOBA_TOPK], col(q), col(k), col(v), ckt, cvt,
                               pages_per_block).reshape(bs, att_dim)
            hs = _tail(hs, [c, att], [w_o[:conv_dim], w_o[conv_dim:]], *tail_w, mk_s, mv_s, *tail_w2, tm_s, 1, l)
            k_s.append(k.reshape(bs, 1, att_dim // ATT_HD, ATT_HD))
            v_s.append(v.reshape(bs, 1, att_dim // ATT_HD, ATT_HD))
            conv_s.append(jnp.concatenate([state_conv[e][:, 1:], u[:, None]], axis=1))
        else:
            o = l // 2
            lora = LANES
            gl = MXU_DIM
            feat = jnp.arange(min(d, MXU_DIM)) // RWKV_HD
            prm = dict(
                mu=rwkv_mu[o], w_r=bf(rwkv_w_r[o]), w_k=bf(rwkv_w_k[o]), w_v=bf(rwkv_w_v[o]),
                w0=rwkv_w0[o][None], w1=bf(_pad_cols(rwkv_w1[o], lora)), w2=bf(_pad_rows(rwkv_w2[o], lora)),
                a0=rwkv_a0[o][None], a1=bf(_pad_cols(rwkv_a1[o], lora)), a2=bf(_pad_rows(rwkv_a2[o], lora)),
                g1=bf(_pad_cols(rwkv_g1[o], gl)), g2=bf(_pad_rows(rwkv_g2[o], gl)),
                k_k=rwkv_k_k[o][None], k_a=rwkv_k_a[o][None],
                seg=(feat[:, None] == feat[None, :]).astype(BF16))
            r_k = rwkv_r_k[o].reshape(1, d)
            lnx_g, lnx_b = rwkv_lnx_g[o][None], rwkv_lnx_b[o][None]
            w_o = bf(rwkv_w_o[o])

            streams, shift = _front1(hp, jnp.zeros((bp, d), F32), g0, prm, tm_p, sp)
            to3 = lambda x: x.reshape(bp, sp, d)
            gated, s_t = _scan([to3(x) for x in streams[:6]], to3(streams[6]),
                               jnp.zeros((bp, heads, RWKV_HD, RWKV_HD), F32), r_k, lnx_g, lnx_b)
            hp = _tail(hp, [gated.reshape(bp * sp, d)], [w_o], *tail_w, mem_k_p[l], mem_v_p[l], *tail_w2, tm_p, sp)
            shift_p.append(shift)
            wkv_p.append(s_t)

            streams, shift = _front1(hs, state_shift[o], g0, prm, tm_s, 1)
            gated, s_t = _wkv_step(streams[:6], streams[6], state_wkv[o], r_k, lnx_g, lnx_b)
            hs = _tail(hs, [gated], [w_o], *tail_w, mk_s, mv_s, *tail_w2, tm_s, 1, l)
            shift_s.append(shift)
            wkv_s.append(s_t)

    mem_shape = (bp, mem, MEM_HEADS, mem_dim // MEM_HEADS)
    return (hp.reshape(bp, sp, d), hs.reshape(bs, 1, d),
            jnp.stack(k_p), jnp.stack(v_p), jnp.stack(k_s), jnp.stack(v_s),
            jnp.stack(conv_p), jnp.stack(conv_s), jnp.stack(shift_p), jnp.stack(shift_s),
            jnp.stack(wkv_p), jnp.stack(wkv_s),
            jnp.stack([x.reshape(mem_shape) for x in mem_k_p]), jnp.stack([x.reshape(mem_shape) for x in mem_v_p]))
```
